```python
import math
import jax
import jax.numpy as jnp
from jax import lax
import numpy as np

D_MODEL = 2048
BATCH = 4
SEQ = 2048
DEPTH = 1

CTX_LEN = 256
GRID_W = 64
HEAD_DIM = 128
A_HEADS = D_MODEL // HEAD_DIM
KV_HEADS = A_HEADS // 4
Q_BLOCK = 128
ROPE_THETA = 10000.0
ROPE_PAIRS_AXIS = HEAD_DIM // 4
M_HEADS = 4
MV_DIM = D_MODEL // M_HEADS
MQK_DIM = MV_DIM // 2
M_CHUNK = 64
N_EXPERTS = 32
TOP_K = 4
D_FF = D_MODEL
SWIGLU_LIMIT = 7.0
SWIGLU_ALPHA = 1.702
EXPERT_BLOCK = 128
EPS = 1e-6

IN_SPLITS = (
    M_HEADS * MQK_DIM,
    M_HEADS * MQK_DIM,
    M_HEADS * MV_DIM,
    M_HEADS * MV_DIM,
    4 * M_HEADS,
    A_HEADS * HEAD_DIM,
    KV_HEADS * HEAD_DIM,
    KV_HEADS * HEAD_DIM,
    2 * D_MODEL,
)
F_IN = sum(IN_SPLITS)

kernel_name = 'hybrid_mlstm_gqa_moe_diffusion_block'


def rmsnorm(x, g):
    xf = x.astype(jnp.float32)
    y = xf * lax.rsqrt(jnp.mean(xf * xf, axis=-1, keepdims=True) + EPS)
    return (y * g.astype(jnp.float32)).astype(x.dtype)


def split_in(p):
    parts, start = [], 0
    for size in IN_SPLITS:
        parts.append(p[..., start:start + size])
        start += size
    return parts


def axial_rope_tables(n_tokens):
    rows = n_tokens // GRID_W
    row_ids = jnp.repeat(jnp.arange(rows), GRID_W).astype(jnp.float32)
    col_ids = jnp.tile(jnp.arange(GRID_W), rows).astype(jnp.float32)
    freqs = jnp.exp(-math.log(ROPE_THETA) * jnp.arange(ROPE_PAIRS_AXIS, dtype=jnp.float32) / ROPE_PAIRS_AXIS)
    ang = jnp.concatenate([row_ids[:, None] * freqs, col_ids[:, None] * freqs], axis=-1)
    return jnp.cos(ang), jnp.sin(ang)


def apply_rope(x, cos, sin):
    xf = x.astype(jnp.float32).reshape(x.shape[:-1] + (HEAD_DIM // 2, 2))
    x0, x1 = xf[..., 0], xf[..., 1]
    co, si = cos[None, :, None, :], sin[None, :, None, :]
    out = jnp.stack([x0 * co - x1 * si, x0 * si + x1 * co], axis=-1)
    return out.reshape(x.shape).astype(x.dtype)


def blocked_attention(q, k, v):
    n_b, n_t = q.shape[:2]
    n_blk = n_t // Q_BLOCK
    qb = q.reshape(n_b, n_blk, Q_BLOCK, KV_HEADS, A_HEADS // KV_HEADS, HEAD_DIM).transpose(1, 0, 2, 3, 4, 5)
    scale = HEAD_DIM ** -0.5

    def one_block(qi):
        s = jnp.einsum('bqkgd,bnkd->bkgqn', qi, k, preferred_element_type=jnp.float32) * scale
        p = jax.nn.softmax(s, axis=-1).astype(v.dtype)
        return jnp.einsum('bkgqn,bnkd->bqkgd', p, v)

    o = lax.map(one_block, qb)
    return o.transpose(1, 0, 2, 3, 4, 5).reshape(n_b, n_t, A_HEADS * HEAD_DIM)


def mlstm_inputs(pp):
    n_b, n_t = pp[0].shape[:2]

    def heads(a, d):
        return a.astype(jnp.float32).reshape(n_b, n_t, M_HEADS, d).transpose(0, 2, 1, 3)

    q = heads(pp[0], MQK_DIM) * (MQK_DIM ** -0.5)
    k = heads(pp[1], MQK_DIM)
    v = heads(pp[2], MV_DIM)
    gates = pp[4].astype(jnp.float32).reshape(n_b, n_t, 4, M_HEADS).transpose(2, 0, 3, 1)
    i_f, f_f, i_b, f_b = gates[0], gates[1], gates[2], gates[3]
    return (q, k, v, i_f, jax.nn.log_sigmoid(f_f)), (q, k, v, i_b, jax.nn.log_sigmoid(f_b))


def mlstm_chunkwise(q, k, v, ig, lf, state):
    n_b, n_h, n_t, _ = q.shape
    n_chunk = n_t // M_CHUNK

    def chunks(a):
        return jnp.moveaxis(a.reshape(a.shape[:2] + (n_chunk, M_CHUNK) + a.shape[3:]), 2, 0)

    tril = jnp.tril(jnp.ones((M_CHUNK, M_CHUNK), dtype=bool))

    def step(carry, inp):
        c_mat, n_vec, m = carry
        qc, kc, vc, igc, lfc = inp
        b = jnp.cumsum(lfc, axis=-1)
        d_intra = jnp.where(tril, b[..., :, None] - b[..., None, :] + igc[..., None, :], -jnp.inf)
        d_inter = b + m[..., None]
        m_t = jnp.maximum(d_inter, jnp.max(d_intra, axis=-1))
        s = jnp.einsum('bhtd,bhsd->bhts', qc, kc) * jnp.exp(d_intra - m_t[..., None])
        w_inter = jnp.exp(d_inter - m_t)
        num = jnp.einsum('bhts,bhsv->bhtv', s, vc) + w_inter[..., None] * jnp.einsum('bhtd,bhvd->bhtv', qc, c_mat)
        den = jnp.sum(s, axis=-1) + w_inter * jnp.einsum('bhtd,bhd->bht', qc, n_vec)
        h = num / jnp.maximum(jnp.abs(den), jnp.exp(-m_t))[..., None]
        b_end = b[..., -1]
        g = b_end[..., None] - b + igc
        m_new = jnp.maximum(b_end + m, jnp.max(g, axis=-1))
        w_s = jnp.exp(g - m_new[..., None])
        w_c = jnp.exp(b_end + m - m_new)
        c_new = w_c[..., None, None] * c_mat + jnp.einsum('bhs,bhsv,bhsd->bhvd', w_s, vc, kc)
        n_new = w_c[..., None] * n_vec + jnp.einsum('bhs,bhsd->bhd', w_s, kc)
        return (c_new, n_new, m_new), h

    state, h = lax.scan(step, state, tuple(chunks(a) for a in (q, k, v, ig, lf)))
    h = jnp.moveaxis(h, 0, 2).reshape(n_b, n_h, n_t, MV_DIM)
    return h, state


def mlstm_final_state(k, v, ig, lf):
    b = jnp.cumsum(lf, axis=-1)
    b_end = b[..., -1]
    g = b_end[..., None] - b + ig
    m = jnp.maximum(b_end, jnp.max(g, axis=-1))
    w = jnp.exp(g - m[..., None])
    return (jnp.einsum('bhs,bhsv,bhsd->bhvd', w, v, k), jnp.einsum('bhs,bhsd->bhd', w, k), m)


def mlstm_direction(lat, ctx, need_ctx_out):
    if need_ctx_out:
        kc = ctx[1]
        zero = (jnp.zeros(kc.shape[:2] + (MV_DIM, MQK_DIM), jnp.float32),
                jnp.zeros(kc.shape[:2] + (MQK_DIM,), jnp.float32),
                jnp.zeros(kc.shape[:2], jnp.float32))
        h_ctx, state = mlstm_chunkwise(ctx[0], ctx[1], ctx[2], ctx[3], ctx[4], zero)
    else:
        h_ctx, state = None, mlstm_final_state(ctx[1], ctx[2], ctx[3], ctx[4])
    h_lat, _ = mlstm_chunkwise(lat[0], lat[1], lat[2], lat[3], lat[4], state)
    return h_lat, h_ctx


def flip_time(tup):
    return tuple(jnp.flip(a, axis=2) for a in tup)


def mlstm_readout(h, o_pre, g_mh):
    n_b, _, n_t, _ = h.shape
    y = rmsnorm(h, g_mh.reshape(M_HEADS, 1, MV_DIM))
    y = y.transpose(0, 2, 1, 3).reshape(n_b, n_t, M_HEADS * MV_DIM).astype(o_pre.dtype)
    return y * jax.nn.sigmoid(o_pre)


def merge_branches(gate_pre, m_out, a_out, w_br_m, w_br_a, w_out):
    g_m, g_a = jnp.split(jax.nn.sigmoid(gate_pre), 2, axis=-1)
    return (g_m * (m_out @ w_br_m) + g_a * (a_out @ w_br_a)) @ w_out


def moe_ffn(u, w_router, b_router, w_gu, b_gu, w_dn, b_dn):
    n_tok = u.shape[0]
    logits = (u @ w_router + b_router).astype(jnp.float32)
    top_logit, top_idx = lax.top_k(logits, TOP_K)
    top_w = jax.nn.softmax(top_logit, axis=-1)
    n_assign = n_tok * TOP_K
    exp_ids = top_idx.reshape(n_assign)
    tok_ids = jnp.repeat(jnp.arange(n_tok, dtype=jnp.int32), TOP_K)
    order = jnp.argsort(exp_ids)
    exp_sorted = exp_ids[order]
    counts = jnp.bincount(exp_ids, length=N_EXPERTS)
    padded = (counts + EXPERT_BLOCK - 1) // EXPERT_BLOCK * EXPERT_BLOCK
    start = jnp.cumsum(counts) - counts
    pad_end = jnp.cumsum(padded)
    pad_start = pad_end - padded
    slot = pad_start[exp_sorted] + jnp.arange(n_assign, dtype=jnp.int32) - start[exp_sorted]
    n_blocks = (n_assign + N_EXPERTS * (EXPERT_BLOCK - 1) + EXPERT_BLOCK - 1) // EXPERT_BLOCK
    n_slots = n_blocks * EXPERT_BLOCK
    slot_tok = jnp.zeros((n_slots,), jnp.int32).at[slot].set(tok_ids[order])
    slot_w = jnp.zeros((n_slots,), jnp.float32).at[slot].set(top_w.reshape(n_assign)[order])
    block_exp = jnp.minimum(jnp.searchsorted(pad_end, jnp.arange(n_blocks) * EXPERT_BLOCK, side='right'), N_EXPERTS - 1)
    xs = u[slot_tok].reshape(n_blocks, EXPERT_BLOCK, u.shape[1])

    def expert_block(args):
        xb, e = args
        gu = xb @ w_gu[e] + b_gu[e]
        gate = jnp.minimum(gu[:, :D_FF], SWIGLU_LIMIT)
        up = jnp.clip(gu[:, D_FF:], -SWIGLU_LIMIT, SWIGLU_LIMIT)
        hid = (up + 1) * gate * jax.nn.sigmoid(SWIGLU_ALPHA * gate)
        return hid @ w_dn[e] + b_dn[e]

    ys = lax.map(expert_block, (xs, block_exp)).reshape(n_slots, u.shape[1])
    return jnp.zeros_like(u).at[slot_tok].add(ys * slot_w[:, None].astype(ys.dtype))


def layer(hx, hc, c, c_ctx, need_ctx_out, cos, sin, w_mod, b_mod, g_norm1, g_norm2, w_in, b_in,
          g_q, g_k, g_mh, w_br_m, w_br_a, w_out, w_router, b_router, w_gu, b_gu, w_dn, b_dn):
    n_b, n_t, d = hx.shape
    n_c = hc.shape[1]
    mod_x = jax.nn.silu(c) @ w_mod + b_mod
    mod_c = jax.nn.silu(c_ctx) @ w_mod + b_mod
    sh1x, sc1x, gt1x, sh2x, sc2x, gt2x = [m[:, None, :] for m in jnp.split(mod_x, 6, axis=-1)]
    sh1c, sc1c, gt1c, sh2c, sc2c, gt2c = jnp.split(mod_c, 6, axis=-1)

    ux = rmsnorm(hx, g_norm1) * (1 + sc1x) + sh1x
    uc = rmsnorm(hc, g_norm1) * (1 + sc1c) + sh1c
    px = split_in(ux @ w_in + b_in)
    pc = split_in(uc @ w_in + b_in)

    qx = apply_rope(rmsnorm(px[5].reshape(n_b, n_t, A_HEADS, HEAD_DIM), g_q), cos, sin)
    kx = apply_rope(rmsnorm(px[6].reshape(n_b, n_t, KV_HEADS, HEAD_DIM), g_k), cos, sin)
    vx = px[7].reshape(n_b, n_t, KV_HEADS, HEAD_DIM)
    kc = rmsnorm(pc[6].reshape(n_b, n_c, KV_HEADS, HEAD_DIM), g_k)
    vc = pc[7].reshape(n_b, n_c, KV_HEADS, HEAD_DIM)
    ax = blocked_attention(qx, jnp.concatenate([kx, kc], axis=1), jnp.concatenate([vx, vc], axis=1))

    lat_f, lat_b = mlstm_inputs(px)
    ctx_f, ctx_b = mlstm_inputs(pc)
    hx_f, hc_f = mlstm_direction(lat_f, ctx_f, need_ctx_out)
    hx_b, hc_b = mlstm_direction(flip_time(lat_b), flip_time(ctx_b), need_ctx_out)
    mx = mlstm_readout(hx_f + jnp.flip(hx_b, axis=2), px[3], g_mh)

    hx_new = hx + gt1x * merge_branches(px[8], mx, ax, w_br_m, w_br_a, w_out)
    u2x = rmsnorm(hx_new, g_norm2) * (1 + sc2x) + sh2x
    hx_new = hx_new + gt2x * moe_ffn(u2x.reshape(-1, d), w_router, b_router, w_gu, b_gu, w_dn, b_dn).reshape(hx.shape)

    hc_new = hc
    if need_ctx_out:
        qc = rmsnorm(pc[5].reshape(n_b, n_c, A_HEADS, HEAD_DIM), g_q)
        ac = blocked_attention(qc, kc, vc)
        mc = mlstm_readout(hc_f + jnp.flip(hc_b, axis=2), pc[3], g_mh)
        hc_new = hc + gt1c * merge_branches(pc[8], mc, ac, w_br_m, w_br_a, w_out)
        u2c = rmsnorm(hc_new, g_norm2) * (1 + sc2c) + sh2c
        hc_new = hc_new + gt2c * moe_ffn(u2c.reshape(-1, d), w_router, b_router, w_gu, b_gu, w_dn, b_dn).reshape(hc.shape)
    return hx_new, hc_new


def setup_inputs(seed: int = 0) -> dict:
    key = jax.random.key(seed)
    ks = jax.random.split(key, 24)
    f32 = jnp.float32
    D = D_MODEL
    nrm = lambda k, shape, s: jax.random.normal(k, shape, f32) * s
    f_off = sum(IN_SPLITS[:4])
    forget_bias = 3.0 + 3.0 * jnp.linspace(0.0, 1.0, M_HEADS)
    b_in = nrm(ks[9], (DEPTH, F_IN), 0.01)
    b_in = b_in.at[:, f_off + M_HEADS:f_off + 2 * M_HEADS].add(forget_bias)
    b_in = b_in.at[:, f_off + 3 * M_HEADS:f_off + 4 * M_HEADS].add(forget_bias)
    return {
        'x': nrm(ks[0], (BATCH, SEQ, D), 1.0),
        'c': nrm(ks[1], (BATCH, D), 1.0),
        'ctx': nrm(ks[2], (BATCH, CTX_LEN, D), 1.0),
        'c_ctx': nrm(ks[3], (D,), 1.0),
        'w_mod': nrm(ks[4], (DEPTH, D, 6 * D), 0.5 * D ** -0.5),
        'b_mod': nrm(ks[5], (DEPTH, 6 * D), 0.01),
        'g_norm1': 1.0 + nrm(ks[6], (DEPTH, D), 0.01),
        'g_norm2': 1.0 + nrm(ks[7], (DEPTH, D), 0.01),
        'w_in': nrm(ks[8], (DEPTH, D, F_IN), D ** -0.5),
        'b_in': b_in,
        'g_q': 1.0 + nrm(ks[10], (DEPTH, HEAD_DIM), 0.01),
        'g_k': 1.0 + nrm(ks[11], (DEPTH, HEAD_DIM), 0.01),
        'g_mh': 1.0 + nrm(ks[12], (DEPTH, M_HEADS * MV_DIM), 0.01),
        'w_br_m': nrm(ks[13], (DEPTH, M_HEADS * MV_DIM, D), (M_HEADS * MV_DIM) ** -0.5),
        'w_br_a': nrm(ks[14], (DEPTH, A_HEADS * HEAD_DIM, D), (A_HEADS * HEAD_DIM) ** -0.5),
        'w_out': nrm(ks[15], (DEPTH, D, D), D ** -0.5),
        'w_router': nrm(ks[16], (DEPTH, D, N_EXPERTS), D ** -0.5),
        'b_router': nrm(ks[17], (DEPTH, N_EXPERTS), 0.01),
        'w_gu': nrm(ks[18], (DEPTH, N_EXPERTS, D, 2 * D_FF), D ** -0.5),
        'b_gu': nrm(ks[19], (DEPTH, N_EXPERTS, 2 * D_FF), 0.01),
        'w_dn': nrm(ks[20], (DEPTH, N_EXPERTS, D_FF, D), D_FF ** -0.5),
        'b_dn': nrm(ks[21], (DEPTH, N_EXPERTS, D), 0.01),
    }


def reference(x, c, ctx, c_ctx, w_mod, b_mod, g_norm1, g_norm2, w_in, b_in, g_q, g_k, g_mh,
              w_br_m, w_br_a, w_out, w_router, b_router, w_gu, b_gu, w_dn, b_dn):
    n_t = x.shape[1]
    cos, sin = axial_rope_tables(n_t)
    hx, hc = x, ctx
    for l in range(DEPTH):
        hx, hc = layer(hx, hc, c, c_ctx, l < DEPTH - 1, cos, sin, w_mod[l], b_mod[l], g_norm1[l], g_norm2[l],
                       w_in[l], b_in[l], g_q[l], g_k[l], g_mh[l], w_br_m[l], w_br_a[l], w_out[l],
                       w_router[l], b_router[l], w_gu[l], b_gu[l], w_dn[l], b_dn[l])
    return hx
```

```python
import functools
import math

import jax
import jax.numpy as jnp
from jax import lax
from jax.experimental import pallas as pl
from jax.experimental.pallas import tpu as pltpu

D_MODEL = 2048
GRID_W = 64
HEAD_DIM = 128
A_HEADS = 16
KV_HEADS = 4
Q_GROUP = A_HEADS // KV_HEADS
ROPE_THETA = 10000.0
ROPE_PAIRS_AXIS = HEAD_DIM // 4
M_HEADS = 4
MV_DIM = D_MODEL // M_HEADS
MQK_DIM = MV_DIM // 2
N_EXPERTS = 32
TOP_K = 4
D_FF = D_MODEL
SWIGLU_LIMIT = 7.0
SWIGLU_ALPHA = 1.702
EPS = 1e-6

_O_MQ, _O_MK, _O_MV, _O_MO = 0, 1024, 2048, 4096
_O_GATES = 6144
_O_AQ, _O_AK, _O_AV, _O_MG = 6160, 8208, 8720, 9232
_F_IN = 13328
_P_MQ, _P_MK, _P_MV, _P_MO, _P_AQ, _P_AK, _P_AV, _P_GM, _P_GA = (
    0, 1024, 2048, 4096, 6144, 8192, 8704, 9216, 11264)
_P_COLS = 13312
_C_MK, _C_MV, _C_AK, _C_AV = 0, 1024, 3072, 3584
_C_COLS = 4096

LANES = 128
M_CHUNK = 256
EXPERT_ROWS = 256
NEG_BIG = -1e30
VMEM_LIMIT = 56 * 1024 * 1024

_HI = lax.Precision.HIGHEST


def _cparams(sem, vmem=VMEM_LIMIT):
    return pltpu.CompilerParams(dimension_semantics=sem, vmem_limit_bytes=vmem)


def _mod_kernel(c_ref, w_ref, b_ref, o_ref):
    c = c_ref[...]
    a = c * jax.nn.sigmoid(c)
    o_ref[...] = lax.dot_general(a, w_ref[...], (((1,), (0,)), ((), ())), precision=_HI,
                                 preferred_element_type=jnp.float32) + b_ref[...]


def _modulation(c8, w_mod, b_mod):
    d, n = w_mod.shape
    tn = 1024
    return pl.pallas_call(
        _mod_kernel,
        grid=(n // tn,),
        in_specs=[pl.BlockSpec((8, d), lambda j: (0, 0)),
                  pl.BlockSpec((d, tn), lambda j: (0, j)),
                  pl.BlockSpec((1, tn), lambda j: (0, j))],
        out_specs=pl.BlockSpec((8, tn), lambda j: (0, j)),
        out_shape=jax.ShapeDtypeStruct((8, n), jnp.float32),
        compiler_params=_cparams(("arbitrary",)),
        name="modulation",
    )(c8, w_mod, b_mod.reshape(1, n))


def _proj_kernel(x_ref, g_ref, sc_ref, sh_ref, w_ref, b_ref, wg_ref, bg_ref, o_ref, og_ref, u_ref):
    @pl.when(pl.program_id(1) == 0)
    def _():
        x = x_ref[...]
        y = x * lax.rsqrt(jnp.mean(x * x, axis=-1, keepdims=True) + EPS) * g_ref[...]
        u = y * (1.0 + sc_ref[0]) + sh_ref[0]
        u_ref[...] = u.astype(jnp.bfloat16)
        og_ref[...] = lax.dot_general(u, wg_ref[...], (((1,), (0,)), ((), ())), precision=_HI,
                                      preferred_element_type=jnp.float32) + bg_ref[...]

    o_ref[...] = (jnp.dot(u_ref[...], w_ref[...], preferred_element_type=jnp.float32)
                  + b_ref[...]).astype(o_ref.dtype)


def _in_projection(x2, g, sc, sh, w, b, wg, bg, rows_per_mod, tm=512, tn=1024):
    n, d = x2.shape
    nc = w.shape[1]
    tiles_per_mod = rows_per_mod // tm
    return pl.pallas_call(
        _proj_kernel,
        grid=(n // tm, nc // tn),
        in_specs=[pl.BlockSpec((tm, d), lambda i, j: (i, 0)),
                  pl.BlockSpec((1, d), lambda i, j: (0, 0)),
                  pl.BlockSpec((1, 1, d), lambda i, j: (i // tiles_per_mod, 0, 0)),
                  pl.BlockSpec((1, 1, d), lambda i, j: (i // tiles_per_mod, 0, 0)),
                  pl.BlockSpec((d, tn), lambda i, j: (0, j)),
                  pl.BlockSpec((1, tn), lambda i, j: (0, j)),
                  pl.BlockSpec((d, LANES), lambda i, j: (0, 0)),
                  pl.BlockSpec((1, LANES), lambda i, j: (0, 0))],
        out_specs=[pl.BlockSpec((tm, tn), lambda i, j: (i, j)),
                   pl.BlockSpec((tm, LANES), lambda i, j: (i, 0))],
        out_shape=[jax.ShapeDtypeStruct((n, nc), jnp.bfloat16),
                   jax.ShapeDtypeStruct((n, LANES), jnp.float32)],
        scratch_shapes=[pltpu.VMEM((tm, d), jnp.bfloat16)],
        compiler_params=_cparams(("arbitrary", "arbitrary")),
        name="in_projection",
    )(x2, g, sc, sh, w, b, wg, bg)


def _rms_head(x, g):
    return x * lax.rsqrt(jnp.mean(x * x, axis=-1, keepdims=True) + EPS) * g


def _rope(x, cos_e, sin_s):
    lane = lax.broadcasted_iota(jnp.int32, x.shape, 1)
    swapped = jnp.where(lane % 2 == 0, pltpu.roll(x, LANES - 1, 1), pltpu.roll(x, 1, 1))
    return x * cos_e + swapped * sin_s


def _attn_kernel(q_ref, kx_ref, vx_ref, kc_ref, vc_ref, cosq_ref, sinq_ref, cosk_ref, sink_ref,
                 gq_ref, gk_ref, o_ref, k_s, v_s):
    n_t = kx_ref.shape[1]

    @pl.when(pl.program_id(2) == 0)
    def _():
        kx = _rms_head(kx_ref[0].astype(jnp.float32), gk_ref[...])
        k_s[0:n_t, :] = _rope(kx, cosk_ref[...], sink_ref[...]).astype(jnp.bfloat16)
        k_s[n_t:, :] = _rms_head(kc_ref[0].astype(jnp.float32), gk_ref[...]).astype(jnp.bfloat16)
        v_s[0:n_t, :] = vx_ref[0]
        v_s[n_t:, :] = vc_ref[0]

    scale = HEAD_DIM ** -0.5
    kk = k_s[...]
    vv = v_s[...]
    for g in range(Q_GROUP):
        q = q_ref[0, :, g * HEAD_DIM:(g + 1) * HEAD_DIM].astype(jnp.float32)
        q = _rope(_rms_head(q, gq_ref[...]), cosq_ref[...], sinq_ref[...]) * scale
        s = lax.dot_general(q.astype(jnp.bfloat16), kk, (((1,), (1,)), ((), ())),
                            preferred_element_type=jnp.float32)
        p = jnp.exp(s - jnp.max(s, axis=-1, keepdims=True))
        l = jnp.sum(p, axis=-1, keepdims=True)
        o = jnp.dot(p.astype(jnp.bfloat16), vv, preferred_element_type=jnp.float32) / l
        o_ref[0, :, g * HEAD_DIM:(g + 1) * HEAD_DIM] = o.astype(o_ref.dtype)


def _attention(px, pc, cos_e, sin_s, g_q, g_k, tq=256):
    n_b, n_t, _ = px.shape
    n_c = pc.shape[1]
    gw = Q_GROUP * HEAD_DIM
    return pl.pallas_call(
        _attn_kernel,
        grid=(n_b, KV_HEADS, n_t // tq),
        in_specs=[pl.BlockSpec((1, tq, gw), lambda b, h, i: (b, i, _P_AQ // gw + h)),
                  pl.BlockSpec((1, n_t, HEAD_DIM), lambda b, h, i: (b, 0, _P_AK // HEAD_DIM + h)),
                  pl.BlockSpec((1, n_t, HEAD_DIM), lambda b, h, i: (b, 0, _P_AV // HEAD_DIM + h)),
                  pl.BlockSpec((1, n_c, HEAD_DIM), lambda b, h, i: (b, 0, _C_AK // HEAD_DIM + h)),
                  pl.BlockSpec((1, n_c, HEAD_DIM), lambda b, h, i: (b, 0, _C_AV // HEAD_DIM + h)),
                  pl.BlockSpec((tq, HEAD_DIM), lambda b, h, i: (i, 0)),
                  pl.BlockSpec((tq, HEAD_DIM), lambda b, h, i: (i, 0)),
                  pl.BlockSpec((n_t, HEAD_DIM), lambda b, h, i: (0, 0)),
                  pl.BlockSpec((n_t, HEAD_DIM), lambda b, h, i: (0, 0)),
                  pl.BlockSpec((1, HEAD_DIM), lambda b, h, i: (0, 0)),
                  pl.BlockSpec((1, HEAD_DIM), lambda b, h, i: (0, 0))],
        out_specs=pl.BlockSpec((1, tq, gw), lambda b, h, i: (b, i, h)),
        out_shape=jax.ShapeDtypeStruct((n_b, n_t, A_HEADS * HEAD_DIM), jnp.bfloat16),
        scratch_shapes=[pltpu.VMEM((n_t + n_c, HEAD_DIM), jnp.bfloat16),
                        pltpu.VMEM((n_t + n_c, HEAD_DIM), jnp.bfloat16)],
        compiler_params=_cparams(("arbitrary", "arbitrary", "arbitrary")),
        name="attention",
    )(px, px, px, pc, pc, cos_e, sin_s, cos_e, sin_s, g_q, g_k)


def _log_sigmoid(x):
    return jnp.minimum(x, 0.0) - jnp.log(1.0 + jnp.exp(-jnp.abs(x)))


def _mlstm_chunk(q, k, v, i_row, f_row, i_col, f_col, ct_ref, n_ref, m_ref, backward, want_h):
    n_l = k.shape[0]
    lf_row = _log_sigmoid(f_row)
    lf_col = _log_sigmoid(f_col)
    t_idx = lax.broadcasted_iota(jnp.int32, (n_l, n_l), 0)
    s_idx = lax.broadcasted_iota(jnp.int32, (n_l, n_l), 1)
    seen = (s_idx >= t_idx) if backward else (s_idx <= t_idx)
    b_col = jnp.sum(jnp.where(seen, lf_row, 0.0), axis=1, keepdims=True)
    seen_t = (t_idx >= s_idx) if backward else (t_idx <= s_idx)
    b_row = jnp.sum(jnp.where(seen_t, lf_col, 0.0), axis=0, keepdims=True)
    b_end = jnp.sum(lf_row, axis=1, keepdims=True)
    m_prev = m_ref[...]
    ct = ct_ref[...]
    n_vec = n_ref[...]
    kf = k.astype(jnp.float32)

    h = None
    if want_h:
        d_intra = jnp.where(seen, b_col - b_row + i_row, NEG_BIG)
        d_inter = b_col + m_prev
        m_t = jnp.maximum(d_inter, jnp.max(d_intra, axis=1, keepdims=True))
        qk = lax.dot_general(q, k, (((1,), (1,)), ((), ())), preferred_element_type=jnp.float32)
        s = qk * jnp.exp(d_intra - m_t)
        w_inter = jnp.exp(d_inter - m_t)
        num = (jnp.dot(s.astype(jnp.bfloat16), v, preferred_element_type=jnp.float32)
               + w_inter * jnp.dot(q, ct.astype(jnp.bfloat16), preferred_element_type=jnp.float32))
        den = (jnp.sum(s, axis=1, keepdims=True)
               + w_inter * jnp.sum(q.astype(jnp.float32) * n_vec, axis=1, keepdims=True))
        h = num / jnp.maximum(jnp.abs(den), jnp.exp(-m_t))

    g_col = b_end - b_col + i_col
    m_new = jnp.maximum(b_end + m_prev, jnp.max(g_col, axis=0, keepdims=True))
    w_s = jnp.exp(g_col - m_new)
    w_c = jnp.exp(b_end + m_prev - m_new)
    wv = (w_s * v.astype(jnp.float32)).astype(jnp.bfloat16)
    ct_ref[...] = w_c * ct + lax.dot_general(k, wv, (((0,), (0,)), ((), ())),
                                             preferred_element_type=jnp.float32)
    n_ref[...] = w_c * n_vec + jnp.sum(w_s * kf, axis=0, keepdims=True)
    m_ref[...] = m_new
    return h


def _mlstm_kernel(backward, *refs):
    if backward:
        (q_ref, k_ref, v_ref, kc_ref, vc_ref, gr_ref, gc_ref, grc_ref, gcc_ref,
         hf_ref, op_ref, gmh_ref, o_ref, ct_ref, n_ref, m_ref) = refs
    else:
        (q_ref, k_ref, v_ref, kc_ref, vc_ref, gr_ref, gc_ref, grc_ref, gcc_ref,
         o_ref, ct_ref, n_ref, m_ref) = refs
    gi, gf = (2, 3) if backward else (0, 1)
    step = pl.program_id(2)

    @pl.when(step == 0)
    def _():
        ct_ref[...] = jnp.zeros_like(ct_ref)
        n_ref[...] = jnp.zeros_like(n_ref)
        m_ref[...] = jnp.zeros_like(m_ref)
        _mlstm_chunk(None, kc_ref[0], vc_ref[0],
                     grc_ref[0, 0, gi:gi + 1, :], grc_ref[0, 0, gf:gf + 1, :],
                     gcc_ref[0, 0, :, gi:gi + 1], gcc_ref[0, 0, :, gf:gf + 1],
                     ct_ref, n_ref, m_ref, backward, False)

    @pl.when(step > 0)
    def _():
        q = (q_ref[0].astype(jnp.float32) * (MQK_DIM ** -0.5)).astype(jnp.bfloat16)
        h = _mlstm_chunk(q, k_ref[0], v_ref[0],
                         gr_ref[0, 0, gi:gi + 1, :], gr_ref[0, 0, gf:gf + 1, :],
                         gc_ref[0, 0, :, gi:gi + 1], gc_ref[0, 0, :, gf:gf + 1],
                         ct_ref, n_ref, m_ref, backward, True)
        if backward:
            ht = h + hf_ref[0]
            y = ht * lax.rsqrt(jnp.mean(ht * ht, axis=-1, keepdims=True) + EPS) * gmh_ref[...]
            o_ref[0] = (y * jax.nn.sigmoid(op_ref[0].astype(jnp.float32))).astype(o_ref.dtype)
        else:
            o_ref[0] = h


def _mlstm(px, pc, g_rows, g_cols, g_rows_c, g_cols_c, backward, h_fwd=None, g_mh=None):
    n_b, n_t, _ = px.shape
    n_c = pc.shape[1]
    n_l = M_CHUNK
    assert n_c == n_l and n_t % n_l == 0
    n_chunk = n_t // n_l

    if backward:
        def cidx(s):
            return jnp.minimum(n_chunk - s, n_chunk - 1)
    else:
        def cidx(s):
            return jnp.maximum(s - 1, 0)

    in_specs = [
        pl.BlockSpec((1, n_l, MQK_DIM), lambda b, h, s: (b, cidx(s), _P_MQ // MQK_DIM + h)),
        pl.BlockSpec((1, n_l, MQK_DIM), lambda b, h, s: (b, cidx(s), _P_MK // MQK_DIM + h)),
        pl.BlockSpec((1, n_l, MV_DIM), lambda b, h, s: (b, cidx(s), _P_MV // MV_DIM + h)),
        pl.BlockSpec((1, n_c, MQK_DIM), lambda b, h, s: (b, 0, _C_MK // MQK_DIM + h)),
        pl.BlockSpec((1, n_c, MV_DIM), lambda b, h, s: (b, 0, _C_MV // MV_DIM + h)),
        pl.BlockSpec((1, 1, 4, n_l), lambda b, h, s: (b, h, 0, cidx(s))),
        pl.BlockSpec((1, 1, n_l, 4), lambda b, h, s: (b, h, cidx(s), 0)),
        pl.BlockSpec((1, 1, 4, n_c), lambda b, h, s: (b, h, 0, 0)),
        pl.BlockSpec((1, 1, n_c, 4), lambda b, h, s: (b, h, 0, 0)),
    ]
    args = [px, px, px, pc, pc, g_rows, g_cols, g_rows_c, g_cols_c]
    if backward:
        in_specs += [
            pl.BlockSpec((1, n_l, MV_DIM), lambda b, h, s: (b, cidx(s), h)),
            pl.BlockSpec((1, n_l, MV_DIM), lambda b, h, s: (b, cidx(s), _P_MO // MV_DIM + h)),
            pl.BlockSpec((1, MV_DIM), lambda b, h, s: (0, h)),
        ]
        args += [h_fwd, px, g_mh]
        out_dtype = jnp.bfloat16
    else:
        out_dtype = jnp.float32
    return pl.pallas_call(
        functools.partial(_mlstm_kernel, backward),
        grid=(n_b, M_HEADS, n_chunk + 1),
        in_specs=in_specs,
        out_specs=pl.BlockSpec((1, n_l, MV_DIM), lambda b, h, s: (b, cidx(s), h)),
        out_shape=jax.ShapeDtypeStruct((n_b, n_t, M_HEADS * MV_DIM), out_dtype),
        scratch_shapes=[pltpu.VMEM((MQK_DIM, MV_DIM), jnp.float32),
                        pltpu.VMEM((1, MQK_DIM), jnp.float32),
                        pltpu.VMEM((1, 1), jnp.float32)],
        compiler_params=_cparams(("arbitrary", "arbitrary", "arbitrary")),
        name="mlstm_bwd" if backward else "mlstm_fwd",
    )(*args)


def _merge_kernel(m_ref, a_ref, gm_ref, ga_ref, wm_ref, wa_ref, o_ref):
    zm = jnp.dot(m_ref[...], wm_ref[...], preferred_element_type=jnp.float32)
    za = jnp.dot(a_ref[...], wa_ref[...], preferred_element_type=jnp.float32)
    z = (jax.nn.sigmoid(gm_ref[...].astype(jnp.float32)) * zm
         + jax.nn.sigmoid(ga_ref[...].astype(jnp.float32)) * za)
    o_ref[...] = z.astype(o_ref.dtype)


def _merge(m_out, a_out, px2, w_br_m, w_br_a, tm=512, tn=512):
    n, d = m_out.shape
    return pl.pallas_call(
        _merge_kernel,
        grid=(n // tm, d // tn),
        in_specs=[pl.BlockSpec((tm, d), lambda i, j: (i, 0)),
                  pl.BlockSpec((tm, d), lambda i, j: (i, 0)),
                  pl.BlockSpec((tm, tn), lambda i, j: (i, _P_GM // tn + j)),
                  pl.BlockSpec((tm, tn), lambda i, j: (i, _P_GA // tn + j)),
                  pl.BlockSpec((d, tn), lambda i, j: (0, j)),
                  pl.BlockSpec((d, tn), lambda i, j: (0, j))],
        out_specs=pl.BlockSpec((tm, tn), lambda i, j: (i, j)),
        out_shape=jax.ShapeDtypeStruct((n, d), jnp.bfloat16),
        compiler_params=_cparams(("arbitrary", "arbitrary")),
        name="merge",
    )(m_out, a_out, px2, px2, w_br_m, w_br_a)


def _split_bf16(x):
    hi = x.astype(jnp.bfloat16)
    lo = (x - hi.astype(jnp.float32)).astype(jnp.bfloat16)
    return hi, lo


def _outproj_router_kernel(z_ref, x_ref, wo_ref, gt_ref, g2_ref, sc_ref, sh_ref, wrh_ref, wrl_ref, br_ref,
                           hx_ref, u2_ref, idx_ref, wgt_ref, rank_ref, cnt_ref, carry_ref):
    tm = z_ref.shape[0]

    @pl.when(pl.program_id(0) == 0)
    def _():
        carry_ref[...] = jnp.zeros_like(carry_ref)

    y = jnp.dot(z_ref[...], wo_ref[...], preferred_element_type=jnp.float32)
    hx = x_ref[...] + gt_ref[0] * y
    hx_ref[...] = hx
    u2 = (hx * lax.rsqrt(jnp.mean(hx * hx, axis=-1, keepdims=True) + EPS) * g2_ref[...]
          * (1.0 + sc_ref[0]) + sh_ref[0])
    u2_ref[...] = u2

    u_hi, u_lo = _split_bf16(u2)
    logits = (jnp.dot(u_hi, wrh_ref[...], preferred_element_type=jnp.float32)
              + jnp.dot(u_lo, wrh_ref[...], preferred_element_type=jnp.float32)
              + jnp.dot(u_hi, wrl_ref[...], preferred_element_type=jnp.float32)) + br_ref[...]
    lane = lax.broadcasted_iota(jnp.int32, (tm, LANES), 1)
    logits = jnp.where(lane < N_EXPERTS, logits, NEG_BIG)

    idx_out = jnp.zeros((tm, LANES), jnp.int32)
    val_out = jnp.zeros((tm, LANES), jnp.float32)
    chosen = jnp.zeros((tm, LANES), jnp.float32)
    sel = []
    top0 = None
    for k in range(TOP_K):
        mx = jnp.max(logits, axis=-1, keepdims=True)
        ix = jnp.min(jnp.where(logits == mx, lane, LANES), axis=-1, keepdims=True)
        hit = lane == ix
        if k == 0:
            top0 = mx
        idx_out = jnp.where(lane == k, ix, idx_out)
        val_out = jnp.where(lane == k, jnp.exp(mx - top0), val_out)
        chosen = jnp.where(hit, 1.0, chosen)
        sel.append(hit)
        logits = jnp.where(hit, NEG_BIG, logits)
    idx_ref[...] = idx_out
    wgt_ref[...] = val_out / jnp.sum(val_out, axis=-1, keepdims=True)

    r_idx = lax.broadcasted_iota(jnp.int32, (tm, tm), 0)
    c_idx = lax.broadcasted_iota(jnp.int32, (tm, tm), 1)
    before = jnp.where(c_idx < r_idx, 1.0, 0.0).astype(jnp.bfloat16)
    prior = jnp.dot(before, chosen.astype(jnp.bfloat16), preferred_element_type=jnp.float32) + carry_ref[...]
    rank_out = jnp.zeros((tm, LANES), jnp.int32)
    for k in range(TOP_K):
        rk = jnp.sum(jnp.where(sel[k], prior, 0.0), axis=-1, keepdims=True)
        rank_out = jnp.where(lane == k, rk.astype(jnp.int32), rank_out)
    rank_ref[...] = rank_out
    carry_ref[...] = carry_ref[...] + jnp.sum(chosen, axis=0, keepdims=True)
    cnt_ref[...] = jnp.broadcast_to(carry_ref[...], cnt_ref.shape)


def _outproj_router(z, x2, w_out, gt1, g2, sc2, sh2, wr_hi, wr_lo, br, rows_per_mod, tm=256):
    n, d = z.shape
    tiles_per_mod = rows_per_mod // tm
    row = lambda i: (i, 0)
    fixed = lambda i: (0, 0)
    modi = lambda i: (i // tiles_per_mod, 0, 0)
    return pl.pallas_call(
        _outproj_router_kernel,
        grid=(n // tm,),
        in_specs=[pl.BlockSpec((tm, d), row),
                  pl.BlockSpec((tm, d), row),
                  pl.BlockSpec((d, d), fixed),
                  pl.BlockSpec((1, 1, d), modi),
                  pl.BlockSpec((1, d), fixed),
                  pl.BlockSpec((1, 1, d), modi),
                  pl.BlockSpec((1, 1, d), modi),
                  pl.BlockSpec((d, LANES), fixed),
                  pl.BlockSpec((d, LANES), fixed),
                  pl.BlockSpec((1, LANES), fixed)],
        out_specs=[pl.BlockSpec((tm, d), row),
                   pl.BlockSpec((tm, d), row),
                   pl.BlockSpec((tm, LANES), row),
                   pl.BlockSpec((tm, LANES), row),
                   pl.BlockSpec((tm, LANES), row),
                   pl.BlockSpec((8, LANES), fixed)],
        out_shape=[jax.ShapeDtypeStruct((n, d), jnp.float32),
                   jax.ShapeDtypeStruct((n, d), jnp.float32),
                   jax.ShapeDtypeStruct((n, LANES), jnp.int32),
                   jax.ShapeDtypeStruct((n, LANES), jnp.float32),
                   jax.ShapeDtypeStruct((n, LANES), jnp.int32),
                   jax.ShapeDtypeStruct((8, LANES), jnp.float32)],
        scratch_shapes=[pltpu.VMEM((1, LANES), jnp.float32)],
        compiler_params=_cparams(("arbitrary",)),
        name="outproj_router",
    )(z, x2, w_out, gt1, g2, sc2, sh2, wr_hi, wr_lo, br)


def _dispatch_kernel(slot_ref, u_ref, xs_in_ref, xs_ref, sem):
    del xs_in_ref
    tm = u_ref.shape[0]
    base = pl.program_id(0) * (tm * TOP_K)

    def issue(r, carry):
        for k in range(TOP_K):
            s = slot_ref[base + r * TOP_K + k]
            pltpu.make_async_copy(u_ref.at[pl.ds(r, 1)], xs_ref.at[pl.ds(s, 1)], sem).start()
        return carry

    lax.fori_loop(0, tm, issue, 0)

    def drain(r, carry):
        for k in range(TOP_K):
            pltpu.make_async_copy(u_ref.at[pl.ds(0, 1)], xs_ref.at[pl.ds(0, 1)], sem).wait()
        return carry

    lax.fori_loop(0, tm, drain, 0)


def _dispatch(slots_flat, u2, n_slots, tm=256):
    n, d = u2.shape
    xs0 = jnp.zeros((n_slots, d), jnp.float32)
    return pl.pallas_call(
        _dispatch_kernel,
        grid_spec=pltpu.PrefetchScalarGridSpec(
            num_scalar_prefetch=1,
            grid=(n // tm,),
            in_specs=[pl.BlockSpec((tm, d), lambda i, s: (i, 0)),
                      pl.BlockSpec(memory_space=pl.ANY)],
            out_specs=pl.BlockSpec(memory_space=pl.ANY),
            scratch_shapes=[pltpu.SemaphoreType.DMA(())]),
        out_shape=jax.ShapeDtypeStruct((n_slots, d), jnp.float32),
        input_output_aliases={2: 0},
        compiler_params=_cparams(("arbitrary",)),
        name="moe_dispatch",
    )(slots_flat, u2, xs0)


def _expert_up_kernel(we_ref, wf_ref, wrb_ref, wfo_ref, wrbo_ref, wvalid_ref, wfirst_ref,
                      x_ref, wg_ref, wu_ref, bg_ref, bu_ref, o_ref, wg_s, wu_s):
    w = pl.program_id(0)

    @pl.when(wfirst_ref[w] == 1)
    def _():
        wg_s[...] = wg_ref[...].astype(jnp.bfloat16)
        wu_s[...] = wu_ref[...].astype(jnp.bfloat16)

    @pl.when(wvalid_ref[w] == 0)
    def _():
        o_ref[...] = jnp.zeros_like(o_ref)

    @pl.when(wvalid_ref[w] == 1)
    def _():
        x = x_ref[...].astype(jnp.bfloat16)
        gate = jnp.dot(x, wg_s[...], preferred_element_type=jnp.float32) + bg_ref[...]
        up = jnp.dot(x, wu_s[...], preferred_element_type=jnp.float32) + bu_ref[...]
        gate = jnp.minimum(gate, SWIGLU_LIMIT)
        up = jnp.clip(up, -SWIGLU_LIMIT, SWIGLU_LIMIT)
        hid = (up + 1.0) * gate * jax.nn.sigmoid(SWIGLU_ALPHA * gate)
        o_ref[...] = hid.astype(o_ref.dtype)


def _expert_up(work, xs, w_gu, b_gu, tf=512):
    n_slots, d = xs.shape
    n_work = work[0].shape[0]
    nfc = D_FF // tf
    tb = EXPERT_ROWS
    b3 = b_gu.reshape(N_EXPERTS, 1, 2 * D_FF)
    return pl.pallas_call(
        _expert_up_kernel,
        grid_spec=pltpu.PrefetchScalarGridSpec(
            num_scalar_prefetch=7,
            grid=(n_work,),
            in_specs=[pl.BlockSpec((tb, d), lambda w, e, f, rb, *_: (rb[w], 0)),
                      pl.BlockSpec((None, d, tf), lambda w, e, f, rb, *_: (e[w], 0, f[w])),
                      pl.BlockSpec((None, d, tf), lambda w, e, f, rb, *_: (e[w], 0, nfc + f[w])),
                      pl.BlockSpec((None, 1, tf), lambda w, e, f, rb, *_: (e[w], 0, f[w])),
                      pl.BlockSpec((None, 1, tf), lambda w, e, f, rb, *_: (e[w], 0, nfc + f[w]))],
            out_specs=pl.BlockSpec((tb, tf), lambda w, e, f, rb, fo, rbo, *_: (rbo[w], fo[w])),
            scratch_shapes=[pltpu.VMEM((d, tf), jnp.bfloat16), pltpu.VMEM((d, tf), jnp.bfloat16)]),
        out_shape=jax.ShapeDtypeStruct((n_slots, D_FF), jnp.bfloat16),
        compiler_params=_cparams(("arbitrary",)),
        name="expert_up",
    )(*work, xs, w_gu, w_gu, b3, b3)


def _expert_down_kernel(we_ref, wf_ref, wrb_ref, wfo_ref, wrbo_ref, wvalid_ref, wfirst_ref,
                        h_ref, wd_ref, bd_ref, o_ref, wd_s):
    w = pl.program_id(0)

    @pl.when(wfirst_ref[w] == 1)
    def _():
        wd_s[...] = wd_ref[...].astype(jnp.bfloat16)

    @pl.when(wvalid_ref[w] == 0)
    def _():
        o_ref[...] = jnp.zeros_like(o_ref)

    @pl.when(wvalid_ref[w] == 1)
    def _():
        o_ref[...] = jnp.dot(h_ref[...], wd_s[...], preferred_element_type=jnp.float32) + bd_ref[...]


def _expert_down(work, hid, w_dn, b_dn, tn=512):
    n_slots, dff = hid.shape
    d = w_dn.shape[2]
    n_work = work[0].shape[0]
    tb = EXPERT_ROWS
    b3 = b_dn.reshape(N_EXPERTS, 1, d)
    return pl.pallas_call(
        _expert_down_kernel,
        grid_spec=pltpu.PrefetchScalarGridSpec(
            num_scalar_prefetch=7,
            grid=(n_work,),
            in_specs=[pl.BlockSpec((tb, dff), lambda w, e, f, rb, *_: (rb[w], 0)),
                      pl.BlockSpec((None, dff, tn), lambda w, e, f, rb, *_: (e[w], 0, f[w])),
                      pl.BlockSpec((None, 1, tn), lambda w, e, f, rb, *_: (e[w], 0, f[w]))],
            out_specs=pl.BlockSpec((tb, tn), lambda w, e, f, rb, fo, rbo, *_: (rbo[w], fo[w])),
            scratch_shapes=[pltpu.VMEM((dff, tn), jnp.bfloat16)]),
        out_shape=jax.ShapeDtypeStruct((n_slots, d), jnp.float32),
        compiler_params=_cparams(("arbitrary",)),
        name="expert_down",
    )(*work, hid, w_dn, b3)


def _work_list(counts, n_tiles, n_blocks_max):
    tb = EXPERT_ROWS
    nb = (counts + tb - 1) // tb
    blk_start = jnp.cumsum(nb) - nb
    items_end = jnp.cumsum(nb * n_tiles)
    total = items_end[-1]
    n_work = n_blocks_max * n_tiles
    w = jnp.arange(n_work, dtype=jnp.int32)
    valid = w < total
    wc = jnp.minimum(w, jnp.maximum(total - 1, 0))
    e = jnp.minimum(jnp.sum((items_end[None, :] <= wc[:, None]).astype(jnp.int32), axis=1), N_EXPERTS - 1)
    off = wc - (items_end[e] - nb[e] * n_tiles)
    nbe = jnp.maximum(nb[e], 1)
    f = off // nbe
    j = off % nbe
    rb = blk_start[e] + j
    first = valid & (j == 0)
    spare = w - total
    fo = jnp.where(valid, f, spare % n_tiles)
    rbo = jnp.where(valid, rb, jnp.sum(nb) + spare // n_tiles)
    i32 = lambda a: a.astype(jnp.int32)
    return (i32(e), i32(f), i32(rb), i32(fo), i32(rbo), i32(valid), i32(first)), blk_start


def _combine_kernel(slot_ref, ys_ref, hx_ref, wgt_ref, gt_ref, o_ref, buf, sem):
    tm = hx_ref.shape[0]
    i = pl.program_id(0)
    n_i = pl.num_programs(0)

    def issue(tile, b):
        base = tile * (tm * TOP_K)

        def body(r, carry):
            for k in range(TOP_K):
                s = slot_ref[base + r * TOP_K + k]
                pltpu.make_async_copy(ys_ref.at[pl.ds(s, 1)], buf.at[b, k, pl.ds(r, 1)], sem.at[b]).start()
            return carry

        lax.fori_loop(0, tm, body, 0)

    @pl.when(i == 0)
    def _():
        issue(0, 0)

    @pl.when(i + 1 < n_i)
    def _():
        issue(i + 1, (i + 1) % 2)

    cur = i % 2

    def drain(r, carry):
        for k in range(TOP_K):
            pltpu.make_async_copy(ys_ref.at[pl.ds(0, 1)], buf.at[cur, k, pl.ds(0, 1)], sem.at[cur]).wait()
        return carry

    lax.fori_loop(0, tm, drain, 0)

    wgt = wgt_ref[...]
    acc = wgt[:, 0:1] * buf[cur, 0]
    for k in range(1, TOP_K):
        acc = acc + wgt[:, k:k + 1] * buf[cur, k]
    o_ref[...] = hx_ref[...] + gt_ref[0] * acc


def _combine(slots_flat, ys, hx, wgt, gt2, rows_per_mod, tm=128):
    n, d = hx.shape
    tiles_per_mod = rows_per_mod // tm
    return pl.pallas_call(
        _combine_kernel,
        grid_spec=pltpu.PrefetchScalarGridSpec(
            num_scalar_prefetch=1,
            grid=(n // tm,),
            in_specs=[pl.BlockSpec(memory_space=pl.ANY),
                      pl.BlockSpec((tm, d), lambda i, s: (i, 0)),
                      pl.BlockSpec((tm, LANES), lambda i, s: (i, 0)),
                      pl.BlockSpec((1, 1, d), lambda i, s: (i // tiles_per_mod, 0, 0))],
            out_specs=pl.BlockSpec((tm, d), lambda i, s: (i, 0)),
            scratch_shapes=[pltpu.VMEM((2, TOP_K, tm, d), jnp.float32),
                            pltpu.SemaphoreType.DMA((2,))]),
        out_shape=jax.ShapeDtypeStruct((n, d), jnp.float32),
        compiler_params=_cparams(("arbitrary",)),
        name="moe_combine",
    )(slots_flat, ys, hx, wgt, gt2)


def _rope_tables(n_t):
    rows = n_t // GRID_W
    row_ids = jnp.repeat(jnp.arange(rows), GRID_W).astype(jnp.float32)
    col_ids = jnp.tile(jnp.arange(GRID_W), rows).astype(jnp.float32)
    freqs = jnp.exp(-math.log(ROPE_THETA) * jnp.arange(ROPE_PAIRS_AXIS, dtype=jnp.float32) / ROPE_PAIRS_AXIS)
    ang = jnp.concatenate([row_ids[:, None] * freqs, col_ids[:, None] * freqs], axis=-1)
    cos_e = jnp.repeat(jnp.cos(ang), 2, axis=-1)
    sin = jnp.sin(ang)
    sin_s = jnp.stack([-sin, sin], axis=-1).reshape(n_t, HEAD_DIM)
    return cos_e, sin_s


def _gate_layouts(og, n_b, n_t):
    g = og[:, :4 * M_HEADS].reshape(n_b, n_t, 4, M_HEADS)
    return g.transpose(0, 3, 2, 1), g.transpose(0, 3, 1, 2)


def _layer(x, c, ctx, c_ctx, w_mod, b_mod, g_norm1, g_norm2, w_in, b_in, g_q, g_k, g_mh,
           w_br_m, w_br_a, w_out, w_router, b_router, w_gu, b_gu, w_dn, b_dn):
    n_b, n_t, d = x.shape
    n_c = ctx.shape[1]
    bf = jnp.bfloat16

    w_lat = jnp.concatenate([w_in[:, :_O_GATES], w_in[:, _O_AQ:]], axis=1).astype(bf)
    b_lat = jnp.concatenate([b_in[:_O_GATES], b_in[_O_AQ:]]).reshape(1, _P_COLS)
    w_ctx = jnp.concatenate([w_in[:, _O_MK:_O_MO], w_in[:, _O_AK:_O_MG]], axis=1).astype(bf)
    b_ctx = jnp.concatenate([b_in[_O_MK:_O_MO], b_in[_O_AK:_O_MG]]).reshape(1, _C_COLS)
    w_gate = jnp.pad(w_in[:, _O_GATES:_O_AQ], ((0, 0), (0, LANES - 4 * M_HEADS)))
    b_gate = jnp.pad(b_in[_O_GATES:_O_AQ], (0, LANES - 4 * M_HEADS)).reshape(1, LANES)
    wr = jnp.pad(w_router, ((0, 0), (0, LANES - N_EXPERTS)))
    wr_hi = wr.astype(bf)
    wr_lo = (wr - wr_hi.astype(jnp.float32)).astype(bf)
    br = jnp.pad(b_router, (0, LANES - N_EXPERTS)).reshape(1, LANES)

    c8 = jnp.zeros((8, d), jnp.float32).at[:n_b].set(c).at[n_b].set(c_ctx)
    mod = _modulation(c8, w_mod, b_mod)
    mod6 = mod.reshape(8, 6, d)
    sh1x, sc1x, gt1x, sh2x, sc2x, gt2x = [mod6[:n_b, i].reshape(n_b, 1, d) for i in range(6)]
    sh1c, sc1c = [mod6[n_b:n_b + 1, i].reshape(1, 1, d) for i in range(2)]

    g1 = g_norm1.reshape(1, d)
    x2 = x.reshape(n_b * n_t, d)
    px2, ogx = _in_projection(x2, g1, sc1x, sh1x, w_lat, b_lat, w_gate, b_gate, rows_per_mod=n_t)
    pc2, ogc = _in_projection(ctx.reshape(n_b * n_c, d), g1, sc1c, sh1c, w_ctx, b_ctx, w_gate, b_gate,
                              rows_per_mod=n_b * n_c, tm=512)
    px = px2.reshape(n_b, n_t, _P_COLS)
    pc = pc2.reshape(n_b, n_c, _C_COLS)

    cos_e, sin_s = _rope_tables(n_t)
    a_out = _attention(px, pc, cos_e, sin_s, g_q.reshape(1, HEAD_DIM), g_k.reshape(1, HEAD_DIM))

    g_rows, g_cols = _gate_layouts(ogx, n_b, n_t)
    g_rows_c, g_cols_c = _gate_layouts(ogc, n_b, n_c)
    h_fwd = _mlstm(px, pc, g_rows, g_cols, g_rows_c, g_cols_c, backward=False)
    m_out = _mlstm(px, pc, g_rows, g_cols, g_rows_c, g_cols_c, backward=True,
                   h_fwd=h_fwd, g_mh=g_mh.reshape(1, M_HEADS * MV_DIM))

    n = n_b * n_t
    z = _merge(m_out.reshape(n, d), a_out.reshape(n, d), px2, w_br_m.astype(bf), w_br_a.astype(bf))
    hx, u2, top_idx, top_w, rank, cnt = _outproj_router(
        z, x2, w_out.astype(bf), gt1x, g_norm2.reshape(1, d), sc2x, sh2x, wr_hi, wr_lo, br, rows_per_mod=n_t)

    counts = cnt[0, :N_EXPERTS].astype(jnp.int32)
    n_blocks_max = (n * TOP_K) // EXPERT_ROWS + N_EXPERTS
    n_slots = n_blocks_max * EXPERT_ROWS
    up_tiles = D_FF // 512
    dn_tiles = d // 512
    work_up, blk_start = _work_list(counts, up_tiles, n_blocks_max)
    work_dn, _ = _work_list(counts, dn_tiles, n_blocks_max)
    idx4 = top_idx[:, :TOP_K]
    slots = (blk_start[idx4] * EXPERT_ROWS + rank[:, :TOP_K]).astype(jnp.int32).reshape(n * TOP_K)

    xs = _dispatch(slots, u2, n_slots)
    hid = _expert_up(work_up, xs, w_gu, b_gu)
    ys = _expert_down(work_dn, hid, w_dn, b_dn)
    out = _combine(slots, ys, hx, top_w, gt2x, rows_per_mod=n_t)
    return out.reshape(n_b, n_t, d)


def kernel(x, c, ctx, c_ctx, w_mod, b_mod, g_norm1, g_norm2, w_in, b_in, g_q, g_k, g_mh, w_br_m, w_br_a, w_out,
           w_router, b_router, w_gu, b_gu, w_dn, b_dn):
    assert w_mod.shape[0] == 1, "single layer: the context stream has no consumer after it"
    return _layer(x, c, ctx, c_ctx, w_mod[0], b_mod[0], g_norm1[0], g_norm2[0], w_in[0], b_in[0], g_q[0], g_k[0],
                  g_mh[0], w_br_m[0], w_br_a[0], w_out[0], w_router[0], b_router[0], w_gu[0], b_gu[0],
                  w_dn[0], b_dn[0])
```

```python
import functools
import math

import jax
import jax.numpy as jnp
from jax import lax
from jax.experimental import pallas as pl
from jax.experimental.pallas import tpu as pltpu

D_MODEL = 2048
GRID_W = 64
HEAD_DIM = 128
A_HEADS = 16
KV_HEADS = 4
Q_GROUP = A_HEADS // KV_HEADS
ROPE_THETA = 10000.0
ROPE_PAIRS_AXIS = HEAD_DIM // 4
M_HEADS = 4
MV_DIM = D_MODEL // M_HEADS
MQK_DIM = MV_DIM // 2
N_EXPERTS = 32
TOP_K = 4
D_FF = D_MODEL
SWIGLU_LIMIT = 7.0
SWIGLU_ALPHA = 1.702
EPS = 1e-6

_O_MQ, _O_MK, _O_MV, _O_MO = 0, 1024, 2048, 4096
_O_GATES = 6144
_O_AQ, _O_AK, _O_AV, _O_MG = 6160, 8208, 8720, 9232
_F_IN = 13328
_P_MQ, _P_MK, _P_MV, _P_MO, _P_AQ, _P_AK, _P_AV, _P_GM, _P_GA = (
    0, 1024, 2048, 4096, 6144, 8192, 8704, 9216, 11264)
_P_COLS = 13312
_C_MK, _C_MV, _C_AK, _C_AV = 0, 1024, 3072, 3584
_C_COLS = 4096

LANES = 128
M_CHUNK = 256
EXPERT_ROWS = 256
ROW_ALIGN = 16
NEG_BIG = -1e30
VMEM_LIMIT = 56 * 1024 * 1024

_HI = lax.Precision.HIGHEST


def _cparams(sem, vmem=VMEM_LIMIT):
    return pltpu.CompilerParams(dimension_semantics=sem, vmem_limit_bytes=vmem)


def _mod_kernel(c_ref, w_ref, b_ref, o_ref):
    c = c_ref[...]
    a = c * jax.nn.sigmoid(c)
    o_ref[...] = lax.dot_general(a, w_ref[...], (((1,), (0,)), ((), ())), precision=_HI,
                                 preferred_element_type=jnp.float32) + b_ref[...]


def _modulation(c8, w_mod, b_mod):
    d, n = w_mod.shape
    tn = 1024
    return pl.pallas_call(
        _mod_kernel,
        grid=(n // tn,),
        in_specs=[pl.BlockSpec((8, d), lambda j: (0, 0)),
                  pl.BlockSpec((d, tn), lambda j: (0, j)),
                  pl.BlockSpec((1, tn), lambda j: (0, j))],
        out_specs=pl.BlockSpec((8, tn), lambda j: (0, j)),
        out_shape=jax.ShapeDtypeStruct((8, n), jnp.float32),
        compiler_params=_cparams(("arbitrary",)),
        name="modulation",
    )(c8, w_mod, b_mod.reshape(1, n))


def _proj_kernel(x_ref, g_ref, sc_ref, sh_ref, w_ref, b_ref, wg_ref, bg_ref, o_ref, og_ref, u_ref):
    @pl.when(pl.program_id(1) == 0)
    def _():
        x = x_ref[...]
        y = x * lax.rsqrt(jnp.mean(x * x, axis=-1, keepdims=True) + EPS) * g_ref[...]
        u = y * (1.0 + sc_ref[0]) + sh_ref[0]
        u_ref[...] = u.astype(jnp.bfloat16)
        og_ref[...] = lax.dot_general(u, wg_ref[...], (((1,), (0,)), ((), ())), precision=_HI,
                                      preferred_element_type=jnp.float32) + bg_ref[...]

    o_ref[...] = (jnp.dot(u_ref[...], w_ref[...], preferred_element_type=jnp.float32)
                  + b_ref[...]).astype(o_ref.dtype)


def _in_projection(x2, g, sc, sh, w, b, wg, bg, rows_per_mod, tm=512, tn=1024):
    n, d = x2.shape
    nc = w.shape[1]
    tiles_per_mod = rows_per_mod // tm
    return pl.pallas_call(
        _proj_kernel,
        grid=(n // tm, nc // tn),
        in_specs=[pl.BlockSpec((tm, d), lambda i, j: (i, 0)),
                  pl.BlockSpec((1, d), lambda i, j: (0, 0)),
                  pl.BlockSpec((1, 1, d), lambda i, j: (i // tiles_per_mod, 0, 0)),
                  pl.BlockSpec((1, 1, d), lambda i, j: (i // tiles_per_mod, 0, 0)),
                  pl.BlockSpec((d, tn), lambda i, j: (0, j)),
                  pl.BlockSpec((1, tn), lambda i, j: (0, j)),
                  pl.BlockSpec((d, LANES), lambda i, j: (0, 0)),
                  pl.BlockSpec((1, LANES), lambda i, j: (0, 0))],
        out_specs=[pl.BlockSpec((tm, tn), lambda i, j: (i, j)),
                   pl.BlockSpec((tm, LANES), lambda i, j: (i, 0))],
        out_shape=[jax.ShapeDtypeStruct((n, nc), jnp.bfloat16),
                   jax.ShapeDtypeStruct((n, LANES), jnp.float32)],
        scratch_shapes=[pltpu.VMEM((tm, d), jnp.bfloat16)],
        compiler_params=_cparams(("arbitrary", "arbitrary")),
        name="in_projection",
    )(x2, g, sc, sh, w, b, wg, bg)


def _rms_head(x, g):
    return x * lax.rsqrt(jnp.mean(x * x, axis=-1, keepdims=True) + EPS) * g


def _rope(x, cos_e, sin_s):
    lane = lax.broadcasted_iota(jnp.int32, x.shape, 1)
    swapped = jnp.where(lane % 2 == 0, pltpu.roll(x, LANES - 1, 1), pltpu.roll(x, 1, 1))
    return x * cos_e + swapped * sin_s


def _attn_kernel(q_ref, kx_ref, vx_ref, kc_ref, vc_ref, cosq_ref, sinq_ref, cosk_ref, sink_ref,
                 gq_ref, gk_ref, o_ref, k_s, v_s):
    n_t = kx_ref.shape[1]

    @pl.when(pl.program_id(2) == 0)
    def _():
        kx = _rms_head(kx_ref[0].astype(jnp.float32), gk_ref[...])
        k_s[0:n_t, :] = _rope(kx, cosk_ref[...], sink_ref[...]).astype(jnp.bfloat16)
        k_s[n_t:, :] = _rms_head(kc_ref[0].astype(jnp.float32), gk_ref[...]).astype(jnp.bfloat16)
        v_s[0:n_t, :] = vx_ref[0]
        v_s[n_t:, :] = vc_ref[0]

    scale = HEAD_DIM ** -0.5
    kk = k_s[...]
    vv = v_s[...]
    for g in range(Q_GROUP):
        q = q_ref[0, :, g * HEAD_DIM:(g + 1) * HEAD_DIM].astype(jnp.float32)
        q = _rope(_rms_head(q, gq_ref[...]), cosq_ref[...], sinq_ref[...]) * scale
        s = lax.dot_general(q.astype(jnp.bfloat16), kk, (((1,), (1,)), ((), ())),
                            preferred_element_type=jnp.float32)
        p = jnp.exp(s - jnp.max(s, axis=-1, keepdims=True))
        l = jnp.sum(p, axis=-1, keepdims=True)
        o = jnp.dot(p.astype(jnp.bfloat16), vv, preferred_element_type=jnp.float32) / l
        o_ref[0, :, g * HEAD_DIM:(g + 1) * HEAD_DIM] = o.astype(o_ref.dtype)


def _attention(px, pc, cos_e, sin_s, g_q, g_k, tq=256):
    n_b, n_t, _ = px.shape
    n_c = pc.shape[1]
    gw = Q_GROUP * HEAD_DIM
    return pl.pallas_call(
        _attn_kernel,
        grid=(n_b, KV_HEADS, n_t // tq),
        in_specs=[pl.BlockSpec((1, tq, gw), lambda b, h, i: (b, i, _P_AQ // gw + h)),
                  pl.BlockSpec((1, n_t, HEAD_DIM), lambda b, h, i: (b, 0, _P_AK // HEAD_DIM + h)),
                  pl.BlockSpec((1, n_t, HEAD_DIM), lambda b, h, i: (b, 0, _P_AV // HEAD_DIM + h)),
                  pl.BlockSpec((1, n_c, HEAD_DIM), lambda b, h, i: (b, 0, _C_AK // HEAD_DIM + h)),
                  pl.BlockSpec((1, n_c, HEAD_DIM), lambda b, h, i: (b, 0, _C_AV // HEAD_DIM + h)),
                  pl.BlockSpec((tq, HEAD_DIM), lambda b, h, i: (i, 0)),
                  pl.BlockSpec((tq, HEAD_DIM), lambda b, h, i: (i, 0)),
                  pl.BlockSpec((n_t, HEAD_DIM), lambda b, h, i: (0, 0)),
                  pl.BlockSpec((n_t, HEAD_DIM), lambda b, h, i: (0, 0)),
                  pl.BlockSpec((1, HEAD_DIM), lambda b, h, i: (0, 0)),
                  pl.BlockSpec((1, HEAD_DIM), lambda b, h, i: (0, 0))],
        out_specs=pl.BlockSpec((1, tq, gw), lambda b, h, i: (b, i, h)),
        out_shape=jax.ShapeDtypeStruct((n_b, n_t, A_HEADS * HEAD_DIM), jnp.bfloat16),
        scratch_shapes=[pltpu.VMEM((n_t + n_c, HEAD_DIM), jnp.bfloat16),
                        pltpu.VMEM((n_t + n_c, HEAD_DIM), jnp.bfloat16)],
        compiler_params=_cparams(("arbitrary", "arbitrary", "arbitrary")),
        name="attention",
    )(px, px, px, pc, pc, cos_e, sin_s, cos_e, sin_s, g_q, g_k)


def _log_sigmoid(x):
    return jnp.minimum(x, 0.0) - jnp.log(1.0 + jnp.exp(-jnp.abs(x)))


def _mlstm_chunk(q, k, v, i_row, f_row, i_col, f_col, ct_ref, n_ref, m_ref, backward, want_h):
    n_l = k.shape[0]
    lf_row = _log_sigmoid(f_row)
    lf_col = _log_sigmoid(f_col)
    t_idx = lax.broadcasted_iota(jnp.int32, (n_l, n_l), 0)
    s_idx = lax.broadcasted_iota(jnp.int32, (n_l, n_l), 1)
    seen = (s_idx >= t_idx) if backward else (s_idx <= t_idx)
    b_col = jnp.sum(jnp.where(seen, lf_row, 0.0), axis=1, keepdims=True)
    seen_t = (t_idx >= s_idx) if backward else (t_idx <= s_idx)
    b_row = jnp.sum(jnp.where(seen_t, lf_col, 0.0), axis=0, keepdims=True)
    b_end = jnp.sum(lf_row, axis=1, keepdims=True)
    m_prev = m_ref[...]
    ct = ct_ref[...]
    n_vec = n_ref[...]
    kf = k.astype(jnp.float32)

    h = None
    if want_h:
        d_intra = jnp.where(seen, b_col - b_row + i_row, NEG_BIG)
        d_inter = b_col + m_prev
        m_t = jnp.maximum(d_inter, jnp.max(d_intra, axis=1, keepdims=True))
        qk = lax.dot_general(q, k, (((1,), (1,)), ((), ())), preferred_element_type=jnp.float32)
        s = qk * jnp.exp(d_intra - m_t)
        w_inter = jnp.exp(d_inter - m_t)
        num = (jnp.dot(s.astype(jnp.bfloat16), v, preferred_element_type=jnp.float32)
               + w_inter * jnp.dot(q, ct.astype(jnp.bfloat16), preferred_element_type=jnp.float32))
        den = (jnp.sum(s, axis=1, keepdims=True)
               + w_inter * jnp.sum(q.astype(jnp.float32) * n_vec, axis=1, keepdims=True))
        h = num / jnp.maximum(jnp.abs(den), jnp.exp(-m_t))

    g_col = b_end - b_col + i_col
    m_new = jnp.maximum(b_end + m_prev, jnp.max(g_col, axis=0, keepdims=True))
    w_s = jnp.exp(g_col - m_new)
    w_c = jnp.exp(b_end + m_prev - m_new)
    wv = (w_s * v.astype(jnp.float32)).astype(jnp.bfloat16)
    ct_ref[...] = w_c * ct + lax.dot_general(k, wv, (((0,), (0,)), ((), ())),
                                             preferred_element_type=jnp.float32)
    n_ref[...] = w_c * n_vec + jnp.sum(w_s * kf, axis=0, keepdims=True)
    m_ref[...] = m_new
    return h


def _mlstm_kernel(backward, *refs):
    if backward:
        (q_ref, k_ref, v_ref, kc_ref, vc_ref, gr_ref, gc_ref, grc_ref, gcc_ref,
         hf_ref, op_ref, gmh_ref, o_ref, ct_ref, n_ref, m_ref) = refs
    else:
        (q_ref, k_ref, v_ref, kc_ref, vc_ref, gr_ref, gc_ref, grc_ref, gcc_ref,
         o_ref, ct_ref, n_ref, m_ref) = refs
    gi, gf = (2, 3) if backward else (0, 1)
    step = pl.program_id(2)

    @pl.when(step == 0)
    def _():
        ct_ref[...] = jnp.zeros_like(ct_ref)
        n_ref[...] = jnp.zeros_like(n_ref)
        m_ref[...] = jnp.zeros_like(m_ref)
        _mlstm_chunk(None, kc_ref[0], vc_ref[0],
                     grc_ref[0, 0, gi:gi + 1, :], grc_ref[0, 0, gf:gf + 1, :],
                     gcc_ref[0, 0, :, gi:gi + 1], gcc_ref[0, 0, :, gf:gf + 1],
                     ct_ref, n_ref, m_ref, backward, False)

    @pl.when(step > 0)
    def _():
        q = (q_ref[0].astype(jnp.float32) * (MQK_DIM ** -0.5)).astype(jnp.bfloat16)
        h = _mlstm_chunk(q, k_ref[0], v_ref[0],
                         gr_ref[0, 0, gi:gi + 1, :], gr_ref[0, 0, gf:gf + 1, :],
                         gc_ref[0, 0, :, gi:gi + 1], gc_ref[0, 0, :, gf:gf + 1],
                         ct_ref, n_ref, m_ref, backward, True)
        if backward:
            ht = h + hf_ref[0]
            y = ht * lax.rsqrt(jnp.mean(ht * ht, axis=-1, keepdims=True) + EPS) * gmh_ref[...]
            o_ref[0] = (y * jax.nn.sigmoid(op_ref[0].astype(jnp.float32))).astype(o_ref.dtype)
        else:
            o_ref[0] = h


def _mlstm(px, pc, g_rows, g_cols, g_rows_c, g_cols_c, backward, h_fwd=None, g_mh=None):
    n_b, n_t, _ = px.shape
    n_c = pc.shape[1]
    n_l = M_CHUNK
    assert n_c == n_l and n_t % n_l == 0
    n_chunk = n_t // n_l

    if backward:
        def cidx(s):
            return jnp.minimum(n_chunk - s, n_chunk - 1)
    else:
        def cidx(s):
            return jnp.maximum(s - 1, 0)

    in_specs = [
        pl.BlockSpec((1, n_l, MQK_DIM), lambda b, h, s: (b, cidx(s), _P_MQ // MQK_DIM + h)),
        pl.BlockSpec((1, n_l, MQK_DIM), lambda b, h, s: (b, cidx(s), _P_MK // MQK_DIM + h)),
        pl.BlockSpec((1, n_l, MV_DIM), lambda b, h, s: (b, cidx(s), _P_MV // MV_DIM + h)),
        pl.BlockSpec((1, n_c, MQK_DIM), lambda b, h, s: (b, 0, _C_MK // MQK_DIM + h)),
        pl.BlockSpec((1, n_c, MV_DIM), lambda b, h, s: (b, 0, _C_MV // MV_DIM + h)),
        pl.BlockSpec((1, 1, 4, n_l), lambda b, h, s: (b, h, 0, cidx(s))),
        pl.BlockSpec((1, 1, n_l, 4), lambda b, h, s: (b, h, cidx(s), 0)),
        pl.BlockSpec((1, 1, 4, n_c), lambda b, h, s: (b, h, 0, 0)),
        pl.BlockSpec((1, 1, n_c, 4), lambda b, h, s: (b, h, 0, 0)),
    ]
    args = [px, px, px, pc, pc, g_rows, g_cols, g_rows_c, g_cols_c]
    if backward:
        in_specs += [
            pl.BlockSpec((1, n_l, MV_DIM), lambda b, h, s: (b, cidx(s), h)),
            pl.BlockSpec((1, n_l, MV_DIM), lambda b, h, s: (b, cidx(s), _P_MO // MV_DIM + h)),
            pl.BlockSpec((1, MV_DIM), lambda b, h, s: (0, h)),
        ]
        args += [h_fwd, px, g_mh]
        out_dtype = jnp.bfloat16
    else:
        out_dtype = jnp.float32
    return pl.pallas_call(
        functools.partial(_mlstm_kernel, backward),
        grid=(n_b, M_HEADS, n_chunk + 1),
        in_specs=in_specs,
        out_specs=pl.BlockSpec((1, n_l, MV_DIM), lambda b, h, s: (b, cidx(s), h)),
        out_shape=jax.ShapeDtypeStruct((n_b, n_t, M_HEADS * MV_DIM), out_dtype),
        scratch_shapes=[pltpu.VMEM((MQK_DIM, MV_DIM), jnp.float32),
                        pltpu.VMEM((1, MQK_DIM), jnp.float32),
                        pltpu.VMEM((1, 1), jnp.float32)],
        compiler_params=_cparams(("arbitrary", "arbitrary", "arbitrary")),
        name="mlstm_bwd" if backward else "mlstm_fwd",
    )(*args)


def _merge_kernel(m_ref, a_ref, gm_ref, ga_ref, wm_ref, wa_ref, o_ref):
    zm = jnp.dot(m_ref[...], wm_ref[...], preferred_element_type=jnp.float32)
    za = jnp.dot(a_ref[...], wa_ref[...], preferred_element_type=jnp.float32)
    z = (jax.nn.sigmoid(gm_ref[...].astype(jnp.float32)) * zm
         + jax.nn.sigmoid(ga_ref[...].astype(jnp.float32)) * za)
    o_ref[...] = z.astype(o_ref.dtype)


def _merge(m_out, a_out, px2, w_br_m, w_br_a, tm=512, tn=512):
    n, d = m_out.shape
    return pl.pallas_call(
        _merge_kernel,
        grid=(n // tm, d // tn),
        in_specs=[pl.BlockSpec((tm, d), lambda i, j: (i, 0)),
                  pl.BlockSpec((tm, d), lambda i, j: (i, 0)),
                  pl.BlockSpec((tm, tn), lambda i, j: (i, _P_GM // tn + j)),
                  pl.BlockSpec((tm, tn), lambda i, j: (i, _P_GA // tn + j)),
                  pl.BlockSpec((d, tn), lambda i, j: (0, j)),
                  pl.BlockSpec((d, tn), lambda i, j: (0, j))],
        out_specs=pl.BlockSpec((tm, tn), lambda i, j: (i, j)),
        out_shape=jax.ShapeDtypeStruct((n, d), jnp.bfloat16),
        compiler_params=_cparams(("arbitrary", "arbitrary")),
        name="merge",
    )(m_out, a_out, px2, px2, w_br_m, w_br_a)


def _split_bf16(x):
    hi = x.astype(jnp.bfloat16)
    lo = (x - hi.astype(jnp.float32)).astype(jnp.bfloat16)
    return hi, lo


def _outproj_router_kernel(z_ref, x_ref, wo_ref, gt_ref, g2_ref, sc_ref, sh_ref, wrh_ref, wrl_ref, br_ref,
                           hx_ref, u2_ref, idx_ref, wgt_ref, rank_ref, cnt_ref, carry_ref):
    tm = z_ref.shape[0]

    @pl.when(pl.program_id(0) == 0)
    def _():
        carry_ref[...] = jnp.zeros_like(carry_ref)

    y = jnp.dot(z_ref[...], wo_ref[...], preferred_element_type=jnp.float32)
    hx = x_ref[...] + gt_ref[0] * y
    hx_ref[...] = hx
    u2 = (hx * lax.rsqrt(jnp.mean(hx * hx, axis=-1, keepdims=True) + EPS) * g2_ref[...]
          * (1.0 + sc_ref[0]) + sh_ref[0])
    u2_ref[...] = u2

    u_hi, u_lo = _split_bf16(u2)
    logits = (jnp.dot(u_hi, wrh_ref[...], preferred_element_type=jnp.float32)
              + jnp.dot(u_lo, wrh_ref[...], preferred_element_type=jnp.float32)
              + jnp.dot(u_hi, wrl_ref[...], preferred_element_type=jnp.float32)) + br_ref[...]
    lane = lax.broadcasted_iota(jnp.int32, (tm, LANES), 1)
    logits = jnp.where(lane < N_EXPERTS, logits, NEG_BIG)

    idx_out = jnp.zeros((tm, LANES), jnp.int32)
    val_out = jnp.zeros((tm, LANES), jnp.float32)
    chosen = jnp.zeros((tm, LANES), jnp.float32)
    sel = []
    top0 = None
    for k in range(TOP_K):
        mx = jnp.max(logits, axis=-1, keepdims=True)
        ix = jnp.min(jnp.where(logits == mx, lane, LANES), axis=-1, keepdims=True)
        hit = lane == ix
        if k == 0:
            top0 = mx
        idx_out = jnp.where(lane == k, ix, idx_out)
        val_out = jnp.where(lane == k, jnp.exp(mx - top0), val_out)
        chosen = jnp.where(hit, 1.0, chosen)
        sel.append(hit)
        logits = jnp.where(hit, NEG_BIG, logits)
    idx_ref[...] = idx_out
    wgt_ref[...] = val_out / jnp.sum(val_out, axis=-1, keepdims=True)

    r_idx = lax.broadcasted_iota(jnp.int32, (tm, tm), 0)
    c_idx = lax.broadcasted_iota(jnp.int32, (tm, tm), 1)
    before = jnp.where(c_idx < r_idx, 1.0, 0.0).astype(jnp.bfloat16)
    prior = jnp.dot(before, chosen.astype(jnp.bfloat16), preferred_element_type=jnp.float32) + carry_ref[...]
    rank_out = jnp.zeros((tm, LANES), jnp.int32)
    for k in range(TOP_K):
        rk = jnp.sum(jnp.where(sel[k], prior, 0.0), axis=-1, keepdims=True)
        rank_out = jnp.where(lane == k, rk.astype(jnp.int32), rank_out)
    rank_ref[...] = rank_out
    carry_ref[...] = carry_ref[...] + jnp.sum(chosen, axis=0, keepdims=True)
    cnt_ref[...] = jnp.broadcast_to(carry_ref[...], cnt_ref.shape)


def _outproj_router(z, x2, w_out, gt1, g2, sc2, sh2, wr_hi, wr_lo, br, rows_per_mod, tm=256):
    n, d = z.shape
    tiles_per_mod = rows_per_mod // tm
    row = lambda i: (i, 0)
    fixed = lambda i: (0, 0)
    modi = lambda i: (i // tiles_per_mod, 0, 0)
    return pl.pallas_call(
        _outproj_router_kernel,
        grid=(n // tm,),
        in_specs=[pl.BlockSpec((tm, d), row),
                  pl.BlockSpec((tm, d), row),
                  pl.BlockSpec((d, d), fixed),
                  pl.BlockSpec((1, 1, d), modi),
                  pl.BlockSpec((1, d), fixed),
                  pl.BlockSpec((1, 1, d), modi),
                  pl.BlockSpec((1, 1, d), modi),
                  pl.BlockSpec((d, LANES), fixed),
                  pl.BlockSpec((d, LANES), fixed),
                  pl.BlockSpec((1, LANES), fixed)],
        out_specs=[pl.BlockSpec((tm, d), row),
                   pl.BlockSpec((tm, d), row),
                   pl.BlockSpec((tm, LANES), row),
                   pl.BlockSpec((tm, LANES), row),
                   pl.BlockSpec((tm, LANES), row),
                   pl.BlockSpec((8, LANES), fixed)],
        out_shape=[jax.ShapeDtypeStruct((n, d), jnp.float32),
                   jax.ShapeDtypeStruct((n, d), jnp.float32),
                   jax.ShapeDtypeStruct((n, LANES), jnp.int32),
                   jax.ShapeDtypeStruct((n, LANES), jnp.float32),
                   jax.ShapeDtypeStruct((n, LANES), jnp.int32),
                   jax.ShapeDtypeStruct((8, LANES), jnp.float32)],
        scratch_shapes=[pltpu.VMEM((1, LANES), jnp.float32)],
        compiler_params=_cparams(("arbitrary",)),
        name="outproj_router",
    )(z, x2, w_out, gt1, g2, sc2, sh2, wr_hi, wr_lo, br)


def _start_zero_rows(zbuf, dst_rows, start, count, sem, wait):
    n_full = count // EXPERT_ROWS
    rem = count % EXPERT_ROWS
    pieces = [(i < n_full, start + i * EXPERT_ROWS, EXPERT_ROWS) for i in range(3)]
    off = start + n_full * EXPERT_ROWS
    p = EXPERT_ROWS // 2
    while p >= ROW_ALIGN:
        pieces.append(((rem & p) != 0, off + (rem // (2 * p)) * (2 * p), p))
        p //= 2
    for cond, row, size in pieces:
        @pl.when(cond)
        def _(row=row, size=size):
            cp = pltpu.make_async_copy(zbuf.at[pl.ds(0, size)], dst_rows(pl.multiple_of(row, ROW_ALIGN), size), sem)
            cp.wait() if wait else cp.start()


def _zero_rows(zbuf, dst_rows, start, count, sem):
    _start_zero_rows(zbuf, dst_rows, start, count, sem, wait=False)
    _start_zero_rows(zbuf, dst_rows, start, count, sem, wait=True)


def _dispatch_kernel(slot_ref, gap_ref, tail_ref, u_ref, xs_ref, zbuf, sem, zsem):
    tm = u_ref.shape[0]
    n_rows = xs_ref.shape[0]
    base = pl.program_id(0) * (tm * TOP_K)

    @pl.when(pl.program_id(0) == 0)
    def _():
        zbuf[...] = jnp.zeros_like(zbuf)
        for wait in (False, True):
            for e in range(N_EXPERTS):
                @pl.when(gap_ref[e] >= 0)
                def _(e=e, wait=wait):
                    cp = pltpu.make_async_copy(
                        zbuf.at[pl.ds(0, ROW_ALIGN)],
                        xs_ref.at[pl.ds(pl.multiple_of(gap_ref[e], ROW_ALIGN), ROW_ALIGN)], zsem)
                    cp.wait() if wait else cp.start()
        _zero_rows(zbuf, lambda r, s: xs_ref.at[pl.ds(r, s)], tail_ref[0], n_rows - tail_ref[0], zsem)

    def issue(r, carry):
        for k in range(TOP_K):
            s = slot_ref[base + r * TOP_K + k]
            pltpu.make_async_copy(u_ref.at[pl.ds(r, 1)], xs_ref.at[pl.ds(s, 1)], sem).start()
        return carry

    lax.fori_loop(0, tm, issue, 0)

    def drain(r, carry):
        for k in range(TOP_K):
            pltpu.make_async_copy(u_ref.at[pl.ds(0, 1)], xs_ref.at[pl.ds(0, 1)], sem).wait()
        return carry

    lax.fori_loop(0, tm, drain, 0)


def _dispatch(slots_flat, gap_rows, tail, u2, n_rows, tm=256):
    n, d = u2.shape
    return pl.pallas_call(
        _dispatch_kernel,
        grid_spec=pltpu.PrefetchScalarGridSpec(
            num_scalar_prefetch=3,
            grid=(n // tm,),
            in_specs=[pl.BlockSpec((tm, d), lambda i, *_: (i, 0))],
            out_specs=pl.BlockSpec(memory_space=pl.ANY),
            scratch_shapes=[pltpu.VMEM((EXPERT_ROWS, d), jnp.float32),
                            pltpu.SemaphoreType.DMA(()), pltpu.SemaphoreType.DMA(())]),
        out_shape=jax.ShapeDtypeStruct((n_rows, d), jnp.float32),
        compiler_params=_cparams(("arbitrary",)),
        name="moe_dispatch",
    )(slots_flat, gap_rows, tail, u2)


def _expert_rows_loop(start_ref, nblk_ref, tail_ref, src_ref, dst_ref, ibuf, obuf, zbuf, sem_in, sem_out, sem_z,
                      compute):
    e = pl.program_id(0)
    tile = obuf.shape[2]
    col = pl.multiple_of(pl.program_id(1) * tile, tile)
    nb = nblk_ref[e]
    base = start_ref[e]

    def rows(k):
        return pl.ds(pl.multiple_of(base + k * EXPERT_ROWS, ROW_ALIGN), EXPERT_ROWS)

    def in_copy(k, slot):
        return pltpu.make_async_copy(src_ref.at[rows(k)], ibuf.at[slot], sem_in.at[slot])

    def out_copy(k, slot):
        return pltpu.make_async_copy(obuf.at[slot], dst_ref.at[rows(k), pl.ds(col, tile)], sem_out.at[slot])

    @pl.when(nb > 0)
    def _():
        in_copy(0, 0).start()

    def body(k, carry):
        slot = lax.rem(k, 2)

        @pl.when(k + 1 < nb)
        def _():
            in_copy(k + 1, 1 - slot).start()

        in_copy(k, slot).wait()

        @pl.when(k >= 2)
        def _():
            out_copy(k - 2, slot).wait()

        obuf[slot] = compute(ibuf[slot]).astype(obuf.dtype)
        out_copy(k, slot).start()
        return carry

    lax.fori_loop(0, nb, body, 0)

    @pl.when(nb >= 2)
    def _():
        out_copy(nb - 2, lax.rem(nb, 2)).wait()

    @pl.when(nb >= 1)
    def _():
        out_copy(nb - 1, lax.rem(nb + 1, 2)).wait()

    @pl.when(e == pl.num_programs(0) - 1)
    def _():
        zbuf[...] = jnp.zeros_like(zbuf)
        _zero_rows(zbuf, lambda r, s: dst_ref.at[pl.ds(r, s), pl.ds(col, tile)],
                   tail_ref[1], dst_ref.shape[0] - tail_ref[1], sem_z)


def _expert_up_kernel(start_ref, nblk_ref, tail_ref, xs_ref, wg_ref, wu_ref, bg_ref, bu_ref, hid_ref,
                      wg_s, wu_s, ibuf, obuf, zbuf, sem_in, sem_out, sem_z):
    wg_s[...] = wg_ref[...].astype(jnp.bfloat16)
    wu_s[...] = wu_ref[...].astype(jnp.bfloat16)

    def compute(xb):
        x = xb.astype(jnp.bfloat16)
        gate = jnp.dot(x, wg_s[...], preferred_element_type=jnp.float32) + bg_ref[...]
        up = jnp.dot(x, wu_s[...], preferred_element_type=jnp.float32) + bu_ref[...]
        gate = jnp.minimum(gate, SWIGLU_LIMIT)
        up = jnp.clip(up, -SWIGLU_LIMIT, SWIGLU_LIMIT)
        return (up + 1.0) * gate * jax.nn.sigmoid(SWIGLU_ALPHA * gate)

    _expert_rows_loop(start_ref, nblk_ref, tail_ref, xs_ref, hid_ref, ibuf, obuf, zbuf, sem_in, sem_out, sem_z,
                      compute)


def _expert_up(meta, xs, w_gu, b_gu, tf=512):
    n_rows, d = xs.shape
    nfc = D_FF // tf
    b3 = b_gu.reshape(N_EXPERTS, 1, 2 * D_FF)
    return pl.pallas_call(
        _expert_up_kernel,
        grid_spec=pltpu.PrefetchScalarGridSpec(
            num_scalar_prefetch=3,
            grid=(N_EXPERTS, nfc),
            in_specs=[pl.BlockSpec(memory_space=pl.ANY),
                      pl.BlockSpec((None, d, tf), lambda e, f, *_: (e, 0, f)),
                      pl.BlockSpec((None, d, tf), lambda e, f, *_: (e, 0, nfc + f)),
                      pl.BlockSpec((None, 1, tf), lambda e, f, *_: (e, 0, f)),
                      pl.BlockSpec((None, 1, tf), lambda e, f, *_: (e, 0, nfc + f))],
            out_specs=pl.BlockSpec(memory_space=pl.ANY),
            scratch_shapes=[pltpu.VMEM((d, tf), jnp.bfloat16), pltpu.VMEM((d, tf), jnp.bfloat16),
                            pltpu.VMEM((2, EXPERT_ROWS, d), jnp.float32),
                            pltpu.VMEM((2, EXPERT_ROWS, tf), jnp.bfloat16),
                            pltpu.VMEM((EXPERT_ROWS, tf), jnp.bfloat16),
                            pltpu.SemaphoreType.DMA((2,)), pltpu.SemaphoreType.DMA((2,)),
                            pltpu.SemaphoreType.DMA(())]),
        out_shape=jax.ShapeDtypeStruct((n_rows, D_FF), jnp.bfloat16),
        compiler_params=_cparams(("arbitrary", "arbitrary")),
        name="expert_up",
    )(*meta, xs, w_gu, w_gu, b3, b3)


def _expert_down_kernel(start_ref, nblk_ref, tail_ref, hid_ref, wd_ref, bd_ref, ys_ref,
                        wd_s, ibuf, obuf, zbuf, sem_in, sem_out, sem_z):
    wd_s[...] = wd_ref[...].astype(jnp.bfloat16)

    def compute(hb):
        return jnp.dot(hb, wd_s[...], preferred_element_type=jnp.float32) + bd_ref[...]

    _expert_rows_loop(start_ref, nblk_ref, tail_ref, hid_ref, ys_ref, ibuf, obuf, zbuf, sem_in, sem_out, sem_z,
                      compute)


def _expert_down(meta, hid, w_dn, b_dn, tn=512):
    n_rows, dff = hid.shape
    d = w_dn.shape[2]
    b3 = b_dn.reshape(N_EXPERTS, 1, d)
    return pl.pallas_call(
        _expert_down_kernel,
        grid_spec=pltpu.PrefetchScalarGridSpec(
            num_scalar_prefetch=3,
            grid=(N_EXPERTS, d // tn),
            in_specs=[pl.BlockSpec(memory_space=pl.ANY),
                      pl.BlockSpec((None, dff, tn), lambda e, f, *_: (e, 0, f)),
                      pl.BlockSpec((None, 1, tn), lambda e, f, *_: (e, 0, f))],
            out_specs=pl.BlockSpec(memory_space=pl.ANY),
            scratch_shapes=[pltpu.VMEM((dff, tn), jnp.bfloat16),
                            pltpu.VMEM((2, EXPERT_ROWS, dff), jnp.bfloat16),
                            pltpu.VMEM((2, EXPERT_ROWS, tn), jnp.float32),
                            pltpu.VMEM((EXPERT_ROWS, tn), jnp.float32),
                            pltpu.SemaphoreType.DMA((2,)), pltpu.SemaphoreType.DMA((2,)),
                            pltpu.SemaphoreType.DMA(())]),
        out_shape=jax.ShapeDtypeStruct((n_rows, d), jnp.float32),
        compiler_params=_cparams(("arbitrary", "arbitrary")),
        name="expert_down",
    )(*meta, hid, w_dn, b3)


def _row_layout(counts, n_assign):
    c_al = (counts + ROW_ALIGN - 1) // ROW_ALIGN * ROW_ALIGN
    start = jnp.cumsum(c_al) - c_al
    total = jnp.sum(c_al)
    nblk = (counts + EXPERT_ROWS - 1) // EXPERT_ROWS
    covered = jnp.max(start + nblk * EXPERT_ROWS)
    gap_rows = jnp.where(counts > 0, start + c_al - ROW_ALIGN, -1)
    n_rows = n_assign + N_EXPERTS * ROW_ALIGN + EXPERT_ROWS
    i32 = lambda a: a.astype(jnp.int32)
    tail = i32(jnp.stack([total, covered]))
    return i32(start), i32(nblk), tail, i32(gap_rows), n_rows


def _combine_kernel(slot_ref, ys_ref, hx_ref, wgt_ref, gt_ref, o_ref, buf, sem):
    tm = hx_ref.shape[0]
    i = pl.program_id(0)
    n_i = pl.num_programs(0)

    def issue(tile, b):
        base = tile * (tm * TOP_K)

        def body(r, carry):
            for k in range(TOP_K):
                s = slot_ref[base + r * TOP_K + k]
                pltpu.make_async_copy(ys_ref.at[pl.ds(s, 1)], buf.at[b, k, pl.ds(r, 1)], sem.at[b]).start()
            return carry

        lax.fori_loop(0, tm, body, 0)

    @pl.when(i == 0)
    def _():
        issue(0, 0)

    @pl.when(i + 1 < n_i)
    def _():
        issue(i + 1, (i + 1) % 2)

    cur = i % 2

    def drain(r, carry):
        for k in range(TOP_K):
            pltpu.make_async_copy(ys_ref.at[pl.ds(0, 1)], buf.at[cur, k, pl.ds(0, 1)], sem.at[cur]).wait()
        return carry

    lax.fori_loop(0, tm, drain, 0)

    wgt = wgt_ref[...]
    acc = wgt[:, 0:1] * buf[cur, 0]
    for k in range(1, TOP_K):
        acc = acc + wgt[:, k:k + 1] * buf[cur, k]
    o_ref[...] = hx_ref[...] + gt_ref[0] * acc


def _combine(slots_flat, ys, hx, wgt, gt2, rows_per_mod, tm=128):
    n, d = hx.shape
    tiles_per_mod = rows_per_mod // tm
    return pl.pallas_call(
        _combine_kernel,
        grid_spec=pltpu.PrefetchScalarGridSpec(
            num_scalar_prefetch=1,
            grid=(n // tm,),
            in_specs=[pl.BlockSpec(memory_space=pl.ANY),
                      pl.BlockSpec((tm, d), lambda i, s: (i, 0)),
                      pl.BlockSpec((tm, LANES), lambda i, s: (i, 0)),
                      pl.BlockSpec((1, 1, d), lambda i, s: (i // tiles_per_mod, 0, 0))],
            out_specs=pl.BlockSpec((tm, d), lambda i, s: (i, 0)),
            scratch_shapes=[pltpu.VMEM((2, TOP_K, tm, d), jnp.float32),
                            pltpu.SemaphoreType.DMA((2,))]),
        out_shape=jax.ShapeDtypeStruct((n, d), jnp.float32),
        compiler_params=_cparams(("arbitrary",)),
        name="moe_combine",
    )(slots_flat, ys, hx, wgt, gt2)


def _rope_tables(n_t):
    rows = n_t // GRID_W
    row_ids = jnp.repeat(jnp.arange(rows), GRID_W).astype(jnp.float32)
    col_ids = jnp.tile(jnp.arange(GRID_W), rows).astype(jnp.float32)
    freqs = jnp.exp(-math.log(ROPE_THETA) * jnp.arange(ROPE_PAIRS_AXIS, dtype=jnp.float32) / ROPE_PAIRS_AXIS)
    ang = jnp.concatenate([row_ids[:, None] * freqs, col_ids[:, None] * freqs], axis=-1)
    cos_e = jnp.repeat(jnp.cos(ang), 2, axis=-1)
    sin = jnp.sin(ang)
    sin_s = jnp.stack([-sin, sin], axis=-1).reshape(n_t, HEAD_DIM)
    return cos_e, sin_s


def _gate_layouts(og, n_b, n_t):
    g = og[:, :4 * M_HEADS].reshape(n_b, n_t, 4, M_HEADS)
    return g.transpose(0, 3, 2, 1), g.transpose(0, 3, 1, 2)


def _layer(x, c, ctx, c_ctx, w_mod, b_mod, g_norm1, g_norm2, w_in, b_in, g_q, g_k, g_mh,
           w_br_m, w_br_a, w_out, w_router, b_router, w_gu, b_gu, w_dn, b_dn):
    n_b, n_t, d = x.shape
    n_c = ctx.shape[1]
    bf = jnp.bfloat16

    w_lat = jnp.concatenate([w_in[:, :_O_GATES], w_in[:, _O_AQ:]], axis=1).astype(bf)
    b_lat = jnp.concatenate([b_in[:_O_GATES], b_in[_O_AQ:]]).reshape(1, _P_COLS)
    w_ctx = jnp.concatenate([w_in[:, _O_MK:_O_MO], w_in[:, _O_AK:_O_MG]], axis=1).astype(bf)
    b_ctx = jnp.concatenate([b_in[_O_MK:_O_MO], b_in[_O_AK:_O_MG]]).reshape(1, _C_COLS)
    w_gate = jnp.pad(w_in[:, _O_GATES:_O_AQ], ((0, 0), (0, LANES - 4 * M_HEADS)))
    b_gate = jnp.pad(b_in[_O_GATES:_O_AQ], (0, LANES - 4 * M_HEADS)).reshape(1, LANES)
    wr = jnp.pad(w_router, ((0, 0), (0, LANES - N_EXPERTS)))
    wr_hi = wr.astype(bf)
    wr_lo = (wr - wr_hi.astype(jnp.float32)).astype(bf)
    br = jnp.pad(b_router, (0, LANES - N_EXPERTS)).reshape(1, LANES)

    c8 = jnp.zeros((8, d), jnp.float32).at[:n_b].set(c).at[n_b].set(c_ctx)
    mod = _modulation(c8, w_mod, b_mod)
    mod6 = mod.reshape(8, 6, d)
    sh1x, sc1x, gt1x, sh2x, sc2x, gt2x = [mod6[:n_b, i].reshape(n_b, 1, d) for i in range(6)]
    sh1c, sc1c = [mod6[n_b:n_b + 1, i].reshape(1, 1, d) for i in range(2)]

    g1 = g_norm1.reshape(1, d)
    x2 = x.reshape(n_b * n_t, d)
    px2, ogx = _in_projection(x2, g1, sc1x, sh1x, w_lat, b_lat, w_gate, b_gate, rows_per_mod=n_t)
    pc2, ogc = _in_projection(ctx.reshape(n_b * n_c, d), g1, sc1c, sh1c, w_ctx, b_ctx, w_gate, b_gate,
                              rows_per_mod=n_b * n_c, tm=512)
    px = px2.reshape(n_b, n_t, _P_COLS)
    pc = pc2.reshape(n_b, n_c, _C_COLS)

    cos_e, sin_s = _rope_tables(n_t)
    a_out = _attention(px, pc, cos_e, sin_s, g_q.reshape(1, HEAD_DIM), g_k.reshape(1, HEAD_DIM))

    g_rows, g_cols = _gate_layouts(ogx, n_b, n_t)
    g_rows_c, g_cols_c = _gate_layouts(ogc, n_b, n_c)
    h_fwd = _mlstm(px, pc, g_rows, g_cols, g_rows_c, g_cols_c, backward=False)
    m_out = _mlstm(px, pc, g_rows, g_cols, g_rows_c, g_cols_c, backward=True,
                   h_fwd=h_fwd, g_mh=g_mh.reshape(1, M_HEADS * MV_DIM))

    n = n_b * n_t
    z = _merge(m_out.reshape(n, d), a_out.reshape(n, d), px2, w_br_m.astype(bf), w_br_a.astype(bf))
    hx, u2, top_idx, top_w, rank, cnt = _outproj_router(
        z, x2, w_out.astype(bf), gt1x, g_norm2.reshape(1, d), sc2x, sh2x, wr_hi, wr_lo, br, rows_per_mod=n_t)

    counts = cnt[0, :N_EXPERTS].astype(jnp.int32)
    start, nblk, tail, gap_rows, n_rows = _row_layout(counts, n * TOP_K)
    meta = (start, nblk, tail)
    slots = (start[top_idx[:, :TOP_K]] + rank[:, :TOP_K]).astype(jnp.int32).reshape(n * TOP_K)

    xs = _dispatch(slots, gap_rows, tail, u2, n_rows)
    hid = _expert_up(meta, xs, w_gu, b_gu)
    ys = _expert_down(meta, hid, w_dn, b_dn)
    out = _combine(slots, ys, hx, top_w, gt2x, rows_per_mod=n_t)
    return out.reshape(n_b, n_t, d)


def kernel(x, c, ctx, c_ctx, w_mod, b_mod, g_norm1, g_norm2, w_in, b_in, g_q, g_k, g_mh, w_br_m, w_br_a, w_out,
           w_router, b_router, w_gu, b_gu, w_dn, b_dn):
    assert w_mod.shape[0] == 1, "single layer: the context stream has no consumer after it"
    return _layer(x, c, ctx, c_ctx, w_mod[0], b_mod[0], g_norm1[0], g_norm2[0], w_in[0], b_in[0], g_q[0], g_k[0],
                  g_mh[0], w_br_m[0], w_br_a[0], w_out[0], w_router[0], b_router[0], w_gu[0], b_gu[0],
                  w_dn[0], b_dn[0])
```

```python
import functools
import math

import jax
import jax.numpy as jnp
from jax import lax
from jax.experimental import pallas as pl
from jax.experimental.pallas import tpu as pltpu

D_MODEL = 2048
GRID_W = 64
HEAD_DIM = 128
A_HEADS = 16
KV_HEADS = 4
Q_GROUP = A_HEADS // KV_HEADS
ROPE_THETA = 10000.0
ROPE_PAIRS_AXIS = HEAD_DIM // 4
M_HEADS = 4
MV_DIM = D_MODEL // M_HEADS
MQK_DIM = MV_DIM // 2
N_EXPERTS = 32
TOP_K = 4
D_FF = D_MODEL
SWIGLU_LIMIT = 7.0
SWIGLU_ALPHA = 1.702
EPS = 1e-6

_O_MQ, _O_MK, _O_MV, _O_MO = 0, 1024, 2048, 4096
_O_GATES = 6144
_O_AQ, _O_AK, _O_AV, _O_MG = 6160, 8208, 8720, 9232
_F_IN = 13328
_P_MQ, _P_MK, _P_MV, _P_MO, _P_AQ, _P_AK, _P_AV, _P_GM, _P_GA = (
    0, 1024, 2048, 4096, 6144, 8192, 8704, 9216, 11264)
_P_COLS = 13312
_C_MK, _C_MV, _C_AK, _C_AV = 0, 1024, 3072, 3584
_C_COLS = 4096

LANES = 128
M_CHUNK = 256
EXPERT_ROWS = 256
ROW_ALIGN = 16
EXPERT_COL_TILES = 2
NEG_BIG = -1e30
VMEM_LIMIT = 56 * 1024 * 1024

_HI = lax.Precision.HIGHEST


def _cparams(sem, vmem=VMEM_LIMIT):
    return pltpu.CompilerParams(dimension_semantics=sem, vmem_limit_bytes=vmem)


def _mod_kernel(c_ref, w_ref, b_ref, o_ref):
    c = c_ref[...]
    a = c * jax.nn.sigmoid(c)
    o_ref[...] = lax.dot_general(a, w_ref[...], (((1,), (0,)), ((), ())), precision=_HI,
                                 preferred_element_type=jnp.float32) + b_ref[...]


def _modulation(c8, w_mod, b_mod):
    d, n = w_mod.shape
    tn = 1024
    return pl.pallas_call(
        _mod_kernel,
        grid=(n // tn,),
        in_specs=[pl.BlockSpec((8, d), lambda j: (0, 0)),
                  pl.BlockSpec((d, tn), lambda j: (0, j)),
                  pl.BlockSpec((1, tn), lambda j: (0, j))],
        out_specs=pl.BlockSpec((8, tn), lambda j: (0, j)),
        out_shape=jax.ShapeDtypeStruct((8, n), jnp.float32),
        compiler_params=_cparams(("arbitrary",)),
        name="modulation",
    )(c8, w_mod, b_mod.reshape(1, n))


def _adaln_kernel(x_ref, g_ref, sc_ref, sh_ref, wg_ref, bg_ref, u_ref, og_ref):
    x = x_ref[...]
    y = x * lax.rsqrt(jnp.mean(x * x, axis=-1, keepdims=True) + EPS) * g_ref[...]
    u = y * (1.0 + sc_ref[0]) + sh_ref[0]
    u_ref[...] = u.astype(jnp.bfloat16)
    og_ref[...] = lax.dot_general(u, wg_ref[...], (((1,), (0,)), ((), ())), precision=_HI,
                                  preferred_element_type=jnp.float32) + bg_ref[...]


def _adaln_norm(x2, g, sc, sh, w_in, b_in2, rows_per_mod, tm=512):
    n, d = x2.shape
    tiles_per_mod = rows_per_mod // tm
    gate_blk = _O_GATES // LANES
    return pl.pallas_call(
        _adaln_kernel,
        grid=(n // tm,),
        in_specs=[pl.BlockSpec((tm, d), lambda i: (i, 0)),
                  pl.BlockSpec((1, d), lambda i: (0, 0)),
                  pl.BlockSpec((1, 1, d), lambda i: (i // tiles_per_mod, 0, 0)),
                  pl.BlockSpec((1, 1, d), lambda i: (i // tiles_per_mod, 0, 0)),
                  pl.BlockSpec((d, LANES), lambda i: (0, gate_blk)),
                  pl.BlockSpec((1, LANES), lambda i: (0, gate_blk))],
        out_specs=[pl.BlockSpec((tm, d), lambda i: (i, 0)),
                   pl.BlockSpec((tm, LANES), lambda i: (i, 0))],
        out_shape=[jax.ShapeDtypeStruct((n, d), jnp.bfloat16),
                   jax.ShapeDtypeStruct((n, LANES), jnp.float32)],
        compiler_params=_cparams(("arbitrary",)),
        name="adaln_norm",
    )(x2, g, sc, sh, w_in, b_in2)


PROJ_TN = 1024
GATE_COLS = 4 * M_HEADS


def _proj_kernel(tiles_ref, u_ref, w_ref, wx_ref, b_ref, bx_ref, o_ref, w_s, b_s):
    t = tiles_ref[pl.program_id(0)]
    shifted = t * PROJ_TN >= _O_GATES

    @pl.when((pl.program_id(1) == 0) & jnp.logical_not(shifted))
    def _():
        w_s[...] = w_ref[...].astype(jnp.bfloat16)
        b_s[...] = b_ref[...]

    @pl.when((pl.program_id(1) == 0) & shifted)
    def _():
        w_s[...] = jnp.concatenate([w_ref[:, GATE_COLS:], wx_ref[:, :GATE_COLS]], axis=1).astype(jnp.bfloat16)
        b_s[...] = jnp.concatenate([b_ref[:, GATE_COLS:], bx_ref[:, :GATE_COLS]], axis=1)

    o_ref[...] = (jnp.dot(u_ref[...], w_s[...], preferred_element_type=jnp.float32) + b_s[...]).astype(o_ref.dtype)


def _in_projection(u, w_in, b_in2, tiles, tm=1024):
    n, d = u.shape
    n_tiles = tiles.shape[0]
    tn = PROJ_TN
    sub = tn // LANES
    return pl.pallas_call(
        _proj_kernel,
        grid_spec=pltpu.PrefetchScalarGridSpec(
            num_scalar_prefetch=1,
            grid=(n_tiles, n // tm),
            in_specs=[pl.BlockSpec((tm, d), lambda j, i, t: (i, 0)),
                      pl.BlockSpec((d, tn), lambda j, i, t: (0, t[j])),
                      pl.BlockSpec((d, LANES), lambda j, i, t: (0, sub * (t[j] + 1))),
                      pl.BlockSpec((1, tn), lambda j, i, t: (0, t[j])),
                      pl.BlockSpec((1, LANES), lambda j, i, t: (0, sub * (t[j] + 1)))],
            out_specs=pl.BlockSpec((tm, tn), lambda j, i, t: (i, j)),
            scratch_shapes=[pltpu.VMEM((d, tn), jnp.bfloat16), pltpu.VMEM((1, tn), jnp.float32)]),
        out_shape=jax.ShapeDtypeStruct((n, n_tiles * tn), jnp.bfloat16),
        compiler_params=_cparams(("arbitrary", "arbitrary")),
        name="in_projection",
    )(tiles, u, w_in, w_in, b_in2, b_in2)


def _rms_head(x, g):
    return x * lax.rsqrt(jnp.mean(x * x, axis=-1, keepdims=True) + EPS) * g


def _rope(x, cos_e, sin_s):
    lane = lax.broadcasted_iota(jnp.int32, x.shape, 1)
    swapped = jnp.where(lane % 2 == 0, pltpu.roll(x, LANES - 1, 1), pltpu.roll(x, 1, 1))
    return x * cos_e + swapped * sin_s


def _attn_kernel(q_ref, kx_ref, vx_ref, kc_ref, vc_ref, cosq_ref, sinq_ref, cosk_ref, sink_ref,
                 gq_ref, gk_ref, o_ref, k_s, v_s):
    n_t = kx_ref.shape[1]

    @pl.when(pl.program_id(2) == 0)
    def _():
        kx = _rms_head(kx_ref[0].astype(jnp.float32), gk_ref[...])
        k_s[0:n_t, :] = _rope(kx, cosk_ref[...], sink_ref[...]).astype(jnp.bfloat16)
        k_s[n_t:, :] = _rms_head(kc_ref[0].astype(jnp.float32), gk_ref[...]).astype(jnp.bfloat16)
        v_s[0:n_t, :] = vx_ref[0]
        v_s[n_t:, :] = vc_ref[0]

    scale = HEAD_DIM ** -0.5
    kk = k_s[...]
    vv = v_s[...]
    for g in range(Q_GROUP):
        q = q_ref[0, :, g * HEAD_DIM:(g + 1) * HEAD_DIM].astype(jnp.float32)
        q = _rope(_rms_head(q, gq_ref[...]), cosq_ref[...], sinq_ref[...]) * scale
        s = lax.dot_general(q.astype(jnp.bfloat16), kk, (((1,), (1,)), ((), ())),
                            preferred_element_type=jnp.float32)
        p = jnp.exp(s - jnp.max(s, axis=-1, keepdims=True))
        l = jnp.sum(p, axis=-1, keepdims=True)
        o = jnp.dot(p.astype(jnp.bfloat16), vv, preferred_element_type=jnp.float32) / l
        o_ref[0, :, g * HEAD_DIM:(g + 1) * HEAD_DIM] = o.astype(o_ref.dtype)


def _attention(px, pc, cos_e, sin_s, g_q, g_k, tq=256):
    n_b, n_t, _ = px.shape
    n_c = pc.shape[1]
    gw = Q_GROUP * HEAD_DIM
    return pl.pallas_call(
        _attn_kernel,
        grid=(n_b, KV_HEADS, n_t // tq),
        in_specs=[pl.BlockSpec((1, tq, gw), lambda b, h, i: (b, i, _P_AQ // gw + h)),
                  pl.BlockSpec((1, n_t, HEAD_DIM), lambda b, h, i: (b, 0, _P_AK // HEAD_DIM + h)),
                  pl.BlockSpec((1, n_t, HEAD_DIM), lambda b, h, i: (b, 0, _P_AV // HEAD_DIM + h)),
                  pl.BlockSpec((1, n_c, HEAD_DIM), lambda b, h, i: (b, 0, _C_AK // HEAD_DIM + h)),
                  pl.BlockSpec((1, n_c, HEAD_DIM), lambda b, h, i: (b, 0, _C_AV // HEAD_DIM + h)),
                  pl.BlockSpec((tq, HEAD_DIM), lambda b, h, i: (i, 0)),
                  pl.BlockSpec((tq, HEAD_DIM), lambda b, h, i: (i, 0)),
                  pl.BlockSpec((n_t, HEAD_DIM), lambda b, h, i: (0, 0)),
                  pl.BlockSpec((n_t, HEAD_DIM), lambda b, h, i: (0, 0)),
                  pl.BlockSpec((1, HEAD_DIM), lambda b, h, i: (0, 0)),
                  pl.BlockSpec((1, HEAD_DIM), lambda b, h, i: (0, 0))],
        out_specs=pl.BlockSpec((1, tq, gw), lambda b, h, i: (b, i, h)),
        out_shape=jax.ShapeDtypeStruct((n_b, n_t, A_HEADS * HEAD_DIM), jnp.bfloat16),
        scratch_shapes=[pltpu.VMEM((n_t + n_c, HEAD_DIM), jnp.bfloat16),
                        pltpu.VMEM((n_t + n_c, HEAD_DIM), jnp.bfloat16)],
        compiler_params=_cparams(("arbitrary", "arbitrary", "arbitrary")),
        name="attention",
    )(px, px, px, pc, pc, cos_e, sin_s, cos_e, sin_s, g_q, g_k)


def _log_sigmoid(x):
    return jnp.minimum(x, 0.0) - jnp.log(1.0 + jnp.exp(-jnp.abs(x)))


def _mlstm_chunk(q, k, v, i_row, f_row, i_col, f_col, ct_ref, n_ref, m_ref, backward, want_h):
    n_l = k.shape[0]
    lf_row = _log_sigmoid(f_row)
    lf_col = _log_sigmoid(f_col)
    t_idx = lax.broadcasted_iota(jnp.int32, (n_l, n_l), 0)
    s_idx = lax.broadcasted_iota(jnp.int32, (n_l, n_l), 1)
    seen = (s_idx >= t_idx) if backward else (s_idx <= t_idx)
    b_col = jnp.sum(jnp.where(seen, lf_row, 0.0), axis=1, keepdims=True)
    seen_t = (t_idx >= s_idx) if backward else (t_idx <= s_idx)
    b_row = jnp.sum(jnp.where(seen_t, lf_col, 0.0), axis=0, keepdims=True)
    b_end = jnp.sum(lf_row, axis=1, keepdims=True)
    m_prev = m_ref[...]
    ct = ct_ref[...]
    n_vec = n_ref[...]
    kf = k.astype(jnp.float32)

    h = None
    if want_h:
        d_intra = jnp.where(seen, b_col - b_row + i_row, NEG_BIG)
        d_inter = b_col + m_prev
        m_t = jnp.maximum(d_inter, jnp.max(d_intra, axis=1, keepdims=True))
        qk = lax.dot_general(q, k, (((1,), (1,)), ((), ())), preferred_element_type=jnp.float32)
        s = qk * jnp.exp(d_intra - m_t)
        w_inter = jnp.exp(d_inter - m_t)
        num = (jnp.dot(s.astype(jnp.bfloat16), v, preferred_element_type=jnp.float32)
               + w_inter * jnp.dot(q, ct.astype(jnp.bfloat16), preferred_element_type=jnp.float32))
        den = (jnp.sum(s, axis=1, keepdims=True)
               + w_inter * jnp.sum(q.astype(jnp.float32) * n_vec, axis=1, keepdims=True))
        h = num / jnp.maximum(jnp.abs(den), jnp.exp(-m_t))

    g_col = b_end - b_col + i_col
    m_new = jnp.maximum(b_end + m_prev, jnp.max(g_col, axis=0, keepdims=True))
    w_s = jnp.exp(g_col - m_new)
    w_c = jnp.exp(b_end + m_prev - m_new)
    wv = (w_s * v.astype(jnp.float32)).astype(jnp.bfloat16)
    ct_ref[...] = w_c * ct + lax.dot_general(k, wv, (((0,), (0,)), ((), ())),
                                             preferred_element_type=jnp.float32)
    n_ref[...] = w_c * n_vec + jnp.sum(w_s * kf, axis=0, keepdims=True)
    m_ref[...] = m_new
    return h


def _mlstm_kernel(backward, *refs):
    if backward:
        (q_ref, k_ref, v_ref, kc_ref, vc_ref, gr_ref, gc_ref, grc_ref, gcc_ref,
         hf_ref, op_ref, gmh_ref, o_ref, ct_ref, n_ref, m_ref) = refs
    else:
        (q_ref, k_ref, v_ref, kc_ref, vc_ref, gr_ref, gc_ref, grc_ref, gcc_ref,
         o_ref, ct_ref, n_ref, m_ref) = refs
    gi, gf = (2, 3) if backward else (0, 1)
    step = pl.program_id(2)

    @pl.when(step == 0)
    def _():
        ct_ref[...] = jnp.zeros_like(ct_ref)
        n_ref[...] = jnp.zeros_like(n_ref)
        m_ref[...] = jnp.zeros_like(m_ref)
        _mlstm_chunk(None, kc_ref[0], vc_ref[0],
                     grc_ref[0, 0, gi:gi + 1, :], grc_ref[0, 0, gf:gf + 1, :],
                     gcc_ref[0, 0, :, gi:gi + 1], gcc_ref[0, 0, :, gf:gf + 1],
                     ct_ref, n_ref, m_ref, backward, False)

    @pl.when(step > 0)
    def _():
        q = (q_ref[0].astype(jnp.float32) * (MQK_DIM ** -0.5)).astype(jnp.bfloat16)
        h = _mlstm_chunk(q, k_ref[0], v_ref[0],
                         gr_ref[0, 0, gi:gi + 1, :], gr_ref[0, 0, gf:gf + 1, :],
                         gc_ref[0, 0, :, gi:gi + 1], gc_ref[0, 0, :, gf:gf + 1],
                         ct_ref, n_ref, m_ref, backward, True)
        if backward:
            ht = h + hf_ref[0]
            y = ht * lax.rsqrt(jnp.mean(ht * ht, axis=-1, keepdims=True) + EPS) * gmh_ref[...]
            o_ref[0] = (y * jax.nn.sigmoid(op_ref[0].astype(jnp.float32))).astype(o_ref.dtype)
        else:
            o_ref[0] = h


def _mlstm(px, pc, g_rows, g_cols, g_rows_c, g_cols_c, backward, h_fwd=None, g_mh=None):
    n_b, n_t, _ = px.shape
    n_c = pc.shape[1]
    n_l = M_CHUNK
    assert n_c == n_l and n_t % n_l == 0
    n_chunk = n_t // n_l

    if backward:
        def cidx(s):
            return jnp.minimum(n_chunk - s, n_chunk - 1)
    else:
        def cidx(s):
            return jnp.maximum(s - 1, 0)

    in_specs = [
        pl.BlockSpec((1, n_l, MQK_DIM), lambda b, h, s: (b, cidx(s), _P_MQ // MQK_DIM + h)),
        pl.BlockSpec((1, n_l, MQK_DIM), lambda b, h, s: (b, cidx(s), _P_MK // MQK_DIM + h)),
        pl.BlockSpec((1, n_l, MV_DIM), lambda b, h, s: (b, cidx(s), _P_MV // MV_DIM + h)),
        pl.BlockSpec((1, n_c, MQK_DIM), lambda b, h, s: (b, 0, _C_MK // MQK_DIM + h)),
        pl.BlockSpec((1, n_c, MV_DIM), lambda b, h, s: (b, 0, _C_MV // MV_DIM + h)),
        pl.BlockSpec((1, 1, 4, n_l), lambda b, h, s: (b, h, 0, cidx(s))),
        pl.BlockSpec((1, 1, n_l, 4), lambda b, h, s: (b, h, cidx(s), 0)),
        pl.BlockSpec((1, 1, 4, n_c), lambda b, h, s: (b, h, 0, 0)),
        pl.BlockSpec((1, 1, n_c, 4), lambda b, h, s: (b, h, 0, 0)),
    ]
    args = [px, px, px, pc, pc, g_rows, g_cols, g_rows_c, g_cols_c]
    if backward:
        in_specs += [
            pl.BlockSpec((1, n_l, MV_DIM), lambda b, h, s: (b, cidx(s), h)),
            pl.BlockSpec((1, n_l, MV_DIM), lambda b, h, s: (b, cidx(s), _P_MO // MV_DIM + h)),
            pl.BlockSpec((1, MV_DIM), lambda b, h, s: (0, h)),
        ]
        args += [h_fwd, px, g_mh]
        out_dtype = jnp.bfloat16
    else:
        out_dtype = jnp.float32
    return pl.pallas_call(
        functools.partial(_mlstm_kernel, backward),
        grid=(n_b, M_HEADS, n_chunk + 1),
        in_specs=in_specs,
        out_specs=pl.BlockSpec((1, n_l, MV_DIM), lambda b, h, s: (b, cidx(s), h)),
        out_shape=jax.ShapeDtypeStruct((n_b, n_t, M_HEADS * MV_DIM), out_dtype),
        scratch_shapes=[pltpu.VMEM((MQK_DIM, MV_DIM), jnp.float32),
                        pltpu.VMEM((1, MQK_DIM), jnp.float32),
                        pltpu.VMEM((1, 1), jnp.float32)],
        compiler_params=_cparams(("arbitrary", "arbitrary", "arbitrary")),
        name="mlstm_bwd" if backward else "mlstm_fwd",
    )(*args)


def _merge_kernel(m_ref, a_ref, gm_ref, ga_ref, wm_ref, wa_ref, o_ref):
    zm = jnp.dot(m_ref[...], wm_ref[...], preferred_element_type=jnp.float32)
    za = jnp.dot(a_ref[...], wa_ref[...], preferred_element_type=jnp.float32)
    z = (jax.nn.sigmoid(gm_ref[...].astype(jnp.float32)) * zm
         + jax.nn.sigmoid(ga_ref[...].astype(jnp.float32)) * za)
    o_ref[...] = z.astype(o_ref.dtype)


def _merge(m_out, a_out, px2, w_br_m, w_br_a, tm=512, tn=512):
    n, d = m_out.shape
    return pl.pallas_call(
        _merge_kernel,
        grid=(n // tm, d // tn),
        in_specs=[pl.BlockSpec((tm, d), lambda i, j: (i, 0)),
                  pl.BlockSpec((tm, d), lambda i, j: (i, 0)),
                  pl.BlockSpec((tm, tn), lambda i, j: (i, _P_GM // tn + j)),
                  pl.BlockSpec((tm, tn), lambda i, j: (i, _P_GA // tn + j)),
                  pl.BlockSpec((d, tn), lambda i, j: (0, j)),
                  pl.BlockSpec((d, tn), lambda i, j: (0, j))],
        out_specs=pl.BlockSpec((tm, tn), lambda i, j: (i, j)),
        out_shape=jax.ShapeDtypeStruct((n, d), jnp.bfloat16),
        compiler_params=_cparams(("arbitrary", "arbitrary")),
        name="merge",
    )(m_out, a_out, px2, px2, w_br_m, w_br_a)


def _split_bf16(x):
    hi = x.astype(jnp.bfloat16)
    lo = (x - hi.astype(jnp.float32)).astype(jnp.bfloat16)
    return hi, lo


def _outproj_router_kernel(z_ref, x_ref, wo_ref, gt_ref, g2_ref, sc_ref, sh_ref, wrh_ref, wrl_ref, br_ref,
                           hx_ref, u2_ref, idx_ref, wgt_ref, rank_ref, cnt_ref, carry_ref):
    tm = z_ref.shape[0]

    @pl.when(pl.program_id(0) == 0)
    def _():
        carry_ref[...] = jnp.zeros_like(carry_ref)

    y = jnp.dot(z_ref[...], wo_ref[...], preferred_element_type=jnp.float32)
    hx = x_ref[...] + gt_ref[0] * y
    hx_ref[...] = hx
    u2 = (hx * lax.rsqrt(jnp.mean(hx * hx, axis=-1, keepdims=True) + EPS) * g2_ref[...]
          * (1.0 + sc_ref[0]) + sh_ref[0])
    u2_ref[...] = u2

    u_hi, u_lo = _split_bf16(u2)
    logits = (jnp.dot(u_hi, wrh_ref[...], preferred_element_type=jnp.float32)
              + jnp.dot(u_lo, wrh_ref[...], preferred_element_type=jnp.float32)
              + jnp.dot(u_hi, wrl_ref[...], preferred_element_type=jnp.float32)) + br_ref[...]
    lane = lax.broadcasted_iota(jnp.int32, (tm, LANES), 1)
    logits = jnp.where(lane < N_EXPERTS, logits, NEG_BIG)

    idx_out = jnp.zeros((tm, LANES), jnp.int32)
    val_out = jnp.zeros((tm, LANES), jnp.float32)
    chosen = jnp.zeros((tm, LANES), jnp.float32)
    sel = []
    top0 = None
    for k in range(TOP_K):
        mx = jnp.max(logits, axis=-1, keepdims=True)
        ix = jnp.min(jnp.where(logits == mx, lane, LANES), axis=-1, keepdims=True)
        hit = lane == ix
        if k == 0:
            top0 = mx
        idx_out = jnp.where(lane == k, ix, idx_out)
        val_out = jnp.where(lane == k, jnp.exp(mx - top0), val_out)
        chosen = jnp.where(hit, 1.0, chosen)
        sel.append(hit)
        logits = jnp.where(hit, NEG_BIG, logits)
    idx_ref[...] = idx_out
    wgt_ref[...] = val_out / jnp.sum(val_out, axis=-1, keepdims=True)

    r_idx = lax.broadcasted_iota(jnp.int32, (tm, tm), 0)
    c_idx = lax.broadcasted_iota(jnp.int32, (tm, tm), 1)
    before = jnp.where(c_idx < r_idx, 1.0, 0.0).astype(jnp.bfloat16)
    prior = jnp.dot(before, chosen.astype(jnp.bfloat16), preferred_element_type=jnp.float32) + carry_ref[...]
    rank_out = jnp.zeros((tm, LANES), jnp.int32)
    for k in range(TOP_K):
        rk = jnp.sum(jnp.where(sel[k], prior, 0.0), axis=-1, keepdims=True)
        rank_out = jnp.where(lane == k, rk.astype(jnp.int32), rank_out)
    rank_ref[...] = rank_out
    carry_ref[...] = carry_ref[...] + jnp.sum(chosen, axis=0, keepdims=True)
    cnt_ref[...] = jnp.broadcast_to(carry_ref[...], cnt_ref.shape)


def _outproj_router(z, x2, w_out, gt1, g2, sc2, sh2, wr_hi, wr_lo, br, rows_per_mod, tm=256):
    n, d = z.shape
    tiles_per_mod = rows_per_mod // tm
    row = lambda i: (i, 0)
    fixed = lambda i: (0, 0)
    modi = lambda i: (i // tiles_per_mod, 0, 0)
    return pl.pallas_call(
        _outproj_router_kernel,
        grid=(n // tm,),
        in_specs=[pl.BlockSpec((tm, d), row),
                  pl.BlockSpec((tm, d), row),
                  pl.BlockSpec((d, d), fixed),
                  pl.BlockSpec((1, 1, d), modi),
                  pl.BlockSpec((1, d), fixed),
                  pl.BlockSpec((1, 1, d), modi),
                  pl.BlockSpec((1, 1, d), modi),
                  pl.BlockSpec((d, LANES), fixed),
                  pl.BlockSpec((d, LANES), fixed),
                  pl.BlockSpec((1, LANES), fixed)],
        out_specs=[pl.BlockSpec((tm, d), row),
                   pl.BlockSpec((tm, d), row),
                   pl.BlockSpec((tm, LANES), row),
                   pl.BlockSpec((tm, LANES), row),
                   pl.BlockSpec((tm, LANES), row),
                   pl.BlockSpec((8, LANES), fixed)],
        out_shape=[jax.ShapeDtypeStruct((n, d), jnp.float32),
                   jax.ShapeDtypeStruct((n, d), jnp.float32),
                   jax.ShapeDtypeStruct((n, LANES), jnp.int32),
                   jax.ShapeDtypeStruct((n, LANES), jnp.float32),
                   jax.ShapeDtypeStruct((n, LANES), jnp.int32),
                   jax.ShapeDtypeStruct((8, LANES), jnp.float32)],
        scratch_shapes=[pltpu.VMEM((1, LANES), jnp.float32)],
        compiler_params=_cparams(("arbitrary",)),
        name="outproj_router",
    )(z, x2, w_out, gt1, g2, sc2, sh2, wr_hi, wr_lo, br)


def _start_zero_rows(zbuf, dst_rows, start, count, sem, wait):
    n_full = count // EXPERT_ROWS
    rem = count % EXPERT_ROWS
    pieces = [(i < n_full, start + i * EXPERT_ROWS, EXPERT_ROWS) for i in range(3)]
    off = start + n_full * EXPERT_ROWS
    p = EXPERT_ROWS // 2
    while p >= ROW_ALIGN:
        pieces.append(((rem & p) != 0, off + (rem // (2 * p)) * (2 * p), p))
        p //= 2
    for cond, row, size in pieces:
        @pl.when(cond)
        def _(row=row, size=size):
            cp = pltpu.make_async_copy(zbuf.at[pl.ds(0, size)], dst_rows(pl.multiple_of(row, ROW_ALIGN), size), sem)
            cp.wait() if wait else cp.start()


def _zero_rows(zbuf, dst_rows, start, count, sem):
    _start_zero_rows(zbuf, dst_rows, start, count, sem, wait=False)
    _start_zero_rows(zbuf, dst_rows, start, count, sem, wait=True)


def _dispatch_kernel(slot_ref, gap_ref, tail_ref, u_ref, xs_ref, zbuf, sem, zsem):
    tm = u_ref.shape[0]
    n_rows = xs_ref.shape[0]
    base = pl.program_id(0) * (tm * TOP_K)

    @pl.when(pl.program_id(0) == 0)
    def _():
        zbuf[...] = jnp.zeros_like(zbuf)
        for wait in (False, True):
            for e in range(N_EXPERTS):
                @pl.when(gap_ref[e] >= 0)
                def _(e=e, wait=wait):
                    cp = pltpu.make_async_copy(
                        zbuf.at[pl.ds(0, ROW_ALIGN)],
                        xs_ref.at[pl.ds(pl.multiple_of(gap_ref[e], ROW_ALIGN), ROW_ALIGN)], zsem)
                    cp.wait() if wait else cp.start()
        _zero_rows(zbuf, lambda r, s: xs_ref.at[pl.ds(r, s)], tail_ref[0], n_rows - tail_ref[0], zsem)

    def issue(r, carry):
        for k in range(TOP_K):
            s = slot_ref[base + r * TOP_K + k]
            pltpu.make_async_copy(u_ref.at[pl.ds(r, 1)], xs_ref.at[pl.ds(s, 1)], sem).start()
        return carry

    lax.fori_loop(0, tm, issue, 0)

    def drain(r, carry):
        for k in range(TOP_K):
            pltpu.make_async_copy(u_ref.at[pl.ds(0, 1)], xs_ref.at[pl.ds(0, 1)], sem).wait()
        return carry

    lax.fori_loop(0, tm, drain, 0)


def _dispatch(slots_flat, gap_rows, tail, u2, n_rows, tm=256):
    n, d = u2.shape
    return pl.pallas_call(
        _dispatch_kernel,
        grid_spec=pltpu.PrefetchScalarGridSpec(
            num_scalar_prefetch=3,
            grid=(n // tm,),
            in_specs=[pl.BlockSpec((tm, d), lambda i, *_: (i, 0))],
            out_specs=pl.BlockSpec(memory_space=pl.ANY),
            scratch_shapes=[pltpu.VMEM((EXPERT_ROWS, d), jnp.float32),
                            pltpu.SemaphoreType.DMA(()), pltpu.SemaphoreType.DMA(())]),
        out_shape=jax.ShapeDtypeStruct((n_rows, d), jnp.float32),
        compiler_params=_cparams(("arbitrary",)),
        name="moe_dispatch",
    )(slots_flat, gap_rows, tail, u2)


def _expert_rows_loop(start_ref, nblk_ref, tail_ref, order_ref, src_ref, dst_ref, ibuf, obuf, zbuf,
                      sem_in, sem_out, sem_z, prepare, compute):
    e = pl.program_id(0)
    f = pl.program_id(1)
    n_f = pl.num_programs(1)
    tile = obuf.shape[2]
    col = pl.multiple_of(f * tile, tile)
    nb = nblk_ref[e]
    base = start_ref[e]
    g0 = order_ref[0, e] + f * nb
    nxt = order_ref[1, e]

    def rows(first_row):
        return pl.ds(pl.multiple_of(first_row, ROW_ALIGN), EXPERT_ROWS)

    def in_copy(first_row, slot):
        return pltpu.make_async_copy(src_ref.at[rows(first_row)], ibuf.at[slot], sem_in.at[slot])

    def out_copy(k, slot):
        return pltpu.make_async_copy(obuf.at[slot], dst_ref.at[rows(base + k * EXPERT_ROWS), pl.ds(col, tile)],
                                     sem_out.at[slot])

    @pl.when((nb > 0) & (g0 == 0))
    def _():
        in_copy(base, 0).start()

    prepare()

    def body(k, carry):
        slot = lax.rem(g0 + k, 2)
        oslot = lax.rem(k, 2)

        @pl.when(k + 1 < nb)
        def _():
            in_copy(base + (k + 1) * EXPERT_ROWS, 1 - slot).start()

        @pl.when((k + 1 == nb) & (f + 1 < n_f))
        def _():
            in_copy(base, 1 - slot).start()

        @pl.when((k + 1 == nb) & (f + 1 == n_f) & (nxt >= 0))
        def _():
            in_copy(start_ref[jnp.maximum(nxt, 0)], 1 - slot).start()

        in_copy(base, slot).wait()

        @pl.when(k >= 2)
        def _():
            out_copy(k - 2, oslot).wait()

        obuf[oslot] = compute(ibuf[slot]).astype(obuf.dtype)
        out_copy(k, oslot).start()
        return carry

    lax.fori_loop(0, nb, body, 0)

    @pl.when(nb >= 2)
    def _():
        out_copy(nb - 2, lax.rem(nb, 2)).wait()

    @pl.when(nb >= 1)
    def _():
        out_copy(nb - 1, lax.rem(nb + 1, 2)).wait()

    @pl.when(e == pl.num_programs(0) - 1)
    def _():
        zbuf[...] = jnp.zeros_like(zbuf)
        _zero_rows(zbuf, lambda r, s: dst_ref.at[pl.ds(r, s), pl.ds(col, tile)],
                   tail_ref[1], dst_ref.shape[0] - tail_ref[1], sem_z)


def _expert_up_kernel(start_ref, nblk_ref, tail_ref, order_ref, xs_ref, wg_ref, wu_ref, bg_ref, bu_ref, hid_ref,
                      wg_s, wu_s, ibuf, obuf, zbuf, sem_in, sem_out, sem_z):
    def prepare():
        wg_s[...] = wg_ref[...].astype(jnp.bfloat16)
        wu_s[...] = wu_ref[...].astype(jnp.bfloat16)

    def compute(xb):
        x = xb.astype(jnp.bfloat16)
        gate = jnp.dot(x, wg_s[...], preferred_element_type=jnp.float32) + bg_ref[...]
        up = jnp.dot(x, wu_s[...], preferred_element_type=jnp.float32) + bu_ref[...]
        gate = jnp.minimum(gate, SWIGLU_LIMIT)
        up = jnp.clip(up, -SWIGLU_LIMIT, SWIGLU_LIMIT)
        return (up + 1.0) * gate * jax.nn.sigmoid(SWIGLU_ALPHA * gate)

    _expert_rows_loop(start_ref, nblk_ref, tail_ref, order_ref, xs_ref, hid_ref, ibuf, obuf, zbuf,
                      sem_in, sem_out, sem_z, prepare, compute)


def _expert_up(meta, xs, w_gu, b_gu):
    n_rows, d = xs.shape
    tf = D_FF // EXPERT_COL_TILES
    nfc = EXPERT_COL_TILES
    b3 = b_gu.reshape(N_EXPERTS, 1, 2 * D_FF)
    return pl.pallas_call(
        _expert_up_kernel,
        grid_spec=pltpu.PrefetchScalarGridSpec(
            num_scalar_prefetch=4,
            grid=(N_EXPERTS, nfc),
            in_specs=[pl.BlockSpec(memory_space=pl.ANY),
                      pl.BlockSpec((None, d, tf), lambda e, f, *_: (e, 0, f)),
                      pl.BlockSpec((None, d, tf), lambda e, f, *_: (e, 0, nfc + f)),
                      pl.BlockSpec((None, 1, tf), lambda e, f, *_: (e, 0, f)),
                      pl.BlockSpec((None, 1, tf), lambda e, f, *_: (e, 0, nfc + f))],
            out_specs=pl.BlockSpec(memory_space=pl.ANY),
            scratch_shapes=[pltpu.VMEM((d, tf), jnp.bfloat16), pltpu.VMEM((d, tf), jnp.bfloat16),
                            pltpu.VMEM((2, EXPERT_ROWS, d), jnp.float32),
                            pltpu.VMEM((2, EXPERT_ROWS, tf), jnp.bfloat16),
                            pltpu.VMEM((EXPERT_ROWS, tf), jnp.bfloat16),
                            pltpu.SemaphoreType.DMA((2,)), pltpu.SemaphoreType.DMA((2,)),
                            pltpu.SemaphoreType.DMA(())]),
        out_shape=jax.ShapeDtypeStruct((n_rows, D_FF), jnp.bfloat16),
        compiler_params=_cparams(("arbitrary", "arbitrary")),
        name="expert_up",
    )(*meta, xs, w_gu, w_gu, b3, b3)


def _expert_down_kernel(start_ref, nblk_ref, tail_ref, order_ref, hid_ref, wd_ref, bd_ref, ys_ref,
                        wd_s, ibuf, obuf, zbuf, sem_in, sem_out, sem_z):
    def prepare():
        wd_s[...] = wd_ref[...].astype(jnp.bfloat16)

    def compute(hb):
        return jnp.dot(hb, wd_s[...], preferred_element_type=jnp.float32) + bd_ref[...]

    _expert_rows_loop(start_ref, nblk_ref, tail_ref, order_ref, hid_ref, ys_ref, ibuf, obuf, zbuf,
                      sem_in, sem_out, sem_z, prepare, compute)


def _expert_down(meta, hid, w_dn, b_dn):
    n_rows, dff = hid.shape
    d = w_dn.shape[2]
    tn = d // EXPERT_COL_TILES
    b3 = b_dn.reshape(N_EXPERTS, 1, d)
    return pl.pallas_call(
        _expert_down_kernel,
        grid_spec=pltpu.PrefetchScalarGridSpec(
            num_scalar_prefetch=4,
            grid=(N_EXPERTS, d // tn),
            in_specs=[pl.BlockSpec(memory_space=pl.ANY),
                      pl.BlockSpec((None, dff, tn), lambda e, f, *_: (e, 0, f)),
                      pl.BlockSpec((None, 1, tn), lambda e, f, *_: (e, 0, f))],
            out_specs=pl.BlockSpec(memory_space=pl.ANY),
            scratch_shapes=[pltpu.VMEM((dff, tn), jnp.bfloat16),
                            pltpu.VMEM((2, EXPERT_ROWS, dff), jnp.bfloat16),
                            pltpu.VMEM((2, EXPERT_ROWS, tn), jnp.float32),
                            pltpu.VMEM((EXPERT_ROWS, tn), jnp.float32),
                            pltpu.SemaphoreType.DMA((2,)), pltpu.SemaphoreType.DMA((2,)),
                            pltpu.SemaphoreType.DMA(())]),
        out_shape=jax.ShapeDtypeStruct((n_rows, d), jnp.float32),
        compiler_params=_cparams(("arbitrary", "arbitrary")),
        name="expert_down",
    )(*meta, hid, w_dn, b3)


def _row_layout(counts, n_assign):
    c_al = (counts + ROW_ALIGN - 1) // ROW_ALIGN * ROW_ALIGN
    start = jnp.cumsum(c_al) - c_al
    total = jnp.sum(c_al)
    nblk = (counts + EXPERT_ROWS - 1) // EXPERT_ROWS
    covered = jnp.max(start + nblk * EXPERT_ROWS)
    gap_rows = jnp.where(counts > 0, start + c_al - ROW_ALIGN, -1)
    n_rows = n_assign + N_EXPERTS * ROW_ALIGN + EXPERT_ROWS
    i32 = lambda a: a.astype(jnp.int32)
    tail = i32(jnp.stack([total, covered]))
    blocks_before = EXPERT_COL_TILES * (jnp.cumsum(nblk) - nblk)
    ids = jnp.arange(N_EXPERTS)
    later = lax.cummin(jnp.where(nblk > 0, ids, N_EXPERTS), reverse=True)
    nxt = jnp.concatenate([later[1:], jnp.full((1,), N_EXPERTS)])
    nxt = jnp.where(nxt < N_EXPERTS, nxt, -1)
    order = i32(jnp.stack([blocks_before, nxt]))
    return (i32(start), i32(nblk), tail, order), i32(gap_rows), n_rows


def _combine_kernel(slot_ref, ys_ref, hx_ref, wgt_ref, gt_ref, o_ref, buf, sem):
    tm = hx_ref.shape[0]
    i = pl.program_id(0)
    n_i = pl.num_programs(0)

    def issue(tile, b):
        base = tile * (tm * TOP_K)

        def body(r, carry):
            for k in range(TOP_K):
                s = slot_ref[base + r * TOP_K + k]
                pltpu.make_async_copy(ys_ref.at[pl.ds(s, 1)], buf.at[b, k, pl.ds(r, 1)], sem.at[b]).start()
            return carry

        lax.fori_loop(0, tm, body, 0)

    @pl.when(i == 0)
    def _():
        issue(0, 0)

    @pl.when(i + 1 < n_i)
    def _():
        issue(i + 1, (i + 1) % 2)

    cur = i % 2

    def drain(r, carry):
        for k in range(TOP_K):
            pltpu.make_async_copy(ys_ref.at[pl.ds(0, 1)], buf.at[cur, k, pl.ds(0, 1)], sem.at[cur]).wait()
        return carry

    lax.fori_loop(0, tm, drain, 0)

    wgt = wgt_ref[...]
    acc = wgt[:, 0:1] * buf[cur, 0]
    for k in range(1, TOP_K):
        acc = acc + wgt[:, k:k + 1] * buf[cur, k]
    o_ref[...] = hx_ref[...] + gt_ref[0] * acc


def _combine(slots_flat, ys, hx, wgt, gt2, rows_per_mod, tm=128):
    n, d = hx.shape
    tiles_per_mod = rows_per_mod // tm
    return pl.pallas_call(
        _combine_kernel,
        grid_spec=pltpu.PrefetchScalarGridSpec(
            num_scalar_prefetch=1,
            grid=(n // tm,),
            in_specs=[pl.BlockSpec(memory_space=pl.ANY),
                      pl.BlockSpec((tm, d), lambda i, s: (i, 0)),
                      pl.BlockSpec((tm, LANES), lambda i, s: (i, 0)),
                      pl.BlockSpec((1, 1, d), lambda i, s: (i // tiles_per_mod, 0, 0))],
            out_specs=pl.BlockSpec((tm, d), lambda i, s: (i, 0)),
            scratch_shapes=[pltpu.VMEM((2, TOP_K, tm, d), jnp.float32),
                            pltpu.SemaphoreType.DMA((2,))]),
        out_shape=jax.ShapeDtypeStruct((n, d), jnp.float32),
        compiler_params=_cparams(("arbitrary",)),
        name="moe_combine",
    )(slots_flat, ys, hx, wgt, gt2)


def _rope_tables(n_t):
    rows = n_t // GRID_W
    row_ids = jnp.repeat(jnp.arange(rows), GRID_W).astype(jnp.float32)
    col_ids = jnp.tile(jnp.arange(GRID_W), rows).astype(jnp.float32)
    freqs = jnp.exp(-math.log(ROPE_THETA) * jnp.arange(ROPE_PAIRS_AXIS, dtype=jnp.float32) / ROPE_PAIRS_AXIS)
    ang = jnp.concatenate([row_ids[:, None] * freqs, col_ids[:, None] * freqs], axis=-1)
    cos_e = jnp.repeat(jnp.cos(ang), 2, axis=-1)
    sin = jnp.sin(ang)
    sin_s = jnp.stack([-sin, sin], axis=-1).reshape(n_t, HEAD_DIM)
    return cos_e, sin_s


def _gate_layouts(og, n_b, n_t):
    g = og[:, :4 * M_HEADS].reshape(n_b, n_t, 4, M_HEADS)
    return g.transpose(0, 3, 2, 1), g.transpose(0, 3, 1, 2)


def _layer(x, c, ctx, c_ctx, w_mod, b_mod, g_norm1, g_norm2, w_in, b_in, g_q, g_k, g_mh,
           w_br_m, w_br_a, w_out, w_router, b_router, w_gu, b_gu, w_dn, b_dn):
    n_b, n_t, d = x.shape
    n_c = ctx.shape[1]
    bf = jnp.bfloat16

    b_in2 = b_in.reshape(1, _F_IN)
    wr = jnp.pad(w_router, ((0, 0), (0, LANES - N_EXPERTS)))
    wr_hi = wr.astype(bf)
    wr_lo = (wr - wr_hi.astype(jnp.float32)).astype(bf)
    br = jnp.pad(b_router, (0, LANES - N_EXPERTS)).reshape(1, LANES)

    c8 = jnp.zeros((8, d), jnp.float32).at[:n_b].set(c).at[n_b].set(c_ctx)
    mod = _modulation(c8, w_mod, b_mod)
    mod6 = mod.reshape(8, 6, d)
    sh1x, sc1x, gt1x, sh2x, sc2x, gt2x = [mod6[:n_b, i].reshape(n_b, 1, d) for i in range(6)]
    sh1c, sc1c = [mod6[n_b:n_b + 1, i].reshape(1, 1, d) for i in range(2)]

    g1 = g_norm1.reshape(1, d)
    x2 = x.reshape(n_b * n_t, d)
    ux, ogx = _adaln_norm(x2, g1, sc1x, sh1x, w_in, b_in2, rows_per_mod=n_t)
    uc, ogc = _adaln_norm(ctx.reshape(n_b * n_c, d), g1, sc1c, sh1c, w_in, b_in2, rows_per_mod=n_b * n_c)
    px2 = _in_projection(ux, w_in, b_in2, jnp.arange(_P_COLS // PROJ_TN, dtype=jnp.int32))
    ctx_tiles = jnp.array([_P_MK // PROJ_TN, _P_MV // PROJ_TN, _P_MV // PROJ_TN + 1, _P_AK // PROJ_TN], jnp.int32)
    pc2 = _in_projection(uc, w_in, b_in2, ctx_tiles)
    px = px2.reshape(n_b, n_t, _P_COLS)
    pc = pc2.reshape(n_b, n_c, _C_COLS)

    cos_e, sin_s = _rope_tables(n_t)
    a_out = _attention(px, pc, cos_e, sin_s, g_q.reshape(1, HEAD_DIM), g_k.reshape(1, HEAD_DIM))

    g_rows, g_cols = _gate_layouts(ogx, n_b, n_t)
    g_rows_c, g_cols_c = _gate_layouts(ogc, n_b, n_c)
    h_fwd = _mlstm(px, pc, g_rows, g_cols, g_rows_c, g_cols_c, backward=False)
    m_out = _mlstm(px, pc, g_rows, g_cols, g_rows_c, g_cols_c, backward=True,
                   h_fwd=h_fwd, g_mh=g_mh.reshape(1, M_HEADS * MV_DIM))

    n = n_b * n_t
    z = _merge(m_out.reshape(n, d), a_out.reshape(n, d), px2, w_br_m.astype(bf), w_br_a.astype(bf))
    hx, u2, top_idx, top_w, rank, cnt = _outproj_router(
        z, x2, w_out.astype(bf), gt1x, g_norm2.reshape(1, d), sc2x, sh2x, wr_hi, wr_lo, br, rows_per_mod=n_t)

    counts = cnt[0, :N_EXPERTS].astype(jnp.int32)
    meta, gap_rows, n_rows = _row_layout(counts, n * TOP_K)
    slots = (meta[0][top_idx[:, :TOP_K]] + rank[:, :TOP_K]).astype(jnp.int32).reshape(n * TOP_K)

    xs = _dispatch(slots, gap_rows, meta[2], u2, n_rows)
    hid = _expert_up(meta, xs, w_gu, b_gu)
    ys = _expert_down(meta, hid, w_dn, b_dn)
    out = _combine(slots, ys, hx, top_w, gt2x, rows_per_mod=n_t)
    return out.reshape(n_b, n_t, d)


def kernel(x, c, ctx, c_ctx, w_mod, b_mod, g_norm1, g_norm2, w_in, b_in, g_q, g_k, g_mh, w_br_m, w_br_a, w_out,
           w_router, b_router, w_gu, b_gu, w_dn, b_dn):
    assert w_mod.shape[0] == 1, "single layer: the context stream has no consumer after it"
    return _layer(x, c, ctx, c_ctx, w_mod[0], b_mod[0], g_norm1[0], g_norm2[0], w_in[0], b_in[0], g_q[0], g_k[0],
                  g_mh[0], w_br_m[0], w_br_a[0], w_out[0], w_router[0], b_router[0], w_gu[0], b_gu[0],
                  w_dn[0], b_dn[0])
```

```python
import functools
import math

import jax
import jax.numpy as jnp
from jax import lax
from jax.experimental import pallas as pl
from jax.experimental.pallas import tpu as pltpu

D_MODEL = 2048
GRID_W = 64
HEAD_DIM = 128
A_HEADS = 16
KV_HEADS = 4
Q_GROUP = A_HEADS // KV_HEADS
ROPE_THETA = 10000.0
ROPE_PAIRS_AXIS = HEAD_DIM // 4
M_HEADS = 4
MV_DIM = D_MODEL // M_HEADS
MQK_DIM = MV_DIM // 2
N_EXPERTS = 32
TOP_K = 4
D_FF = D_MODEL
SWIGLU_LIMIT = 7.0
SWIGLU_ALPHA = 1.702
EPS = 1e-6

_O_MQ, _O_MK, _O_MV, _O_MO = 0, 1024, 2048, 4096
_O_GATES = 6144
GATE_COLS = 4 * M_HEADS
_O_AQ, _O_AK, _O_AV, _O_MG = 6160, 8208, 8720, 9232
_F_IN = 13328
_P_MQ, _P_MK, _P_MV, _P_MO, _P_AQ, _P_AK, _P_AV, _P_GM, _P_GA = (
    0, 1024, 2048, 4096, 6144, 8192, 8704, 9216, 11264)
_P_COLS = 13312
_C_MK, _C_MV, _C_AK, _C_AV = 0, 1024, 3072, 3584
_C_COLS = 4096

LANES = 128
M_CHUNK = 256
EXPERT_ROWS = 256
ROW_ALIGN = 16
ROW_DMA_PRIORITY = 1
EXPERT_COL_TILES = 2
NEG_BIG = -1e30
VMEM_LIMIT = 56 * 1024 * 1024

_HI = lax.Precision.HIGHEST


def _cparams(sem, vmem=VMEM_LIMIT):
    return pltpu.CompilerParams(dimension_semantics=sem, vmem_limit_bytes=vmem)


def _mod_kernel(c_ref, w_ref, b_ref, o_ref):
    c = c_ref[...]
    a = c * jax.nn.sigmoid(c)
    o_ref[...] = lax.dot_general(a, w_ref[...], (((1,), (0,)), ((), ())), precision=_HI,
                                 preferred_element_type=jnp.float32) + b_ref[...]


def _modulation(c8, w_mod, b_mod):
    d, n = w_mod.shape
    tn = 1024
    return pl.pallas_call(
        _mod_kernel,
        grid=(n // tn,),
        in_specs=[pl.BlockSpec((8, d), lambda j: (0, 0)),
                  pl.BlockSpec((d, tn), lambda j: (0, j)),
                  pl.BlockSpec((1, tn), lambda j: (0, j))],
        out_specs=pl.BlockSpec((8, tn), lambda j: (0, j)),
        out_shape=jax.ShapeDtypeStruct((8, n), jnp.float32),
        compiler_params=_cparams(("arbitrary",)),
        name="modulation",
    )(c8, w_mod, b_mod.reshape(1, n))


def _adaln_kernel(x_ref, g_ref, sc_ref, sh_ref, wg_ref, bg_ref, u_ref, og_ref):
    x = x_ref[...]
    y = x * lax.rsqrt(jnp.mean(x * x, axis=-1, keepdims=True) + EPS) * g_ref[...]
    u = y * (1.0 + sc_ref[0]) + sh_ref[0]
    u_ref[...] = u.astype(jnp.bfloat16)
    og_ref[...] = lax.dot_general(u, wg_ref[...], (((1,), (1,)), ((), ())), precision=_HI,
                                  preferred_element_type=jnp.float32) + bg_ref[...]


def _adaln_norm(x2, g, sc, sh, w_in_t, b_gate, rows_per_mod, tm=512):
    n, d = x2.shape
    tiles_per_mod = rows_per_mod // tm
    gate_blk = _O_GATES // GATE_COLS
    return pl.pallas_call(
        _adaln_kernel,
        grid=(n // tm,),
        in_specs=[pl.BlockSpec((tm, d), lambda i: (i, 0)),
                  pl.BlockSpec((1, d), lambda i: (0, 0)),
                  pl.BlockSpec((1, 1, d), lambda i: (i // tiles_per_mod, 0, 0)),
                  pl.BlockSpec((1, 1, d), lambda i: (i // tiles_per_mod, 0, 0)),
                  pl.BlockSpec((GATE_COLS, d), lambda i: (gate_blk, 0)),
                  pl.BlockSpec((1, GATE_COLS), lambda i: (0, 0))],
        out_specs=[pl.BlockSpec((tm, d), lambda i: (i, 0)),
                   pl.BlockSpec((tm, GATE_COLS), lambda i: (i, 0))],
        out_shape=[jax.ShapeDtypeStruct((n, d), jnp.bfloat16),
                   jax.ShapeDtypeStruct((n, GATE_COLS), jnp.float32)],
        compiler_params=_cparams(("arbitrary",)),
        name="adaln_norm",
    )(x2, g, sc, sh, w_in_t, b_gate)


PROJ_TN = 1024


def _proj_kernel(tiles_ref, u_ref, w_ref, wx_ref, b_ref, bx_ref, o_ref, w_s, b_s):
    t = tiles_ref[pl.program_id(0)]
    shifted = t * PROJ_TN >= _O_GATES
    keep = PROJ_TN - GATE_COLS

    @pl.when((pl.program_id(1) == 0) & jnp.logical_not(shifted))
    def _():
        w_s[...] = w_ref[...].astype(jnp.bfloat16)
        b_s[...] = b_ref[...]

    @pl.when((pl.program_id(1) == 0) & shifted)
    def _():
        w_s[0:keep, :] = w_ref[GATE_COLS:, :].astype(jnp.bfloat16)
        w_s[keep:, :] = wx_ref[...].astype(jnp.bfloat16)
        b_s[...] = jnp.concatenate([b_ref[:, GATE_COLS:], bx_ref[:, :GATE_COLS]], axis=1)

    acc = lax.dot_general(u_ref[...], w_s[...], (((1,), (1,)), ((), ())), preferred_element_type=jnp.float32)
    o_ref[...] = (acc + b_s[...]).astype(o_ref.dtype)


def _in_projection(u, w_in_t, b_in2, tiles, tm=1024):
    n, d = u.shape
    n_tiles = tiles.shape[0]
    tn = PROJ_TN
    return pl.pallas_call(
        _proj_kernel,
        grid_spec=pltpu.PrefetchScalarGridSpec(
            num_scalar_prefetch=1,
            grid=(n_tiles, n // tm),
            in_specs=[pl.BlockSpec((tm, d), lambda j, i, t: (i, 0)),
                      pl.BlockSpec((tn, d), lambda j, i, t: (t[j], 0)),
                      pl.BlockSpec((GATE_COLS, d), lambda j, i, t: ((tn // GATE_COLS) * (t[j] + 1), 0)),
                      pl.BlockSpec((1, tn), lambda j, i, t: (0, t[j])),
                      pl.BlockSpec((1, LANES), lambda j, i, t: (0, (tn // LANES) * (t[j] + 1)))],
            out_specs=pl.BlockSpec((tm, tn), lambda j, i, t: (i, j)),
            scratch_shapes=[pltpu.VMEM((tn, d), jnp.bfloat16), pltpu.VMEM((1, tn), jnp.float32)]),
        out_shape=jax.ShapeDtypeStruct((n, n_tiles * tn), jnp.bfloat16),
        compiler_params=_cparams(("arbitrary", "arbitrary")),
        name="in_projection",
    )(tiles, u, w_in_t, w_in_t, b_in2, b_in2)


def _rms_head(x, g):
    return x * lax.rsqrt(jnp.mean(x * x, axis=-1, keepdims=True) + EPS) * g


def _rope(x, cos_e, sin_s):
    lane = lax.broadcasted_iota(jnp.int32, x.shape, 1)
    swapped = jnp.where(lane % 2 == 0, pltpu.roll(x, LANES - 1, 1), pltpu.roll(x, 1, 1))
    return x * cos_e + swapped * sin_s


def _attn_kernel(q_ref, kx_ref, vx_ref, kc_ref, vc_ref, cosq_ref, sinq_ref, cosk_ref, sink_ref,
                 gq_ref, gk_ref, o_ref, k_s, v_s):
    n_t = kx_ref.shape[1]

    @pl.when(pl.program_id(2) == 0)
    def _():
        kx = _rms_head(kx_ref[0].astype(jnp.float32), gk_ref[...])
        k_s[0:n_t, :] = _rope(kx, cosk_ref[...], sink_ref[...]).astype(jnp.bfloat16)
        k_s[n_t:, :] = _rms_head(kc_ref[0].astype(jnp.float32), gk_ref[...]).astype(jnp.bfloat16)
        v_s[0:n_t, :] = vx_ref[0]
        v_s[n_t:, :] = vc_ref[0]

    scale = HEAD_DIM ** -0.5
    kk = k_s[...]
    vv = v_s[...]
    for g in range(Q_GROUP):
        q = q_ref[0, :, g * HEAD_DIM:(g + 1) * HEAD_DIM].astype(jnp.float32)
        q = _rope(_rms_head(q, gq_ref[...]), cosq_ref[...], sinq_ref[...]) * scale
        s = lax.dot_general(q.astype(jnp.bfloat16), kk, (((1,), (1,)), ((), ())),
                            preferred_element_type=jnp.float32)
        p = jnp.exp(s - jnp.max(s, axis=-1, keepdims=True))
        l = jnp.sum(p, axis=-1, keepdims=True)
        o = jnp.dot(p.astype(jnp.bfloat16), vv, preferred_element_type=jnp.float32) / l
        o_ref[0, :, g * HEAD_DIM:(g + 1) * HEAD_DIM] = o.astype(o_ref.dtype)


def _attention(px, pc, cos_e, sin_s, g_q, g_k, tq=256):
    n_b, n_t, _ = px.shape
    n_c = pc.shape[1]
    gw = Q_GROUP * HEAD_DIM
    return pl.pallas_call(
        _attn_kernel,
        grid=(n_b, KV_HEADS, n_t // tq),
        in_specs=[pl.BlockSpec((1, tq, gw), lambda b, h, i: (b, i, _P_AQ // gw + h)),
                  pl.BlockSpec((1, n_t, HEAD_DIM), lambda b, h, i: (b, 0, _P_AK // HEAD_DIM + h)),
                  pl.BlockSpec((1, n_t, HEAD_DIM), lambda b, h, i: (b, 0, _P_AV // HEAD_DIM + h)),
                  pl.BlockSpec((1, n_c, HEAD_DIM), lambda b, h, i: (b, 0, _C_AK // HEAD_DIM + h)),
                  pl.BlockSpec((1, n_c, HEAD_DIM), lambda b, h, i: (b, 0, _C_AV // HEAD_DIM + h)),
                  pl.BlockSpec((tq, HEAD_DIM), lambda b, h, i: (i, 0)),
                  pl.BlockSpec((tq, HEAD_DIM), lambda b, h, i: (i, 0)),
                  pl.BlockSpec((n_t, HEAD_DIM), lambda b, h, i: (0, 0)),
                  pl.BlockSpec((n_t, HEAD_DIM), lambda b, h, i: (0, 0)),
                  pl.BlockSpec((1, HEAD_DIM), lambda b, h, i: (0, 0)),
                  pl.BlockSpec((1, HEAD_DIM), lambda b, h, i: (0, 0))],
        out_specs=pl.BlockSpec((1, tq, gw), lambda b, h, i: (b, i, h)),
        out_shape=jax.ShapeDtypeStruct((n_b, n_t, A_HEADS * HEAD_DIM), jnp.bfloat16),
        scratch_shapes=[pltpu.VMEM((n_t + n_c, HEAD_DIM), jnp.bfloat16),
                        pltpu.VMEM((n_t + n_c, HEAD_DIM), jnp.bfloat16)],
        compiler_params=_cparams(("arbitrary", "arbitrary", "arbitrary")),
        name="attention",
    )(px, px, px, pc, pc, cos_e, sin_s, cos_e, sin_s, g_q, g_k)


def _log_sigmoid(x):
    return jnp.minimum(x, 0.0) - jnp.log(1.0 + jnp.exp(-jnp.abs(x)))


def _mlstm_chunk(q, k, v, i_row, f_row, i_col, f_col, ct_ref, n_ref, m_ref, backward, want_h):
    n_l = k.shape[0]
    lf_row = _log_sigmoid(f_row)
    lf_col = _log_sigmoid(f_col)
    t_idx = lax.broadcasted_iota(jnp.int32, (n_l, n_l), 0)
    s_idx = lax.broadcasted_iota(jnp.int32, (n_l, n_l), 1)
    seen = (s_idx >= t_idx) if backward else (s_idx <= t_idx)
    b_col = jnp.sum(jnp.where(seen, lf_row, 0.0), axis=1, keepdims=True)
    seen_t = (t_idx >= s_idx) if backward else (t_idx <= s_idx)
    b_row = jnp.sum(jnp.where(seen_t, lf_col, 0.0), axis=0, keepdims=True)
    b_end = jnp.sum(lf_row, axis=1, keepdims=True)
    m_prev = m_ref[...]
    ct = ct_ref[...]
    n_vec = n_ref[...]
    kf = k.astype(jnp.float32)

    h = None
    if want_h:
        d_intra = jnp.where(seen, b_col - b_row + i_row, NEG_BIG)
        d_inter = b_col + m_prev
        m_t = jnp.maximum(d_inter, jnp.max(d_intra, axis=1, keepdims=True))
        qk = lax.dot_general(q, k, (((1,), (1,)), ((), ())), preferred_element_type=jnp.float32)
        s = qk * jnp.exp(d_intra - m_t)
        w_inter = jnp.exp(d_inter - m_t)
        num = (jnp.dot(s.astype(jnp.bfloat16), v, preferred_element_type=jnp.float32)
               + w_inter * jnp.dot(q, ct.astype(jnp.bfloat16), preferred_element_type=jnp.float32))
        den = (jnp.sum(s, axis=1, keepdims=True)
               + w_inter * jnp.sum(q.astype(jnp.float32) * n_vec, axis=1, keepdims=True))
        h = num / jnp.maximum(jnp.abs(den), jnp.exp(-m_t))

    g_col = b_end - b_col + i_col
    m_new = jnp.maximum(b_end + m_prev, jnp.max(g_col, axis=0, keepdims=True))
    w_s = jnp.exp(g_col - m_new)
    w_c = jnp.exp(b_end + m_prev - m_new)
    wv = (w_s * v.astype(jnp.float32)).astype(jnp.bfloat16)
    ct_ref[...] = w_c * ct + lax.dot_general(k, wv, (((0,), (0,)), ((), ())),
                                             preferred_element_type=jnp.float32)
    n_ref[...] = w_c * n_vec + jnp.sum(w_s * kf, axis=0, keepdims=True)
    m_ref[...] = m_new
    return h


def _mlstm_kernel(backward, *refs):
    if backward:
        (q_ref, k_ref, v_ref, kc_ref, vc_ref, gr_ref, gc_ref, grc_ref, gcc_ref,
         hf_ref, op_ref, gmh_ref, o_ref, ct_ref, n_ref, m_ref) = refs
    else:
        (q_ref, k_ref, v_ref, kc_ref, vc_ref, gr_ref, gc_ref, grc_ref, gcc_ref,
         o_ref, ct_ref, n_ref, m_ref) = refs
    gi, gf = (2, 3) if backward else (0, 1)
    step = pl.program_id(2)

    @pl.when(step == 0)
    def _():
        ct_ref[...] = jnp.zeros_like(ct_ref)
        n_ref[...] = jnp.zeros_like(n_ref)
        m_ref[...] = jnp.zeros_like(m_ref)
        _mlstm_chunk(None, kc_ref[0], vc_ref[0],
                     grc_ref[0, 0, gi:gi + 1, :], grc_ref[0, 0, gf:gf + 1, :],
                     gcc_ref[0, 0, :, gi:gi + 1], gcc_ref[0, 0, :, gf:gf + 1],
                     ct_ref, n_ref, m_ref, backward, False)

    @pl.when(step > 0)
    def _():
        q = (q_ref[0].astype(jnp.float32) * (MQK_DIM ** -0.5)).astype(jnp.bfloat16)
        h = _mlstm_chunk(q, k_ref[0], v_ref[0],
                         gr_ref[0, 0, gi:gi + 1, :], gr_ref[0, 0, gf:gf + 1, :],
                         gc_ref[0, 0, :, gi:gi + 1], gc_ref[0, 0, :, gf:gf + 1],
                         ct_ref, n_ref, m_ref, backward, True)
        if backward:
            ht = h + hf_ref[0]
            y = ht * lax.rsqrt(jnp.mean(ht * ht, axis=-1, keepdims=True) + EPS) * gmh_ref[...]
            o_ref[0] = (y * jax.nn.sigmoid(op_ref[0].astype(jnp.float32))).astype(o_ref.dtype)
        else:
            o_ref[0] = h


def _mlstm(px, pc, g_rows, g_cols, g_rows_c, g_cols_c, backward, h_fwd=None, g_mh=None):
    n_b, n_t, _ = px.shape
    n_c = pc.shape[1]
    n_l = M_CHUNK
    assert n_c == n_l and n_t % n_l == 0
    n_chunk = n_t // n_l

    if backward:
        def cidx(s):
            return jnp.minimum(n_chunk - s, n_chunk - 1)
    else:
        def cidx(s):
            return jnp.maximum(s - 1, 0)

    in_specs = [
        pl.BlockSpec((1, n_l, MQK_DIM), lambda b, h, s: (b, cidx(s), _P_MQ // MQK_DIM + h)),
        pl.BlockSpec((1, n_l, MQK_DIM), lambda b, h, s: (b, cidx(s), _P_MK // MQK_DIM + h)),
        pl.BlockSpec((1, n_l, MV_DIM), lambda b, h, s: (b, cidx(s), _P_MV // MV_DIM + h)),
        pl.BlockSpec((1, n_c, MQK_DIM), lambda b, h, s: (b, 0, _C_MK // MQK_DIM + h)),
        pl.BlockSpec((1, n_c, MV_DIM), lambda b, h, s: (b, 0, _C_MV // MV_DIM + h)),
        pl.BlockSpec((1, 1, 4, n_l), lambda b, h, s: (b, h, 0, cidx(s))),
        pl.BlockSpec((1, 1, n_l, 4), lambda b, h, s: (b, h, cidx(s), 0)),
        pl.BlockSpec((1, 1, 4, n_c), lambda b, h, s: (b, h, 0, 0)),
        pl.BlockSpec((1, 1, n_c, 4), lambda b, h, s: (b, h, 0, 0)),
    ]
    args = [px, px, px, pc, pc, g_rows, g_cols, g_rows_c, g_cols_c]
    if backward:
        in_specs += [
            pl.BlockSpec((1, n_l, MV_DIM), lambda b, h, s: (b, cidx(s), h)),
            pl.BlockSpec((1, n_l, MV_DIM), lambda b, h, s: (b, cidx(s), _P_MO // MV_DIM + h)),
            pl.BlockSpec((1, MV_DIM), lambda b, h, s: (0, h)),
        ]
        args += [h_fwd, px, g_mh]
        out_dtype = jnp.bfloat16
    else:
        out_dtype = jnp.float32
    return pl.pallas_call(
        functools.partial(_mlstm_kernel, backward),
        grid=(n_b, M_HEADS, n_chunk + 1),
        in_specs=in_specs,
        out_specs=pl.BlockSpec((1, n_l, MV_DIM), lambda b, h, s: (b, cidx(s), h)),
        out_shape=jax.ShapeDtypeStruct((n_b, n_t, M_HEADS * MV_DIM), out_dtype),
        scratch_shapes=[pltpu.VMEM((MQK_DIM, MV_DIM), jnp.float32),
                        pltpu.VMEM((1, MQK_DIM), jnp.float32),
                        pltpu.VMEM((1, 1), jnp.float32)],
        compiler_params=_cparams(("arbitrary", "arbitrary", "arbitrary")),
        name="mlstm_bwd" if backward else "mlstm_fwd",
    )(*args)


def _merge_kernel(m_ref, a_ref, gm_ref, ga_ref, wm_ref, wa_ref, o_ref):
    zm = jnp.dot(m_ref[...], wm_ref[...], preferred_element_type=jnp.float32)
    za = jnp.dot(a_ref[...], wa_ref[...], preferred_element_type=jnp.float32)
    z = (jax.nn.sigmoid(gm_ref[...].astype(jnp.float32)) * zm
         + jax.nn.sigmoid(ga_ref[...].astype(jnp.float32)) * za)
    o_ref[...] = z.astype(o_ref.dtype)


def _merge(m_out, a_out, px2, w_br_m, w_br_a, tm=512, tn=512):
    n, d = m_out.shape
    return pl.pallas_call(
        _merge_kernel,
        grid=(n // tm, d // tn),
        in_specs=[pl.BlockSpec((tm, d), lambda i, j: (i, 0)),
                  pl.BlockSpec((tm, d), lambda i, j: (i, 0)),
                  pl.BlockSpec((tm, tn), lambda i, j: (i, _P_GM // tn + j)),
                  pl.BlockSpec((tm, tn), lambda i, j: (i, _P_GA // tn + j)),
                  pl.BlockSpec((d, tn), lambda i, j: (0, j)),
                  pl.BlockSpec((d, tn), lambda i, j: (0, j))],
        out_specs=pl.BlockSpec((tm, tn), lambda i, j: (i, j)),
        out_shape=jax.ShapeDtypeStruct((n, d), jnp.bfloat16),
        compiler_params=_cparams(("arbitrary", "arbitrary")),
        name="merge",
    )(m_out, a_out, px2, px2, w_br_m, w_br_a)


def _split_bf16(x):
    hi = x.astype(jnp.bfloat16)
    lo = (x - hi.astype(jnp.float32)).astype(jnp.bfloat16)
    return hi, lo


def _outproj_router_kernel(z_ref, x_ref, wo_ref, gt_ref, g2_ref, sc_ref, sh_ref, wrh_ref, wrl_ref, br_ref,
                           hx_ref, u2_ref, idx_ref, wgt_ref, rank_ref, cnt_ref, carry_ref):
    tm = z_ref.shape[0]

    @pl.when(pl.program_id(0) == 0)
    def _():
        carry_ref[...] = jnp.zeros_like(carry_ref)

    y = jnp.dot(z_ref[...], wo_ref[...], preferred_element_type=jnp.float32)
    hx = x_ref[...] + gt_ref[0] * y
    hx_ref[...] = hx
    u2 = (hx * lax.rsqrt(jnp.mean(hx * hx, axis=-1, keepdims=True) + EPS) * g2_ref[...]
          * (1.0 + sc_ref[0]) + sh_ref[0])
    u2_ref[...] = u2

    u_hi, u_lo = _split_bf16(u2)
    logits = (jnp.dot(u_hi, wrh_ref[...], preferred_element_type=jnp.float32)
              + jnp.dot(u_lo, wrh_ref[...], preferred_element_type=jnp.float32)
              + jnp.dot(u_hi, wrl_ref[...], preferred_element_type=jnp.float32)) + br_ref[...]
    lane = lax.broadcasted_iota(jnp.int32, (tm, LANES), 1)
    logits = jnp.where(lane < N_EXPERTS, logits, NEG_BIG)

    idx_out = jnp.zeros((tm, LANES), jnp.int32)
    val_out = jnp.zeros((tm, LANES), jnp.float32)
    chosen = jnp.zeros((tm, LANES), jnp.float32)
    sel = []
    top0 = None
    for k in range(TOP_K):
        mx = jnp.max(logits, axis=-1, keepdims=True)
        ix = jnp.min(jnp.where(logits == mx, lane, LANES), axis=-1, keepdims=True)
        hit = lane == ix
        if k == 0:
            top0 = mx
        idx_out = jnp.where(lane == k, ix, idx_out)
        val_out = jnp.where(lane == k, jnp.exp(mx - top0), val_out)
        chosen = jnp.where(hit, 1.0, chosen)
        sel.append(hit)
        logits = jnp.where(hit, NEG_BIG, logits)
    idx_ref[...] = idx_out
    wgt_ref[...] = val_out / jnp.sum(val_out, axis=-1, keepdims=True)

    r_idx = lax.broadcasted_iota(jnp.int32, (tm, tm), 0)
    c_idx = lax.broadcasted_iota(jnp.int32, (tm, tm), 1)
    before = jnp.where(c_idx < r_idx, 1.0, 0.0).astype(jnp.bfloat16)
    prior = jnp.dot(before, chosen.astype(jnp.bfloat16), preferred_element_type=jnp.float32) + carry_ref[...]
    rank_out = jnp.zeros((tm, LANES), jnp.int32)
    for k in range(TOP_K):
        rk = jnp.sum(jnp.where(sel[k], prior, 0.0), axis=-1, keepdims=True)
        rank_out = jnp.where(lane == k, rk.astype(jnp.int32), rank_out)
    rank_ref[...] = rank_out
    carry_ref[...] = carry_ref[...] + jnp.sum(chosen, axis=0, keepdims=True)
    cnt_ref[...] = jnp.broadcast_to(carry_ref[...], cnt_ref.shape)


def _outproj_router(z, x2, w_out, gt1, g2, sc2, sh2, wr_hi, wr_lo, br, rows_per_mod, tm=256):
    n, d = z.shape
    tiles_per_mod = rows_per_mod // tm
    row = lambda i: (i, 0)
    fixed = lambda i: (0, 0)
    modi = lambda i: (i // tiles_per_mod, 0, 0)
    return pl.pallas_call(
        _outproj_router_kernel,
        grid=(n // tm,),
        in_specs=[pl.BlockSpec((tm, d), row),
                  pl.BlockSpec((tm, d), row),
                  pl.BlockSpec((d, d), fixed),
                  pl.BlockSpec((1, 1, d), modi),
                  pl.BlockSpec((1, d), fixed),
                  pl.BlockSpec((1, 1, d), modi),
                  pl.BlockSpec((1, 1, d), modi),
                  pl.BlockSpec((d, LANES), fixed),
                  pl.BlockSpec((d, LANES), fixed),
                  pl.BlockSpec((1, LANES), fixed)],
        out_specs=[pl.BlockSpec((tm, d), row),
                   pl.BlockSpec((tm, d), row),
                   pl.BlockSpec((tm, LANES), row),
                   pl.BlockSpec((tm, LANES), row),
                   pl.BlockSpec((tm, LANES), row),
                   pl.BlockSpec((8, LANES), fixed)],
        out_shape=[jax.ShapeDtypeStruct((n, d), jnp.float32),
                   jax.ShapeDtypeStruct((n, d), jnp.float32),
                   jax.ShapeDtypeStruct((n, LANES), jnp.int32),
                   jax.ShapeDtypeStruct((n, LANES), jnp.float32),
                   jax.ShapeDtypeStruct((n, LANES), jnp.int32),
                   jax.ShapeDtypeStruct((8, LANES), jnp.float32)],
        scratch_shapes=[pltpu.VMEM((1, LANES), jnp.float32)],
        compiler_params=_cparams(("arbitrary",)),
        name="outproj_router",
    )(z, x2, w_out, gt1, g2, sc2, sh2, wr_hi, wr_lo, br)


def _start_zero_rows(zbuf, dst_rows, start, count, sem, wait):
    n_full = count // EXPERT_ROWS
    rem = count % EXPERT_ROWS
    pieces = [(i < n_full, start + i * EXPERT_ROWS, EXPERT_ROWS) for i in range(3)]
    off = start + n_full * EXPERT_ROWS
    p = EXPERT_ROWS // 2
    while p >= ROW_ALIGN:
        pieces.append(((rem & p) != 0, off + (rem // (2 * p)) * (2 * p), p))
        p //= 2
    for cond, row, size in pieces:
        @pl.when(cond)
        def _(row=row, size=size):
            cp = pltpu.make_async_copy(zbuf.at[pl.ds(0, size)], dst_rows(pl.multiple_of(row, ROW_ALIGN), size), sem)
            cp.wait() if wait else cp.start()


def _zero_rows(zbuf, dst_rows, start, count, sem):
    _start_zero_rows(zbuf, dst_rows, start, count, sem, wait=False)
    _start_zero_rows(zbuf, dst_rows, start, count, sem, wait=True)


def _dispatch_kernel(slot_ref, gap_ref, tail_ref, u_ref, xs_ref, zbuf, sem, zsem):
    tm = u_ref.shape[0]
    n_rows = xs_ref.shape[0]
    base = pl.program_id(0) * (tm * TOP_K)

    @pl.when(pl.program_id(0) == 0)
    def _():
        zbuf[...] = jnp.zeros_like(zbuf)
        for wait in (False, True):
            for e in range(N_EXPERTS):
                @pl.when(gap_ref[e] >= 0)
                def _(e=e, wait=wait):
                    cp = pltpu.make_async_copy(
                        zbuf.at[pl.ds(0, ROW_ALIGN)],
                        xs_ref.at[pl.ds(pl.multiple_of(gap_ref[e], ROW_ALIGN), ROW_ALIGN)], zsem)
                    cp.wait() if wait else cp.start()
        _zero_rows(zbuf, lambda r, s: xs_ref.at[pl.ds(r, s)], tail_ref[0], n_rows - tail_ref[0], zsem)

    def issue(r, carry):
        for k in range(TOP_K):
            s = slot_ref[base + r * TOP_K + k]
            pltpu.make_async_copy(u_ref.at[pl.ds(r, 1)], xs_ref.at[pl.ds(s, 1)], sem).start(priority=k % 2)
        return carry

    lax.fori_loop(0, tm, issue, 0)

    def drain(r, carry):
        for k in range(TOP_K):
            pltpu.make_async_copy(u_ref.at[pl.ds(0, 1)], xs_ref.at[pl.ds(0, 1)], sem).wait()
        return carry

    lax.fori_loop(0, tm, drain, 0)


def _dispatch(slots_flat, gap_rows, tail, u2, n_rows, tm=256):
    n, d = u2.shape
    return pl.pallas_call(
        _dispatch_kernel,
        grid_spec=pltpu.PrefetchScalarGridSpec(
            num_scalar_prefetch=3,
            grid=(n // tm,),
            in_specs=[pl.BlockSpec((tm, d), lambda i, *_: (i, 0))],
            out_specs=pl.BlockSpec(memory_space=pl.ANY),
            scratch_shapes=[pltpu.VMEM((EXPERT_ROWS, d), jnp.float32),
                            pltpu.SemaphoreType.DMA(()), pltpu.SemaphoreType.DMA(())]),
        out_shape=jax.ShapeDtypeStruct((n_rows, d), jnp.float32),
        compiler_params=_cparams(("arbitrary",)),
        name="moe_dispatch",
    )(slots_flat, gap_rows, tail, u2)


def _expert_rows_loop(start_ref, nblk_ref, tail_ref, order_ref, src_ref, dst_ref, ibuf, obuf, zbuf,
                      sem_in, sem_out, sem_z, prepare, compute):
    e = pl.program_id(0)
    f = pl.program_id(1)
    n_f = pl.num_programs(1)
    tile = obuf.shape[2]
    col = pl.multiple_of(f * tile, tile)
    nb = nblk_ref[e]
    base = start_ref[e]
    g0 = order_ref[0, e] + f * nb
    nxt = order_ref[1, e]

    def rows(first_row):
        return pl.ds(pl.multiple_of(first_row, ROW_ALIGN), EXPERT_ROWS)

    def in_copy(first_row, slot):
        return pltpu.make_async_copy(src_ref.at[rows(first_row)], ibuf.at[slot], sem_in.at[slot])

    def out_copy(k, slot):
        return pltpu.make_async_copy(obuf.at[slot], dst_ref.at[rows(base + k * EXPERT_ROWS), pl.ds(col, tile)],
                                     sem_out.at[slot])

    @pl.when((nb > 0) & (g0 == 0))
    def _():
        in_copy(base, 0).start(priority=ROW_DMA_PRIORITY)

    prepare()

    def body(k, carry):
        slot = lax.rem(g0 + k, 2)
        oslot = lax.rem(k, 2)

        @pl.when(k + 1 < nb)
        def _():
            in_copy(base + (k + 1) * EXPERT_ROWS, 1 - slot).start(priority=ROW_DMA_PRIORITY)

        @pl.when((k + 1 == nb) & (f + 1 < n_f))
        def _():
            in_copy(base, 1 - slot).start(priority=ROW_DMA_PRIORITY)

        @pl.when((k + 1 == nb) & (f + 1 == n_f) & (nxt >= 0))
        def _():
            in_copy(start_ref[jnp.maximum(nxt, 0)], 1 - slot).start(priority=ROW_DMA_PRIORITY)

        in_copy(base, slot).wait()

        @pl.when(k >= 2)
        def _():
            out_copy(k - 2, oslot).wait()

        obuf[oslot] = compute(ibuf[slot]).astype(obuf.dtype)
        out_copy(k, oslot).start()
        return carry

    lax.fori_loop(0, nb, body, 0)

    @pl.when(nb >= 2)
    def _():
        out_copy(nb - 2, lax.rem(nb, 2)).wait()

    @pl.when(nb >= 1)
    def _():
        out_copy(nb - 1, lax.rem(nb + 1, 2)).wait()

    @pl.when(e == pl.num_programs(0) - 1)
    def _():
        zbuf[...] = jnp.zeros_like(zbuf)
        _zero_rows(zbuf, lambda r, s: dst_ref.at[pl.ds(r, s), pl.ds(col, tile)],
                   tail_ref[1], dst_ref.shape[0] - tail_ref[1], sem_z)


def _expert_up_kernel(start_ref, nblk_ref, tail_ref, order_ref, xs_ref, wg_ref, wu_ref, bg_ref, bu_ref, hid_ref,
                      wg_s, wu_s, ibuf, obuf, zbuf, sem_in, sem_out, sem_z):
    def prepare():
        wg_s[...] = wg_ref[...].astype(jnp.bfloat16)
        wu_s[...] = wu_ref[...].astype(jnp.bfloat16)

    def compute(xb):
        x = xb.astype(jnp.bfloat16)
        gate = jnp.dot(x, wg_s[...], preferred_element_type=jnp.float32) + bg_ref[...]
        up = jnp.dot(x, wu_s[...], preferred_element_type=jnp.float32) + bu_ref[...]
        gate = jnp.minimum(gate, SWIGLU_LIMIT)
        up = jnp.clip(up, -SWIGLU_LIMIT, SWIGLU_LIMIT)
        return (up + 1.0) * gate * jax.nn.sigmoid(SWIGLU_ALPHA * gate)

    _expert_rows_loop(start_ref, nblk_ref, tail_ref, order_ref, xs_ref, hid_ref, ibuf, obuf, zbuf,
                      sem_in, sem_out, sem_z, prepare, compute)


def _expert_up(meta, xs, w_gu, b_gu):
    n_rows, d = xs.shape
    tf = D_FF // EXPERT_COL_TILES
    nfc = EXPERT_COL_TILES
    b3 = b_gu.reshape(N_EXPERTS, 1, 2 * D_FF)
    return pl.pallas_call(
        _expert_up_kernel,
        grid_spec=pltpu.PrefetchScalarGridSpec(
            num_scalar_prefetch=4,
            grid=(N_EXPERTS, nfc),
            in_specs=[pl.BlockSpec(memory_space=pl.ANY),
                      pl.BlockSpec((None, d, tf), lambda e, f, *_: (e, 0, f)),
                      pl.BlockSpec((None, d, tf), lambda e, f, *_: (e, 0, nfc + f)),
                      pl.BlockSpec((None, 1, tf), lambda e, f, *_: (e, 0, f)),
                      pl.BlockSpec((None, 1, tf), lambda e, f, *_: (e, 0, nfc + f))],
            out_specs=pl.BlockSpec(memory_space=pl.ANY),
            scratch_shapes=[pltpu.VMEM((d, tf), jnp.bfloat16), pltpu.VMEM((d, tf), jnp.bfloat16),
                            pltpu.VMEM((2, EXPERT_ROWS, d), jnp.float32),
                            pltpu.VMEM((2, EXPERT_ROWS, tf), jnp.bfloat16),
                            pltpu.VMEM((EXPERT_ROWS, tf), jnp.bfloat16),
                            pltpu.SemaphoreType.DMA((2,)), pltpu.SemaphoreType.DMA((2,)),
                            pltpu.SemaphoreType.DMA(())]),
        out_shape=jax.ShapeDtypeStruct((n_rows, D_FF), jnp.bfloat16),
        compiler_params=_cparams(("arbitrary", "arbitrary")),
        name="expert_up",
    )(*meta, xs, w_gu, w_gu, b3, b3)


def _expert_down_kernel(start_ref, nblk_ref, tail_ref, order_ref, hid_ref, wd_ref, bd_ref, ys_ref,
                        wd_s, ibuf, obuf, zbuf, sem_in, sem_out, sem_z):
    def prepare():
        wd_s[...] = wd_ref[...].astype(jnp.bfloat16)

    def compute(hb):
        return jnp.dot(hb, wd_s[...], preferred_element_type=jnp.float32) + bd_ref[...]

    _expert_rows_loop(start_ref, nblk_ref, tail_ref, order_ref, hid_ref, ys_ref, ibuf, obuf, zbuf,
                      sem_in, sem_out, sem_z, prepare, compute)


def _expert_down(meta, hid, w_dn, b_dn):
    n_rows, dff = hid.shape
    d = w_dn.shape[2]
    tn = d // EXPERT_COL_TILES
    b3 = b_dn.reshape(N_EXPERTS, 1, d)
    return pl.pallas_call(
        _expert_down_kernel,
        grid_spec=pltpu.PrefetchScalarGridSpec(
            num_scalar_prefetch=4,
            grid=(N_EXPERTS, d // tn),
            in_specs=[pl.BlockSpec(memory_space=pl.ANY),
                      pl.BlockSpec((None, dff, tn), lambda e, f, *_: (e, 0, f)),
                      pl.BlockSpec((None, 1, tn), lambda e, f, *_: (e, 0, f))],
            out_specs=pl.BlockSpec(memory_space=pl.ANY),
            scratch_shapes=[pltpu.VMEM((dff, tn), jnp.bfloat16),
                            pltpu.VMEM((2, EXPERT_ROWS, dff), jnp.bfloat16),
                            pltpu.VMEM((2, EXPERT_ROWS, tn), jnp.float32),
                            pltpu.VMEM((EXPERT_ROWS, tn), jnp.float32),
                            pltpu.SemaphoreType.DMA((2,)), pltpu.SemaphoreType.DMA((2,)),
                            pltpu.SemaphoreType.DMA(())]),
        out_shape=jax.ShapeDtypeStruct((n_rows, d), jnp.float32),
        compiler_params=_cparams(("arbitrary", "arbitrary")),
        name="expert_down",
    )(*meta, hid, w_dn, b3)


def _row_layout(counts, n_assign):
    c_al = (counts + ROW_ALIGN - 1) // ROW_ALIGN * ROW_ALIGN
    start = jnp.cumsum(c_al) - c_al
    total = jnp.sum(c_al)
    nblk = (counts + EXPERT_ROWS - 1) // EXPERT_ROWS
    covered = jnp.max(start + nblk * EXPERT_ROWS)
    gap_rows = jnp.where(counts > 0, start + c_al - ROW_ALIGN, -1)
    n_rows = n_assign + N_EXPERTS * ROW_ALIGN + EXPERT_ROWS
    i32 = lambda a: a.astype(jnp.int32)
    tail = i32(jnp.stack([total, covered]))
    blocks_before = EXPERT_COL_TILES * (jnp.cumsum(nblk) - nblk)
    ids = jnp.arange(N_EXPERTS)
    later = lax.cummin(jnp.where(nblk > 0, ids, N_EXPERTS), reverse=True)
    nxt = jnp.concatenate([later[1:], jnp.full((1,), N_EXPERTS)])
    nxt = jnp.where(nxt < N_EXPERTS, nxt, -1)
    order = i32(jnp.stack([blocks_before, nxt]))
    return (i32(start), i32(nblk), tail, order), i32(gap_rows), n_rows


def _combine_kernel(slot_ref, ys_ref, hx_ref, wgt_ref, gt_ref, o_ref, buf, sem):
    tm = hx_ref.shape[0]
    i = pl.program_id(0)
    n_i = pl.num_programs(0)

    def issue(tile, b):
        base = tile * (tm * TOP_K)

        def body(r, carry):
            for k in range(TOP_K):
                s = slot_ref[base + r * TOP_K + k]
                pltpu.make_async_copy(ys_ref.at[pl.ds(s, 1)], buf.at[b, k, pl.ds(r, 1)],
                                      sem.at[b]).start(priority=k % 2)
            return carry

        lax.fori_loop(0, tm, body, 0)

    @pl.when(i == 0)
    def _():
        issue(0, 0)

    @pl.when(i + 1 < n_i)
    def _():
        issue(i + 1, (i + 1) % 2)

    cur = i % 2

    def drain(r, carry):
        for k in range(TOP_K):
            pltpu.make_async_copy(ys_ref.at[pl.ds(0, 1)], buf.at[cur, k, pl.ds(0, 1)], sem.at[cur]).wait()
        return carry

    lax.fori_loop(0, tm, drain, 0)

    wgt = wgt_ref[...]
    acc = wgt[:, 0:1] * buf[cur, 0]
    for k in range(1, TOP_K):
        acc = acc + wgt[:, k:k + 1] * buf[cur, k]
    o_ref[...] = hx_ref[...] + gt_ref[0] * acc


def _combine(slots_flat, ys, hx, wgt, gt2, rows_per_mod, tm=128):
    n, d = hx.shape
    tiles_per_mod = rows_per_mod // tm
    return pl.pallas_call(
        _combine_kernel,
        grid_spec=pltpu.PrefetchScalarGridSpec(
            num_scalar_prefetch=1,
            grid=(n // tm,),
            in_specs=[pl.BlockSpec(memory_space=pl.ANY),
                      pl.BlockSpec((tm, d), lambda i, s: (i, 0)),
                      pl.BlockSpec((tm, LANES), lambda i, s: (i, 0)),
                      pl.BlockSpec((1, 1, d), lambda i, s: (i // tiles_per_mod, 0, 0))],
            out_specs=pl.BlockSpec((tm, d), lambda i, s: (i, 0)),
            scratch_shapes=[pltpu.VMEM((2, TOP_K, tm, d), jnp.float32),
                            pltpu.SemaphoreType.DMA((2,))]),
        out_shape=jax.ShapeDtypeStruct((n, d), jnp.float32),
        compiler_params=_cparams(("arbitrary",)),
        name="moe_combine",
    )(slots_flat, ys, hx, wgt, gt2)


def _rope_tables(n_t):
    rows = n_t // GRID_W
    row_ids = jnp.repeat(jnp.arange(rows), GRID_W).astype(jnp.float32)
    col_ids = jnp.tile(jnp.arange(GRID_W), rows).astype(jnp.float32)
    freqs = jnp.exp(-math.log(ROPE_THETA) * jnp.arange(ROPE_PAIRS_AXIS, dtype=jnp.float32) / ROPE_PAIRS_AXIS)
    ang = jnp.concatenate([row_ids[:, None] * freqs, col_ids[:, None] * freqs], axis=-1)
    cos_e = jnp.repeat(jnp.cos(ang), 2, axis=-1)
    sin = jnp.sin(ang)
    sin_s = jnp.stack([-sin, sin], axis=-1).reshape(n_t, HEAD_DIM)
    return cos_e, sin_s


def _gate_layouts(og, n_b, n_t):
    g = og[:, :4 * M_HEADS].reshape(n_b, n_t, 4, M_HEADS)
    return g.transpose(0, 3, 2, 1), g.transpose(0, 3, 1, 2)


def _layer(x, c, ctx, c_ctx, w_mod, b_mod, g_norm1, g_norm2, w_in, b_in, g_q, g_k, g_mh,
           w_br_m, w_br_a, w_out, w_router, b_router, w_gu, b_gu, w_dn, b_dn):
    n_b, n_t, d = x.shape
    n_c = ctx.shape[1]
    bf = jnp.bfloat16

    b_in2 = b_in.reshape(1, _F_IN)
    wr = jnp.pad(w_router, ((0, 0), (0, LANES - N_EXPERTS)))
    wr_hi = wr.astype(bf)
    wr_lo = (wr - wr_hi.astype(jnp.float32)).astype(bf)
    br = jnp.pad(b_router, (0, LANES - N_EXPERTS)).reshape(1, LANES)

    c8 = jnp.zeros((8, d), jnp.float32).at[:n_b].set(c).at[n_b].set(c_ctx)
    mod = _modulation(c8, w_mod, b_mod)
    mod6 = mod.reshape(8, 6, d)
    sh1x, sc1x, gt1x, sh2x, sc2x, gt2x = [mod6[:n_b, i].reshape(n_b, 1, d) for i in range(6)]
    sh1c, sc1c = [mod6[n_b:n_b + 1, i].reshape(1, 1, d) for i in range(2)]

    g1 = g_norm1.reshape(1, d)
    x2 = x.reshape(n_b * n_t, d)
    w_in_t = w_in.T
    b_gate = b_in[_O_GATES:_O_AQ].reshape(1, GATE_COLS)
    ux, ogx = _adaln_norm(x2, g1, sc1x, sh1x, w_in_t, b_gate, rows_per_mod=n_t)
    uc, ogc = _adaln_norm(ctx.reshape(n_b * n_c, d), g1, sc1c, sh1c, w_in_t, b_gate, rows_per_mod=n_b * n_c)
    px2 = _in_projection(ux, w_in_t, b_in2, jnp.arange(_P_COLS // PROJ_TN, dtype=jnp.int32))
    ctx_tiles = jnp.array([_P_MK // PROJ_TN, _P_MV // PROJ_TN, _P_MV // PROJ_TN + 1, _P_AK // PROJ_TN], jnp.int32)
    pc2 = _in_projection(uc, w_in_t, b_in2, ctx_tiles)
    px = px2.reshape(n_b, n_t, _P_COLS)
    pc = pc2.reshape(n_b, n_c, _C_COLS)

    cos_e, sin_s = _rope_tables(n_t)
    a_out = _attention(px, pc, cos_e, sin_s, g_q.reshape(1, HEAD_DIM), g_k.reshape(1, HEAD_DIM))

    g_rows, g_cols = _gate_layouts(ogx, n_b, n_t)
    g_rows_c, g_cols_c = _gate_layouts(ogc, n_b, n_c)
    h_fwd = _mlstm(px, pc, g_rows, g_cols, g_rows_c, g_cols_c, backward=False)
    m_out = _mlstm(px, pc, g_rows, g_cols, g_rows_c, g_cols_c, backward=True,
                   h_fwd=h_fwd, g_mh=g_mh.reshape(1, M_HEADS * MV_DIM))

    n = n_b * n_t
    z = _merge(m_out.reshape(n, d), a_out.reshape(n, d), px2, w_br_m.astype(bf), w_br_a.astype(bf))
    hx, u2, top_idx, top_w, rank, cnt = _outproj_router(
        z, x2, w_out.astype(bf), gt1x, g_norm2.reshape(1, d), sc2x, sh2x, wr_hi, wr_lo, br, rows_per_mod=n_t)

    counts = cnt[0, :N_EXPERTS].astype(jnp.int32)
    meta, gap_rows, n_rows = _row_layout(counts, n * TOP_K)
    slots = (meta[0][top_idx[:, :TOP_K]] + rank[:, :TOP_K]).astype(jnp.int32).reshape(n * TOP_K)

    xs = _dispatch(slots, gap_rows, meta[2], u2, n_rows)
    hid = _expert_up(meta, xs, w_gu, b_gu)
    ys = _expert_down(meta, hid, w_dn, b_dn)
    out = _combine(slots, ys, hx, top_w, gt2x, rows_per_mod=n_t)
    return out.reshape(n_b, n_t, d)


def kernel(x, c, ctx, c_ctx, w_mod, b_mod, g_norm1, g_norm2, w_in, b_in, g_q, g_k, g_mh, w_br_m, w_br_a, w_out,
           w_router, b_router, w_gu, b_gu, w_dn, b_dn):
    assert w_mod.shape[0] == 1, "single layer: the context stream has no consumer after it"
    return _layer(x, c, ctx, c_ctx, w_mod[0], b_mod[0], g_norm1[0], g_norm2[0], w_in[0], b_in[0], g_q[0], g_k[0],
                  g_mh[0], w_br_m[0], w_br_a[0], w_out[0], w_router[0], b_router[0], w_gu[0], b_gu[0],
                  w_dn[0], b_dn[0])
```

```python
import functools
import math

import jax
import jax.numpy as jnp
from jax import lax
from jax.experimental import pallas as pl
from jax.experimental.pallas import tpu as pltpu

D_MODEL = 2048
GRID_W = 64
HEAD_DIM = 128
A_HEADS = 16
KV_HEADS = 4
Q_GROUP = A_HEADS // KV_HEADS
ROPE_THETA = 10000.0
ROPE_PAIRS_AXIS = HEAD_DIM // 4
M_HEADS = 4
MV_DIM = D_MODEL // M_HEADS
MQK_DIM = MV_DIM // 2
N_EXPERTS = 32
TOP_K = 4
D_FF = D_MODEL
SWIGLU_LIMIT = 7.0
SWIGLU_ALPHA = 1.702
EPS = 1e-6

_O_MQ, _O_MK, _O_MV, _O_MO = 0, 1024, 2048, 4096
_O_GATES = 6144
GATE_COLS = 4 * M_HEADS
_O_AQ, _O_AK, _O_AV, _O_MG = 6160, 8208, 8720, 9232
_F_IN = 13328
_P_MQ, _P_MK, _P_MV, _P_MO, _P_AQ, _P_AK, _P_AV, _P_GM, _P_GA = (
    0, 1024, 2048, 4096, 6144, 8192, 8704, 9216, 11264)
_P_COLS = 13312
_C_MK, _C_MV, _C_AK, _C_AV = 0, 1024, 3072, 3584
_C_COLS = 4096

LANES = 128
M_CHUNK = 256
EXPERT_ROWS = 256
ROW_ALIGN = 16
BLOCK_DMA_PARTS = 4
WEIGHT_DMA_PARTS = 8
ROW_DMA_PRIORITY = 1
EXPERT_COL_TILES = 2
NEG_BIG = -1e30
VMEM_LIMIT = 56 * 1024 * 1024

_HI = lax.Precision.HIGHEST


def _cparams(sem, vmem=VMEM_LIMIT):
    return pltpu.CompilerParams(dimension_semantics=sem, vmem_limit_bytes=vmem)


def _mod_kernel(c_ref, w_ref, b_ref, o_ref):
    c = c_ref[...]
    a = c * jax.nn.sigmoid(c)
    o_ref[...] = lax.dot_general(a, w_ref[...], (((1,), (0,)), ((), ())), precision=_HI,
                                 preferred_element_type=jnp.float32) + b_ref[...]


def _modulation(c8, w_mod, b_mod):
    d, n = w_mod.shape
    tn = 1024
    return pl.pallas_call(
        _mod_kernel,
        grid=(n // tn,),
        in_specs=[pl.BlockSpec((8, d), lambda j: (0, 0)),
                  pl.BlockSpec((d, tn), lambda j: (0, j)),
                  pl.BlockSpec((1, tn), lambda j: (0, j))],
        out_specs=pl.BlockSpec((8, tn), lambda j: (0, j)),
        out_shape=jax.ShapeDtypeStruct((8, n), jnp.float32),
        compiler_params=_cparams(("arbitrary",)),
        name="modulation",
    )(c8, w_mod, b_mod.reshape(1, n))


def _adaln_kernel(x_ref, g_ref, sc_ref, sh_ref, wg_ref, bg_ref, u_ref, og_ref):
    x = x_ref[...]
    y = x * lax.rsqrt(jnp.mean(x * x, axis=-1, keepdims=True) + EPS) * g_ref[...]
    u = y * (1.0 + sc_ref[0]) + sh_ref[0]
    u_ref[...] = u.astype(jnp.bfloat16)
    og_ref[...] = lax.dot_general(u, wg_ref[...], (((1,), (1,)), ((), ())), precision=_HI,
                                  preferred_element_type=jnp.float32) + bg_ref[...]


def _adaln_norm(x2, g, sc, sh, w_in_t, b_gate, rows_per_mod, tm=512):
    n, d = x2.shape
    tiles_per_mod = rows_per_mod // tm
    gate_blk = _O_GATES // GATE_COLS
    return pl.pallas_call(
        _adaln_kernel,
        grid=(n // tm,),
        in_specs=[pl.BlockSpec((tm, d), lambda i: (i, 0)),
                  pl.BlockSpec((1, d), lambda i: (0, 0)),
                  pl.BlockSpec((1, 1, d), lambda i: (i // tiles_per_mod, 0, 0)),
                  pl.BlockSpec((1, 1, d), lambda i: (i // tiles_per_mod, 0, 0)),
                  pl.BlockSpec((GATE_COLS, d), lambda i: (gate_blk, 0)),
                  pl.BlockSpec((1, GATE_COLS), lambda i: (0, 0))],
        out_specs=[pl.BlockSpec((tm, d), lambda i: (i, 0)),
                   pl.BlockSpec((tm, GATE_COLS), lambda i: (i, 0))],
        out_shape=[jax.ShapeDtypeStruct((n, d), jnp.bfloat16),
                   jax.ShapeDtypeStruct((n, GATE_COLS), jnp.float32)],
        compiler_params=_cparams(("arbitrary",)),
        name="adaln_norm",
    )(x2, g, sc, sh, w_in_t, b_gate)


PROJ_TN = 1024


def _proj_kernel(tiles_ref, u_ref, w_ref, wx_ref, b_ref, bx_ref, o_ref, w_s, b_s):
    t = tiles_ref[pl.program_id(0)]
    shifted = t * PROJ_TN >= _O_GATES
    keep = PROJ_TN - GATE_COLS

    @pl.when((pl.program_id(1) == 0) & jnp.logical_not(shifted))
    def _():
        w_s[...] = w_ref[...].astype(jnp.bfloat16)
        b_s[...] = b_ref[...]

    @pl.when((pl.program_id(1) == 0) & shifted)
    def _():
        w_s[0:keep, :] = w_ref[GATE_COLS:, :].astype(jnp.bfloat16)
        w_s[keep:, :] = wx_ref[...].astype(jnp.bfloat16)
        b_s[...] = jnp.concatenate([b_ref[:, GATE_COLS:], bx_ref[:, :GATE_COLS]], axis=1)

    acc = lax.dot_general(u_ref[...], w_s[...], (((1,), (1,)), ((), ())), preferred_element_type=jnp.float32)
    o_ref[...] = (acc + b_s[...]).astype(o_ref.dtype)


def _in_projection(u, w_in_t, b_in2, tiles, tm=1024):
    n, d = u.shape
    n_tiles = tiles.shape[0]
    tn = PROJ_TN
    return pl.pallas_call(
        _proj_kernel,
        grid_spec=pltpu.PrefetchScalarGridSpec(
            num_scalar_prefetch=1,
            grid=(n_tiles, n // tm),
            in_specs=[pl.BlockSpec((tm, d), lambda j, i, t: (i, 0)),
                      pl.BlockSpec((tn, d), lambda j, i, t: (t[j], 0)),
                      pl.BlockSpec((GATE_COLS, d), lambda j, i, t: ((tn // GATE_COLS) * (t[j] + 1), 0)),
                      pl.BlockSpec((1, tn), lambda j, i, t: (0, t[j])),
                      pl.BlockSpec((1, LANES), lambda j, i, t: (0, (tn // LANES) * (t[j] + 1)))],
            out_specs=pl.BlockSpec((tm, tn), lambda j, i, t: (i, j)),
            scratch_shapes=[pltpu.VMEM((tn, d), jnp.bfloat16), pltpu.VMEM((1, tn), jnp.float32)]),
        out_shape=jax.ShapeDtypeStruct((n, n_tiles * tn), jnp.bfloat16),
        compiler_params=_cparams(("arbitrary", "arbitrary")),
        name="in_projection",
    )(tiles, u, w_in_t, w_in_t, b_in2, b_in2)


def _rms_head(x, g):
    return x * lax.rsqrt(jnp.mean(x * x, axis=-1, keepdims=True) + EPS) * g


def _rope(x, cos_e, sin_s):
    lane = lax.broadcasted_iota(jnp.int32, x.shape, 1)
    swapped = jnp.where(lane % 2 == 0, pltpu.roll(x, LANES - 1, 1), pltpu.roll(x, 1, 1))
    return x * cos_e + swapped * sin_s


def _attn_kernel(q_ref, kx_ref, vx_ref, kc_ref, vc_ref, cosq_ref, sinq_ref, cosk_ref, sink_ref,
                 gq_ref, gk_ref, o_ref, k_s, v_s):
    n_t = kx_ref.shape[1]

    @pl.when(pl.program_id(2) == 0)
    def _():
        kx = _rms_head(kx_ref[0].astype(jnp.float32), gk_ref[...])
        k_s[0:n_t, :] = _rope(kx, cosk_ref[...], sink_ref[...]).astype(jnp.bfloat16)
        k_s[n_t:, :] = _rms_head(kc_ref[0].astype(jnp.float32), gk_ref[...]).astype(jnp.bfloat16)
        v_s[0:n_t, :] = vx_ref[0]
        v_s[n_t:, :] = vc_ref[0]

    scale = HEAD_DIM ** -0.5
    kk = k_s[...]
    vv = v_s[...]
    for g in range(Q_GROUP):
        q = q_ref[0, :, g * HEAD_DIM:(g + 1) * HEAD_DIM].astype(jnp.float32)
        q = _rope(_rms_head(q, gq_ref[...]), cosq_ref[...], sinq_ref[...]) * scale
        s = lax.dot_general(q.astype(jnp.bfloat16), kk, (((1,), (1,)), ((), ())),
                            preferred_element_type=jnp.float32)
        p = jnp.exp(s - jnp.max(s, axis=-1, keepdims=True))
        l = jnp.sum(p, axis=-1, keepdims=True)
        o = jnp.dot(p.astype(jnp.bfloat16), vv, preferred_element_type=jnp.float32) / l
        o_ref[0, :, g * HEAD_DIM:(g + 1) * HEAD_DIM] = o.astype(o_ref.dtype)


def _attention(px, pc, cos_e, sin_s, g_q, g_k, tq=256):
    n_b, n_t, _ = px.shape
    n_c = pc.shape[1]
    gw = Q_GROUP * HEAD_DIM
    return pl.pallas_call(
        _attn_kernel,
        grid=(n_b, KV_HEADS, n_t // tq),
        in_specs=[pl.BlockSpec((1, tq, gw), lambda b, h, i: (b, i, _P_AQ // gw + h)),
                  pl.BlockSpec((1, n_t, HEAD_DIM), lambda b, h, i: (b, 0, _P_AK // HEAD_DIM + h)),
                  pl.BlockSpec((1, n_t, HEAD_DIM), lambda b, h, i: (b, 0, _P_AV // HEAD_DIM + h)),
                  pl.BlockSpec((1, n_c, HEAD_DIM), lambda b, h, i: (b, 0, _C_AK // HEAD_DIM + h)),
                  pl.BlockSpec((1, n_c, HEAD_DIM), lambda b, h, i: (b, 0, _C_AV // HEAD_DIM + h)),
                  pl.BlockSpec((tq, HEAD_DIM), lambda b, h, i: (i, 0)),
                  pl.BlockSpec((tq, HEAD_DIM), lambda b, h, i: (i, 0)),
                  pl.BlockSpec((n_t, HEAD_DIM), lambda b, h, i: (0, 0)),
                  pl.BlockSpec((n_t, HEAD_DIM), lambda b, h, i: (0, 0)),
                  pl.BlockSpec((1, HEAD_DIM), lambda b, h, i: (0, 0)),
                  pl.BlockSpec((1, HEAD_DIM), lambda b, h, i: (0, 0))],
        out_specs=pl.BlockSpec((1, tq, gw), lambda b, h, i: (b, i, h)),
        out_shape=jax.ShapeDtypeStruct((n_b, n_t, A_HEADS * HEAD_DIM), jnp.bfloat16),
        scratch_shapes=[pltpu.VMEM((n_t + n_c, HEAD_DIM), jnp.bfloat16),
                        pltpu.VMEM((n_t + n_c, HEAD_DIM), jnp.bfloat16)],
        compiler_params=_cparams(("arbitrary", "arbitrary", "arbitrary")),
        name="attention",
    )(px, px, px, pc, pc, cos_e, sin_s, cos_e, sin_s, g_q, g_k)


def _log_sigmoid(x):
    return jnp.minimum(x, 0.0) - jnp.log(1.0 + jnp.exp(-jnp.abs(x)))


def _mlstm_chunk(q, k, v, i_row, f_row, i_col, f_col, ct_ref, n_ref, m_ref, backward, want_h):
    n_l = k.shape[0]
    lf_row = _log_sigmoid(f_row)
    lf_col = _log_sigmoid(f_col)
    t_idx = lax.broadcasted_iota(jnp.int32, (n_l, n_l), 0)
    s_idx = lax.broadcasted_iota(jnp.int32, (n_l, n_l), 1)
    seen = (s_idx >= t_idx) if backward else (s_idx <= t_idx)
    b_col = jnp.sum(jnp.where(seen, lf_row, 0.0), axis=1, keepdims=True)
    seen_t = (t_idx >= s_idx) if backward else (t_idx <= s_idx)
    b_row = jnp.sum(jnp.where(seen_t, lf_col, 0.0), axis=0, keepdims=True)
    b_end = jnp.sum(lf_row, axis=1, keepdims=True)
    m_prev = m_ref[...]
    ct = ct_ref[...]
    n_vec = n_ref[...]
    kf = k.astype(jnp.float32)

    h = None
    if want_h:
        d_intra = jnp.where(seen, b_col - b_row + i_row, NEG_BIG)
        d_inter = b_col + m_prev
        m_t = jnp.maximum(d_inter, jnp.max(d_intra, axis=1, keepdims=True))
        qk = lax.dot_general(q, k, (((1,), (1,)), ((), ())), preferred_element_type=jnp.float32)
        s = qk * jnp.exp(d_intra - m_t)
        w_inter = jnp.exp(d_inter - m_t)
        num = (jnp.dot(s.astype(jnp.bfloat16), v, preferred_element_type=jnp.float32)
               + w_inter * jnp.dot(q, ct.astype(jnp.bfloat16), preferred_element_type=jnp.float32))
        den = (jnp.sum(s, axis=1, keepdims=True)
               + w_inter * jnp.sum(q.astype(jnp.float32) * n_vec, axis=1, keepdims=True))
        h = num / jnp.maximum(jnp.abs(den), jnp.exp(-m_t))

    g_col = b_end - b_col + i_col
    m_new = jnp.maximum(b_end + m_prev, jnp.max(g_col, axis=0, keepdims=True))
    w_s = jnp.exp(g_col - m_new)
    w_c = jnp.exp(b_end + m_prev - m_new)
    wv = (w_s * v.astype(jnp.float32)).astype(jnp.bfloat16)
    ct_ref[...] = w_c * ct + lax.dot_general(k, wv, (((0,), (0,)), ((), ())),
                                             preferred_element_type=jnp.float32)
    n_ref[...] = w_c * n_vec + jnp.sum(w_s * kf, axis=0, keepdims=True)
    m_ref[...] = m_new
    return h


def _mlstm_kernel(backward, *refs):
    if backward:
        (q_ref, k_ref, v_ref, kc_ref, vc_ref, gr_ref, gc_ref, grc_ref, gcc_ref,
         hf_ref, op_ref, gmh_ref, o_ref, ct_ref, n_ref, m_ref) = refs
    else:
        (q_ref, k_ref, v_ref, kc_ref, vc_ref, gr_ref, gc_ref, grc_ref, gcc_ref,
         o_ref, ct_ref, n_ref, m_ref) = refs
    gi, gf = (2, 3) if backward else (0, 1)
    step = pl.program_id(2)

    @pl.when(step == 0)
    def _():
        ct_ref[...] = jnp.zeros_like(ct_ref)
        n_ref[...] = jnp.zeros_like(n_ref)
        m_ref[...] = jnp.zeros_like(m_ref)
        _mlstm_chunk(None, kc_ref[0], vc_ref[0],
                     grc_ref[0, 0, gi:gi + 1, :], grc_ref[0, 0, gf:gf + 1, :],
                     gcc_ref[0, 0, :, gi:gi + 1], gcc_ref[0, 0, :, gf:gf + 1],
                     ct_ref, n_ref, m_ref, backward, False)

    @pl.when(step > 0)
    def _():
        q = (q_ref[0].astype(jnp.float32) * (MQK_DIM ** -0.5)).astype(jnp.bfloat16)
        h = _mlstm_chunk(q, k_ref[0], v_ref[0],
                         gr_ref[0, 0, gi:gi + 1, :], gr_ref[0, 0, gf:gf + 1, :],
                         gc_ref[0, 0, :, gi:gi + 1], gc_ref[0, 0, :, gf:gf + 1],
                         ct_ref, n_ref, m_ref, backward, True)
        if backward:
            ht = h + hf_ref[0]
            y = ht * lax.rsqrt(jnp.mean(ht * ht, axis=-1, keepdims=True) + EPS) * gmh_ref[...]
            o_ref[0] = (y * jax.nn.sigmoid(op_ref[0].astype(jnp.float32))).astype(o_ref.dtype)
        else:
            o_ref[0] = h


def _mlstm(px, pc, g_rows, g_cols, g_rows_c, g_cols_c, backward, h_fwd=None, g_mh=None):
    n_b, n_t, _ = px.shape
    n_c = pc.shape[1]
    n_l = M_CHUNK
    assert n_c == n_l and n_t % n_l == 0
    n_chunk = n_t // n_l

    if backward:
        def cidx(s):
            return jnp.minimum(n_chunk - s, n_chunk - 1)
    else:
        def cidx(s):
            return jnp.maximum(s - 1, 0)

    in_specs = [
        pl.BlockSpec((1, n_l, MQK_DIM), lambda b, h, s: (b, cidx(s), _P_MQ // MQK_DIM + h)),
        pl.BlockSpec((1, n_l, MQK_DIM), lambda b, h, s: (b, cidx(s), _P_MK // MQK_DIM + h)),
        pl.BlockSpec((1, n_l, MV_DIM), lambda b, h, s: (b, cidx(s), _P_MV // MV_DIM + h)),
        pl.BlockSpec((1, n_c, MQK_DIM), lambda b, h, s: (b, 0, _C_MK // MQK_DIM + h)),
        pl.BlockSpec((1, n_c, MV_DIM), lambda b, h, s: (b, 0, _C_MV // MV_DIM + h)),
        pl.BlockSpec((1, 1, 4, n_l), lambda b, h, s: (b, h, 0, cidx(s))),
        pl.BlockSpec((1, 1, n_l, 4), lambda b, h, s: (b, h, cidx(s), 0)),
        pl.BlockSpec((1, 1, 4, n_c), lambda b, h, s: (b, h, 0, 0)),
        pl.BlockSpec((1, 1, n_c, 4), lambda b, h, s: (b, h, 0, 0)),
    ]
    args = [px, px, px, pc, pc, g_rows, g_cols, g_rows_c, g_cols_c]
    if backward:
        in_specs += [
            pl.BlockSpec((1, n_l, MV_DIM), lambda b, h, s: (b, cidx(s), h)),
            pl.BlockSpec((1, n_l, MV_DIM), lambda b, h, s: (b, cidx(s), _P_MO // MV_DIM + h)),
            pl.BlockSpec((1, MV_DIM), lambda b, h, s: (0, h)),
        ]
        args += [h_fwd, px, g_mh]
        out_dtype = jnp.bfloat16
    else:
        out_dtype = jnp.float32
    return pl.pallas_call(
        functools.partial(_mlstm_kernel, backward),
        grid=(n_b, M_HEADS, n_chunk + 1),
        in_specs=in_specs,
        out_specs=pl.BlockSpec((1, n_l, MV_DIM), lambda b, h, s: (b, cidx(s), h)),
        out_shape=jax.ShapeDtypeStruct((n_b, n_t, M_HEADS * MV_DIM), out_dtype),
        scratch_shapes=[pltpu.VMEM((MQK_DIM, MV_DIM), jnp.float32),
                        pltpu.VMEM((1, MQK_DIM), jnp.float32),
                        pltpu.VMEM((1, 1), jnp.float32)],
        compiler_params=_cparams(("arbitrary", "arbitrary", "arbitrary")),
        name="mlstm_bwd" if backward else "mlstm_fwd",
    )(*args)


def _merge_kernel(m_ref, a_ref, gm_ref, ga_ref, wm_ref, wa_ref, o_ref):
    zm = jnp.dot(m_ref[...], wm_ref[...], preferred_element_type=jnp.float32)
    za = jnp.dot(a_ref[...], wa_ref[...], preferred_element_type=jnp.float32)
    z = (jax.nn.sigmoid(gm_ref[...].astype(jnp.float32)) * zm
         + jax.nn.sigmoid(ga_ref[...].astype(jnp.float32)) * za)
    o_ref[...] = z.astype(o_ref.dtype)


def _merge(m_out, a_out, px2, w_br_m, w_br_a, tm=512, tn=512):
    n, d = m_out.shape
    return pl.pallas_call(
        _merge_kernel,
        grid=(n // tm, d // tn),
        in_specs=[pl.BlockSpec((tm, d), lambda i, j: (i, 0)),
                  pl.BlockSpec((tm, d), lambda i, j: (i, 0)),
                  pl.BlockSpec((tm, tn), lambda i, j: (i, _P_GM // tn + j)),
                  pl.BlockSpec((tm, tn), lambda i, j: (i, _P_GA // tn + j)),
                  pl.BlockSpec((d, tn), lambda i, j: (0, j)),
                  pl.BlockSpec((d, tn), lambda i, j: (0, j))],
        out_specs=pl.BlockSpec((tm, tn), lambda i, j: (i, j)),
        out_shape=jax.ShapeDtypeStruct((n, d), jnp.bfloat16),
        compiler_params=_cparams(("arbitrary", "arbitrary")),
        name="merge",
    )(m_out, a_out, px2, px2, w_br_m, w_br_a)


def _split_bf16(x):
    hi = x.astype(jnp.bfloat16)
    lo = (x - hi.astype(jnp.float32)).astype(jnp.bfloat16)
    return hi, lo


def _outproj_router_kernel(z_ref, x_ref, wo_ref, gt_ref, g2_ref, sc_ref, sh_ref, wrh_ref, wrl_ref, br_ref,
                           hx_ref, u2_ref, idx_ref, wgt_ref, rank_ref, cnt_ref, carry_ref):
    tm = z_ref.shape[0]

    @pl.when(pl.program_id(0) == 0)
    def _():
        carry_ref[...] = jnp.zeros_like(carry_ref)

    y = jnp.dot(z_ref[...], wo_ref[...], preferred_element_type=jnp.float32)
    hx = x_ref[...] + gt_ref[0] * y
    hx_ref[...] = hx
    u2 = (hx * lax.rsqrt(jnp.mean(hx * hx, axis=-1, keepdims=True) + EPS) * g2_ref[...]
          * (1.0 + sc_ref[0]) + sh_ref[0])
    u2_ref[...] = u2

    u_hi, u_lo = _split_bf16(u2)
    logits = (jnp.dot(u_hi, wrh_ref[...], preferred_element_type=jnp.float32)
              + jnp.dot(u_lo, wrh_ref[...], preferred_element_type=jnp.float32)
              + jnp.dot(u_hi, wrl_ref[...], preferred_element_type=jnp.float32)) + br_ref[...]
    lane = lax.broadcasted_iota(jnp.int32, (tm, LANES), 1)
    logits = jnp.where(lane < N_EXPERTS, logits, NEG_BIG)

    idx_out = jnp.zeros((tm, LANES), jnp.int32)
    val_out = jnp.zeros((tm, LANES), jnp.float32)
    chosen = jnp.zeros((tm, LANES), jnp.float32)
    sel = []
    top0 = None
    for k in range(TOP_K):
        mx = jnp.max(logits, axis=-1, keepdims=True)
        ix = jnp.min(jnp.where(logits == mx, lane, LANES), axis=-1, keepdims=True)
        hit = lane == ix
        if k == 0:
            top0 = mx
        idx_out = jnp.where(lane == k, ix, idx_out)
        val_out = jnp.where(lane == k, jnp.exp(mx - top0), val_out)
        chosen = jnp.where(hit, 1.0, chosen)
        sel.append(hit)
        logits = jnp.where(hit, NEG_BIG, logits)
    idx_ref[...] = idx_out
    wgt_ref[...] = val_out / jnp.sum(val_out, axis=-1, keepdims=True)

    r_idx = lax.broadcasted_iota(jnp.int32, (tm, tm), 0)
    c_idx = lax.broadcasted_iota(jnp.int32, (tm, tm), 1)
    before = jnp.where(c_idx < r_idx, 1.0, 0.0).astype(jnp.bfloat16)
    prior = jnp.dot(before, chosen.astype(jnp.bfloat16), preferred_element_type=jnp.float32) + carry_ref[...]
    rank_out = jnp.zeros((tm, LANES), jnp.int32)
    for k in range(TOP_K):
        rk = jnp.sum(jnp.where(sel[k], prior, 0.0), axis=-1, keepdims=True)
        rank_out = jnp.where(lane == k, rk.astype(jnp.int32), rank_out)
    rank_ref[...] = rank_out
    carry_ref[...] = carry_ref[...] + jnp.sum(chosen, axis=0, keepdims=True)
    cnt_ref[...] = jnp.broadcast_to(carry_ref[...], cnt_ref.shape)


def _outproj_router(z, x2, w_out, gt1, g2, sc2, sh2, wr_hi, wr_lo, br, rows_per_mod, tm=256):
    n, d = z.shape
    tiles_per_mod = rows_per_mod // tm
    row = lambda i: (i, 0)
    fixed = lambda i: (0, 0)
    modi = lambda i: (i // tiles_per_mod, 0, 0)
    return pl.pallas_call(
        _outproj_router_kernel,
        grid=(n // tm,),
        in_specs=[pl.BlockSpec((tm, d), row),
                  pl.BlockSpec((tm, d), row),
                  pl.BlockSpec((d, d), fixed),
                  pl.BlockSpec((1, 1, d), modi),
                  pl.BlockSpec((1, d), fixed),
                  pl.BlockSpec((1, 1, d), modi),
                  pl.BlockSpec((1, 1, d), modi),
                  pl.BlockSpec((d, LANES), fixed),
                  pl.BlockSpec((d, LANES), fixed),
                  pl.BlockSpec((1, LANES), fixed)],
        out_specs=[pl.BlockSpec((tm, d), row),
                   pl.BlockSpec((tm, d), row),
                   pl.BlockSpec((tm, LANES), row),
                   pl.BlockSpec((tm, LANES), row),
                   pl.BlockSpec((tm, LANES), row),
                   pl.BlockSpec((8, LANES), fixed)],
        out_shape=[jax.ShapeDtypeStruct((n, d), jnp.float32),
                   jax.ShapeDtypeStruct((n, d), jnp.float32),
                   jax.ShapeDtypeStruct((n, LANES), jnp.int32),
                   jax.ShapeDtypeStruct((n, LANES), jnp.float32),
                   jax.ShapeDtypeStruct((n, LANES), jnp.int32),
                   jax.ShapeDtypeStruct((8, LANES), jnp.float32)],
        scratch_shapes=[pltpu.VMEM((1, LANES), jnp.float32)],
        compiler_params=_cparams(("arbitrary",)),
        name="outproj_router",
    )(z, x2, w_out, gt1, g2, sc2, sh2, wr_hi, wr_lo, br)


def _start_zero_rows(zbuf, dst_rows, start, count, sem, wait):
    n_full = count // EXPERT_ROWS
    rem = count % EXPERT_ROWS
    pieces = [(i < n_full, start + i * EXPERT_ROWS, EXPERT_ROWS) for i in range(3)]
    off = start + n_full * EXPERT_ROWS
    p = EXPERT_ROWS // 2
    while p >= ROW_ALIGN:
        pieces.append(((rem & p) != 0, off + (rem // (2 * p)) * (2 * p), p))
        p //= 2
    for cond, row, size in pieces:
        @pl.when(cond)
        def _(row=row, size=size):
            cp = pltpu.make_async_copy(zbuf.at[pl.ds(0, size)], dst_rows(pl.multiple_of(row, ROW_ALIGN), size), sem)
            cp.wait() if wait else cp.start()


def _zero_rows(zbuf, dst_rows, start, count, sem):
    _start_zero_rows(zbuf, dst_rows, start, count, sem, wait=False)
    _start_zero_rows(zbuf, dst_rows, start, count, sem, wait=True)


def _dispatch_kernel(slot_ref, gap_ref, tail_ref, u_ref, xs_ref, zbuf, sem, zsem):
    tm = u_ref.shape[0]
    n_rows = xs_ref.shape[0]
    base = pl.program_id(0) * (tm * TOP_K)

    @pl.when(pl.program_id(0) == 0)
    def _():
        zbuf[...] = jnp.zeros_like(zbuf)
        for wait in (False, True):
            for e in range(N_EXPERTS):
                @pl.when(gap_ref[e] >= 0)
                def _(e=e, wait=wait):
                    cp = pltpu.make_async_copy(
                        zbuf.at[pl.ds(0, ROW_ALIGN)],
                        xs_ref.at[pl.ds(pl.multiple_of(gap_ref[e], ROW_ALIGN), ROW_ALIGN)], zsem)
                    cp.wait() if wait else cp.start()
        _zero_rows(zbuf, lambda r, s: xs_ref.at[pl.ds(r, s)], tail_ref[0], n_rows - tail_ref[0], zsem)

    def issue(r, carry):
        for k in range(TOP_K):
            s = slot_ref[base + r * TOP_K + k]
            pltpu.make_async_copy(u_ref.at[pl.ds(r, 1)], xs_ref.at[pl.ds(s, 1)], sem).start(priority=k % 2)
        return carry

    lax.fori_loop(0, tm, issue, 0)

    def drain(r, carry):
        for k in range(TOP_K):
            pltpu.make_async_copy(u_ref.at[pl.ds(0, 1)], xs_ref.at[pl.ds(0, 1)], sem).wait()
        return carry

    lax.fori_loop(0, tm, drain, 0)


def _dispatch(slots_flat, gap_rows, tail, u2, n_rows, tm=256):
    n, d = u2.shape
    return pl.pallas_call(
        _dispatch_kernel,
        grid_spec=pltpu.PrefetchScalarGridSpec(
            num_scalar_prefetch=3,
            grid=(n // tm,),
            in_specs=[pl.BlockSpec((tm, d), lambda i, *_: (i, 0))],
            out_specs=pl.BlockSpec(memory_space=pl.ANY),
            scratch_shapes=[pltpu.VMEM((EXPERT_ROWS, d), jnp.float32),
                            pltpu.SemaphoreType.DMA(()), pltpu.SemaphoreType.DMA(())]),
        out_shape=jax.ShapeDtypeStruct((n_rows, d), jnp.float32),
        compiler_params=_cparams(("arbitrary",)),
        name="moe_dispatch",
    )(slots_flat, gap_rows, tail, u2)


class _CopyGroup:
    def __init__(self, copies):
        self.copies = copies

    def start(self, priority=0):
        for cp in self.copies:
            cp.start(priority=priority)

    def wait(self):
        for cp in self.copies:
            cp.wait()


def _expert_rows_loop(start_ref, nblk_ref, tail_ref, order_ref, src_ref, dst_ref, ibuf, obuf, zbuf,
                      sem_in, sem_out, sem_z, prepare, compute):
    e = pl.program_id(0)
    f = pl.program_id(1)
    n_f = pl.num_programs(1)
    tile = obuf.shape[2]
    col = pl.multiple_of(f * tile, tile)
    nb = nblk_ref[e]
    base = start_ref[e]
    g0 = order_ref[0, e] + f * nb
    nxt = order_ref[1, e]

    part = EXPERT_ROWS // BLOCK_DMA_PARTS

    def rows(first_row, j):
        return pl.ds(pl.multiple_of(first_row + j * part, ROW_ALIGN), part)

    def in_copy(first_row, slot):
        return _CopyGroup([pltpu.make_async_copy(src_ref.at[rows(first_row, j)],
                                                 ibuf.at[slot, pl.ds(j * part, part)], sem_in.at[slot])
                           for j in range(BLOCK_DMA_PARTS)])

    def out_copy(k, slot):
        return _CopyGroup([pltpu.make_async_copy(obuf.at[slot, pl.ds(j * part, part)],
                                                 dst_ref.at[rows(base + k * EXPERT_ROWS, j), pl.ds(col, tile)],
                                                 sem_out.at[slot])
                           for j in range(BLOCK_DMA_PARTS)])

    @pl.when((nb > 0) & (g0 == 0))
    def _():
        in_copy(base, 0).start(priority=ROW_DMA_PRIORITY)

    prepare()

    def body(k, carry):
        slot = lax.rem(g0 + k, 2)
        oslot = lax.rem(k, 2)

        @pl.when(k + 1 < nb)
        def _():
            in_copy(base + (k + 1) * EXPERT_ROWS, 1 - slot).start(priority=ROW_DMA_PRIORITY)

        @pl.when((k + 1 == nb) & (f + 1 < n_f))
        def _():
            in_copy(base, 1 - slot).start(priority=ROW_DMA_PRIORITY)

        @pl.when((k + 1 == nb) & (f + 1 == n_f) & (nxt >= 0))
        def _():
            in_copy(start_ref[jnp.maximum(nxt, 0)], 1 - slot).start(priority=ROW_DMA_PRIORITY)

        in_copy(base, slot).wait()

        @pl.when(k >= 2)
        def _():
            out_copy(k - 2, oslot).wait()

        obuf[oslot] = compute(ibuf[slot]).astype(obuf.dtype)
        out_copy(k, oslot).start()
        return carry

    lax.fori_loop(0, nb, body, 0)

    @pl.when(nb >= 2)
    def _():
        out_copy(nb - 2, lax.rem(nb, 2)).wait()

    @pl.when(nb >= 1)
    def _():
        out_copy(nb - 1, lax.rem(nb + 1, 2)).wait()

    @pl.when(e == pl.num_programs(0) - 1)
    def _():
        zbuf[...] = jnp.zeros_like(zbuf)
        _zero_rows(zbuf, lambda r, s: dst_ref.at[pl.ds(r, s), pl.ds(col, tile)],
                   tail_ref[1], dst_ref.shape[0] - tail_ref[1], sem_z)


def _convert_slabs(slab_refs, dst):
    rows = dst.shape[0] // len(slab_refs)
    for s, ref in enumerate(slab_refs):
        dst[s * rows:(s + 1) * rows, :] = ref[...].astype(dst.dtype)


def _expert_up_kernel(start_ref, nblk_ref, tail_ref, order_ref, xs_ref, *refs):
    wg_refs, refs = refs[:WEIGHT_DMA_PARTS], refs[WEIGHT_DMA_PARTS:]
    wu_refs, refs = refs[:WEIGHT_DMA_PARTS], refs[WEIGHT_DMA_PARTS:]
    bg_ref, bu_ref, hid_ref, wg_s, wu_s, ibuf, obuf, zbuf, sem_in, sem_out, sem_z = refs

    def prepare():
        _convert_slabs(wg_refs, wg_s)
        _convert_slabs(wu_refs, wu_s)

    def compute(xb):
        x = xb.astype(jnp.bfloat16)
        gate = jnp.dot(x, wg_s[...], preferred_element_type=jnp.float32) + bg_ref[...]
        up = jnp.dot(x, wu_s[...], preferred_element_type=jnp.float32) + bu_ref[...]
        gate = jnp.minimum(gate, SWIGLU_LIMIT)
        up = jnp.clip(up, -SWIGLU_LIMIT, SWIGLU_LIMIT)
        return (up + 1.0) * gate * jax.nn.sigmoid(SWIGLU_ALPHA * gate)

    _expert_rows_loop(start_ref, nblk_ref, tail_ref, order_ref, xs_ref, hid_ref, ibuf, obuf, zbuf,
                      sem_in, sem_out, sem_z, prepare, compute)


def _expert_up(meta, xs, w_gu, b_gu):
    n_rows, d = xs.shape
    tf = D_FF // EXPERT_COL_TILES
    nfc = EXPERT_COL_TILES
    b3 = b_gu.reshape(N_EXPERTS, 1, 2 * D_FF)
    slab = d // WEIGHT_DMA_PARTS
    gate_slabs = [pl.BlockSpec((None, slab, tf), functools.partial(lambda s, e, f, *_: (e, s, f), s))
                  for s in range(WEIGHT_DMA_PARTS)]
    up_slabs = [pl.BlockSpec((None, slab, tf), functools.partial(lambda s, e, f, *_: (e, s, nfc + f), s))
                for s in range(WEIGHT_DMA_PARTS)]
    return pl.pallas_call(
        _expert_up_kernel,
        grid_spec=pltpu.PrefetchScalarGridSpec(
            num_scalar_prefetch=4,
            grid=(N_EXPERTS, nfc),
            in_specs=[pl.BlockSpec(memory_space=pl.ANY)] + gate_slabs + up_slabs + [
                      pl.BlockSpec((None, 1, tf), lambda e, f, *_: (e, 0, f)),
                      pl.BlockSpec((None, 1, tf), lambda e, f, *_: (e, 0, nfc + f))],
            out_specs=pl.BlockSpec(memory_space=pl.ANY),
            scratch_shapes=[pltpu.VMEM((d, tf), jnp.bfloat16), pltpu.VMEM((d, tf), jnp.bfloat16),
                            pltpu.VMEM((2, EXPERT_ROWS, d), jnp.float32),
                            pltpu.VMEM((2, EXPERT_ROWS, tf), jnp.bfloat16),
                            pltpu.VMEM((EXPERT_ROWS, tf), jnp.bfloat16),
                            pltpu.SemaphoreType.DMA((2,)), pltpu.SemaphoreType.DMA((2,)),
                            pltpu.SemaphoreType.DMA(())]),
        out_shape=jax.ShapeDtypeStruct((n_rows, D_FF), jnp.bfloat16),
        compiler_params=_cparams(("arbitrary", "arbitrary")),
        name="expert_up",
    )(*meta, xs, *([w_gu] * (2 * WEIGHT_DMA_PARTS)), b3, b3)


def _expert_down_kernel(start_ref, nblk_ref, tail_ref, order_ref, hid_ref, *refs):
    wd_refs, refs = refs[:WEIGHT_DMA_PARTS], refs[WEIGHT_DMA_PARTS:]
    bd_ref, ys_ref, wd_s, ibuf, obuf, zbuf, sem_in, sem_out, sem_z = refs

    def prepare():
        _convert_slabs(wd_refs, wd_s)

    def compute(hb):
        return jnp.dot(hb, wd_s[...], preferred_element_type=jnp.float32) + bd_ref[...]

    _expert_rows_loop(start_ref, nblk_ref, tail_ref, order_ref, hid_ref, ys_ref, ibuf, obuf, zbuf,
                      sem_in, sem_out, sem_z, prepare, compute)


def _expert_down(meta, hid, w_dn, b_dn):
    n_rows, dff = hid.shape
    d = w_dn.shape[2]
    tn = d // EXPERT_COL_TILES
    b3 = b_dn.reshape(N_EXPERTS, 1, d)
    slab = dff // WEIGHT_DMA_PARTS
    down_slabs = [pl.BlockSpec((None, slab, tn), functools.partial(lambda s, e, f, *_: (e, s, f), s))
                  for s in range(WEIGHT_DMA_PARTS)]
    return pl.pallas_call(
        _expert_down_kernel,
        grid_spec=pltpu.PrefetchScalarGridSpec(
            num_scalar_prefetch=4,
            grid=(N_EXPERTS, d // tn),
            in_specs=[pl.BlockSpec(memory_space=pl.ANY)] + down_slabs + [
                      pl.BlockSpec((None, 1, tn), lambda e, f, *_: (e, 0, f))],
            out_specs=pl.BlockSpec(memory_space=pl.ANY),
            scratch_shapes=[pltpu.VMEM((dff, tn), jnp.bfloat16),
                            pltpu.VMEM((2, EXPERT_ROWS, dff), jnp.bfloat16),
                            pltpu.VMEM((2, EXPERT_ROWS, tn), jnp.float32),
                            pltpu.VMEM((EXPERT_ROWS, tn), jnp.float32),
                            pltpu.SemaphoreType.DMA((2,)), pltpu.SemaphoreType.DMA((2,)),
                            pltpu.SemaphoreType.DMA(())]),
        out_shape=jax.ShapeDtypeStruct((n_rows, d), jnp.float32),
        compiler_params=_cparams(("arbitrary", "arbitrary")),
        name="expert_down",
    )(*meta, hid, *([w_dn] * WEIGHT_DMA_PARTS), b3)


def _row_layout(counts, n_assign):
    c_al = (counts + ROW_ALIGN - 1) // ROW_ALIGN * ROW_ALIGN
    start = jnp.cumsum(c_al) - c_al
    total = jnp.sum(c_al)
    nblk = (counts + EXPERT_ROWS - 1) // EXPERT_ROWS
    covered = jnp.max(start + nblk * EXPERT_ROWS)
    gap_rows = jnp.where(counts > 0, start + c_al - ROW_ALIGN, -1)
    n_rows = n_assign + N_EXPERTS * ROW_ALIGN + EXPERT_ROWS
    i32 = lambda a: a.astype(jnp.int32)
    tail = i32(jnp.stack([total, covered]))
    blocks_before = EXPERT_COL_TILES * (jnp.cumsum(nblk) - nblk)
    ids = jnp.arange(N_EXPERTS)
    later = lax.cummin(jnp.where(nblk > 0, ids, N_EXPERTS), reverse=True)
    nxt = jnp.concatenate([later[1:], jnp.full((1,), N_EXPERTS)])
    nxt = jnp.where(nxt < N_EXPERTS, nxt, -1)
    order = i32(jnp.stack([blocks_before, nxt]))
    return (i32(start), i32(nblk), tail, order), i32(gap_rows), n_rows


def _combine_kernel(slot_ref, ys_ref, hx_ref, wgt_ref, gt_ref, o_ref, buf, sem):
    tm = hx_ref.shape[0]
    i = pl.program_id(0)
    n_i = pl.num_programs(0)

    def issue(tile, b):
        base = tile * (tm * TOP_K)

        def body(r, carry):
            for k in range(TOP_K):
                s = slot_ref[base + r * TOP_K + k]
                pltpu.make_async_copy(ys_ref.at[pl.ds(s, 1)], buf.at[b, k, pl.ds(r, 1)],
                                      sem.at[b]).start(priority=k % 2)
            return carry

        lax.fori_loop(0, tm, body, 0)

    @pl.when(i == 0)
    def _():
        issue(0, 0)

    @pl.when(i + 1 < n_i)
    def _():
        issue(i + 1, (i + 1) % 2)

    cur = i % 2

    def drain(r, carry):
        for k in range(TOP_K):
            pltpu.make_async_copy(ys_ref.at[pl.ds(0, 1)], buf.at[cur, k, pl.ds(0, 1)], sem.at[cur]).wait()
        return carry

    lax.fori_loop(0, tm, drain, 0)

    wgt = wgt_ref[...]
    acc = wgt[:, 0:1] * buf[cur, 0]
    for k in range(1, TOP_K):
        acc = acc + wgt[:, k:k + 1] * buf[cur, k]
    o_ref[...] = hx_ref[...] + gt_ref[0] * acc


def _combine(slots_flat, ys, hx, wgt, gt2, rows_per_mod, tm=128):
    n, d = hx.shape
    tiles_per_mod = rows_per_mod // tm
    return pl.pallas_call(
        _combine_kernel,
        grid_spec=pltpu.PrefetchScalarGridSpec(
            num_scalar_prefetch=1,
            grid=(n // tm,),
            in_specs=[pl.BlockSpec(memory_space=pl.ANY),
                      pl.BlockSpec((tm, d), lambda i, s: (i, 0)),
                      pl.BlockSpec((tm, LANES), lambda i, s: (i, 0)),
                      pl.BlockSpec((1, 1, d), lambda i, s: (i // tiles_per_mod, 0, 0))],
            out_specs=pl.BlockSpec((tm, d), lambda i, s: (i, 0)),
            scratch_shapes=[pltpu.VMEM((2, TOP_K, tm, d), jnp.float32),
                            pltpu.SemaphoreType.DMA((2,))]),
        out_shape=jax.ShapeDtypeStruct((n, d), jnp.float32),
        compiler_params=_cparams(("arbitrary",)),
        name="moe_combine",
    )(slots_flat, ys, hx, wgt, gt2)


def _rope_tables(n_t):
    rows = n_t // GRID_W
    row_ids = jnp.repeat(jnp.arange(rows), GRID_W).astype(jnp.float32)
    col_ids = jnp.tile(jnp.arange(GRID_W), rows).astype(jnp.float32)
    freqs = jnp.exp(-math.log(ROPE_THETA) * jnp.arange(ROPE_PAIRS_AXIS, dtype=jnp.float32) / ROPE_PAIRS_AXIS)
    ang = jnp.concatenate([row_ids[:, None] * freqs, col_ids[:, None] * freqs], axis=-1)
    cos_e = jnp.repeat(jnp.cos(ang), 2, axis=-1)
    sin = jnp.sin(ang)
    sin_s = jnp.stack([-sin, sin], axis=-1).reshape(n_t, HEAD_DIM)
    return cos_e, sin_s


def _gate_layouts(og, n_b, n_t):
    g = og[:, :4 * M_HEADS].reshape(n_b, n_t, 4, M_HEADS)
    return g.transpose(0, 3, 2, 1), g.transpose(0, 3, 1, 2)


def _layer(x, c, ctx, c_ctx, w_mod, b_mod, g_norm1, g_norm2, w_in, b_in, g_q, g_k, g_mh,
           w_br_m, w_br_a, w_out, w_router, b_router, w_gu, b_gu, w_dn, b_dn):
    n_b, n_t, d = x.shape
    n_c = ctx.shape[1]
    bf = jnp.bfloat16

    b_in2 = b_in.reshape(1, _F_IN)
    wr = jnp.pad(w_router, ((0, 0), (0, LANES - N_EXPERTS)))
    wr_hi = wr.astype(bf)
    wr_lo = (wr - wr_hi.astype(jnp.float32)).astype(bf)
    br = jnp.pad(b_router, (0, LANES - N_EXPERTS)).reshape(1, LANES)

    c8 = jnp.zeros((8, d), jnp.float32).at[:n_b].set(c).at[n_b].set(c_ctx)
    mod = _modulation(c8, w_mod, b_mod)
    mod6 = mod.reshape(8, 6, d)
    sh1x, sc1x, gt1x, sh2x, sc2x, gt2x = [mod6[:n_b, i].reshape(n_b, 1, d) for i in range(6)]
    sh1c, sc1c = [mod6[n_b:n_b + 1, i].reshape(1, 1, d) for i in range(2)]

    g1 = g_norm1.reshape(1, d)
    x2 = x.reshape(n_b * n_t, d)
    w_in_t = w_in.T
    b_gate = b_in[_O_GATES:_O_AQ].reshape(1, GATE_COLS)
    ux, ogx = _adaln_norm(x2, g1, sc1x, sh1x, w_in_t, b_gate, rows_per_mod=n_t)
    uc, ogc = _adaln_norm(ctx.reshape(n_b * n_c, d), g1, sc1c, sh1c, w_in_t, b_gate, rows_per_mod=n_b * n_c)
    px2 = _in_projection(ux, w_in_t, b_in2, jnp.arange(_P_COLS // PROJ_TN, dtype=jnp.int32))
    ctx_tiles = jnp.array([_P_MK // PROJ_TN, _P_MV // PROJ_TN, _P_MV // PROJ_TN + 1, _P_AK // PROJ_TN], jnp.int32)
    pc2 = _in_projection(uc, w_in_t, b_in2, ctx_tiles)
    px = px2.reshape(n_b, n_t, _P_COLS)
    pc = pc2.reshape(n_b, n_c, _C_COLS)

    cos_e, sin_s = _rope_tables(n_t)
    a_out = _attention(px, pc, cos_e, sin_s, g_q.reshape(1, HEAD_DIM), g_k.reshape(1, HEAD_DIM))

    g_rows, g_cols = _gate_layouts(ogx, n_b, n_t)
    g_rows_c, g_cols_c = _gate_layouts(ogc, n_b, n_c)
    h_fwd = _mlstm(px, pc, g_rows, g_cols, g_rows_c, g_cols_c, backward=False)
    m_out = _mlstm(px, pc, g_rows, g_cols, g_rows_c, g_cols_c, backward=True,
                   h_fwd=h_fwd, g_mh=g_mh.reshape(1, M_HEADS * MV_DIM))

    n = n_b * n_t
    z = _merge(m_out.reshape(n, d), a_out.reshape(n, d), px2, w_br_m.astype(bf), w_br_a.astype(bf))
    hx, u2, top_idx, top_w, rank, cnt = _outproj_router(
        z, x2, w_out.astype(bf), gt1x, g_norm2.reshape(1, d), sc2x, sh2x, wr_hi, wr_lo, br, rows_per_mod=n_t)

    counts = cnt[0, :N_EXPERTS].astype(jnp.int32)
    meta, gap_rows, n_rows = _row_layout(counts, n * TOP_K)
    slots = (meta[0][top_idx[:, :TOP_K]] + rank[:, :TOP_K]).astype(jnp.int32).reshape(n * TOP_K)

    xs = _dispatch(slots, gap_rows, meta[2], u2, n_rows)
    hid = _expert_up(meta, xs, w_gu, b_gu)
    ys = _expert_down(meta, hid, w_dn, b_dn)
    out = _combine(slots, ys, hx, top_w, gt2x, rows_per_mod=n_t)
    return out.reshape(n_b, n_t, d)


def kernel(x, c, ctx, c_ctx, w_mod, b_mod, g_norm1, g_norm2, w_in, b_in, g_q, g_k, g_mh, w_br_m, w_br_a, w_out,
           w_router, b_router, w_gu, b_gu, w_dn, b_dn):
    assert w_mod.shape[0] == 1, "single layer: the context stream has no consumer after it"
    return _layer(x, c, ctx, c_ctx, w_mod[0], b_mod[0], g_norm1[0], g_norm2[0], w_in[0], b_in[0], g_q[0], g_k[0],
                  g_mh[0], w_br_m[0], w_br_a[0], w_out[0], w_router[0], b_router[0], w_gu[0], b_gu[0],
                  w_dn[0], b_dn[0])
```

```python
import functools
import math

import jax
import jax.numpy as jnp
from jax import lax
from jax.experimental import pallas as pl
from jax.experimental.pallas import tpu as pltpu

D_MODEL = 2048
GRID_W = 64
HEAD_DIM = 128
A_HEADS = 16
KV_HEADS = 4
Q_GROUP = A_HEADS // KV_HEADS
ROPE_THETA = 10000.0
ROPE_PAIRS_AXIS = HEAD_DIM // 4
M_HEADS = 4
MV_DIM = D_MODEL // M_HEADS
MQK_DIM = MV_DIM // 2
N_EXPERTS = 32
TOP_K = 4
D_FF = D_MODEL
SWIGLU_LIMIT = 7.0
SWIGLU_ALPHA = 1.702
EPS = 1e-6

_O_MQ, _O_MK, _O_MV, _O_MO = 0, 1024, 2048, 4096
_O_GATES = 6144
GATE_COLS = 4 * M_HEADS
_O_AQ, _O_AK, _O_AV, _O_MG = 6160, 8208, 8720, 9232
_F_IN = 13328
_P_MQ, _P_MK, _P_MV, _P_MO, _P_AQ, _P_AK, _P_AV, _P_GM, _P_GA = (
    0, 1024, 2048, 4096, 6144, 8192, 8704, 9216, 11264)
_P_COLS = 13312
_C_MK, _C_MV, _C_AK, _C_AV = 0, 1024, 3072, 3584
_C_COLS = 4096

LANES = 128
M_CHUNK = 256
EXPERT_ROWS = 256
ROW_ALIGN = 16
BLOCK_DMA_PARTS = 4
WEIGHT_DMA_PARTS = 8
ROW_DMA_PRIORITY = 1
UP_COL_TILES = 2
DOWN_COL_TILES = 1
NEG_BIG = -1e30
VMEM_LIMIT = 56 * 1024 * 1024

_HI = lax.Precision.HIGHEST


def _cparams(sem, vmem=VMEM_LIMIT):
    return pltpu.CompilerParams(dimension_semantics=sem, vmem_limit_bytes=vmem)


def _mod_kernel(c_ref, w_ref, b_ref, o_ref):
    c = c_ref[...]
    a = c * jax.nn.sigmoid(c)
    o_ref[...] = lax.dot_general(a, w_ref[...], (((1,), (0,)), ((), ())), precision=_HI,
                                 preferred_element_type=jnp.float32) + b_ref[...]


def _modulation(c8, w_mod, b_mod):
    d, n = w_mod.shape
    tn = 1024
    return pl.pallas_call(
        _mod_kernel,
        grid=(n // tn,),
        in_specs=[pl.BlockSpec((8, d), lambda j: (0, 0)),
                  pl.BlockSpec((d, tn), lambda j: (0, j)),
                  pl.BlockSpec((1, tn), lambda j: (0, j))],
        out_specs=pl.BlockSpec((8, tn), lambda j: (0, j)),
        out_shape=jax.ShapeDtypeStruct((8, n), jnp.float32),
        compiler_params=_cparams(("arbitrary",)),
        name="modulation",
    )(c8, w_mod, b_mod.reshape(1, n))


def _adaln_kernel(x_ref, g_ref, sc_ref, sh_ref, wg_ref, bg_ref, u_ref, og_ref):
    x = x_ref[...]
    y = x * lax.rsqrt(jnp.mean(x * x, axis=-1, keepdims=True) + EPS) * g_ref[...]
    u = y * (1.0 + sc_ref[0]) + sh_ref[0]
    u_ref[...] = u.astype(jnp.bfloat16)
    og_ref[...] = lax.dot_general(u, wg_ref[...], (((1,), (1,)), ((), ())), precision=_HI,
                                  preferred_element_type=jnp.float32) + bg_ref[...]


def _adaln_norm(x2, g, sc, sh, w_in_t, b_gate, rows_per_mod, tm=512):
    n, d = x2.shape
    tiles_per_mod = rows_per_mod // tm
    gate_blk = _O_GATES // GATE_COLS
    return pl.pallas_call(
        _adaln_kernel,
        grid=(n // tm,),
        in_specs=[pl.BlockSpec((tm, d), lambda i: (i, 0)),
                  pl.BlockSpec((1, d), lambda i: (0, 0)),
                  pl.BlockSpec((1, 1, d), lambda i: (i // tiles_per_mod, 0, 0)),
                  pl.BlockSpec((1, 1, d), lambda i: (i // tiles_per_mod, 0, 0)),
                  pl.BlockSpec((GATE_COLS, d), lambda i: (gate_blk, 0)),
                  pl.BlockSpec((1, GATE_COLS), lambda i: (0, 0))],
        out_specs=[pl.BlockSpec((tm, d), lambda i: (i, 0)),
                   pl.BlockSpec((tm, GATE_COLS), lambda i: (i, 0))],
        out_shape=[jax.ShapeDtypeStruct((n, d), jnp.bfloat16),
                   jax.ShapeDtypeStruct((n, GATE_COLS), jnp.float32)],
        compiler_params=_cparams(("arbitrary",)),
        name="adaln_norm",
    )(x2, g, sc, sh, w_in_t, b_gate)


PROJ_TN = 1024


def _proj_kernel(tiles_ref, u_ref, w_ref, wx_ref, b_ref, bx_ref, o_ref, w_s, b_s):
    t = tiles_ref[pl.program_id(0)]
    shifted = t * PROJ_TN >= _O_GATES
    keep = PROJ_TN - GATE_COLS

    @pl.when((pl.program_id(1) == 0) & jnp.logical_not(shifted))
    def _():
        w_s[...] = w_ref[...].astype(jnp.bfloat16)
        b_s[...] = b_ref[...]

    @pl.when((pl.program_id(1) == 0) & shifted)
    def _():
        w_s[0:keep, :] = w_ref[GATE_COLS:, :].astype(jnp.bfloat16)
        w_s[keep:, :] = wx_ref[...].astype(jnp.bfloat16)
        b_s[...] = jnp.concatenate([b_ref[:, GATE_COLS:], bx_ref[:, :GATE_COLS]], axis=1)

    acc = lax.dot_general(u_ref[...], w_s[...], (((1,), (1,)), ((), ())), preferred_element_type=jnp.float32)
    o_ref[...] = (acc + b_s[...]).astype(o_ref.dtype)


def _in_projection(u, w_in_t, b_in2, tiles, tm=1024):
    n, d = u.shape
    n_tiles = tiles.shape[0]
    tn = PROJ_TN
    return pl.pallas_call(
        _proj_kernel,
        grid_spec=pltpu.PrefetchScalarGridSpec(
            num_scalar_prefetch=1,
            grid=(n_tiles, n // tm),
            in_specs=[pl.BlockSpec((tm, d), lambda j, i, t: (i, 0)),
                      pl.BlockSpec((tn, d), lambda j, i, t: (t[j], 0)),
                      pl.BlockSpec((GATE_COLS, d), lambda j, i, t: ((tn // GATE_COLS) * (t[j] + 1), 0)),
                      pl.BlockSpec((1, tn), lambda j, i, t: (0, t[j])),
                      pl.BlockSpec((1, LANES), lambda j, i, t: (0, (tn // LANES) * (t[j] + 1)))],
            out_specs=pl.BlockSpec((tm, tn), lambda j, i, t: (i, j)),
            scratch_shapes=[pltpu.VMEM((tn, d), jnp.bfloat16), pltpu.VMEM((1, tn), jnp.float32)]),
        out_shape=jax.ShapeDtypeStruct((n, n_tiles * tn), jnp.bfloat16),
        compiler_params=_cparams(("arbitrary", "arbitrary")),
        name="in_projection",
    )(tiles, u, w_in_t, w_in_t, b_in2, b_in2)


def _rms_head(x, g):
    return x * lax.rsqrt(jnp.mean(x * x, axis=-1, keepdims=True) + EPS) * g


def _rope(x, cos_e, sin_s):
    lane = lax.broadcasted_iota(jnp.int32, x.shape, 1)
    swapped = jnp.where(lane % 2 == 0, pltpu.roll(x, LANES - 1, 1), pltpu.roll(x, 1, 1))
    return x * cos_e + swapped * sin_s


def _attn_kernel(q_ref, kx_ref, vx_ref, kc_ref, vc_ref, cosq_ref, sinq_ref, cosk_ref, sink_ref,
                 gq_ref, gk_ref, o_ref, k_s, v_s):
    n_t = kx_ref.shape[1]

    @pl.when(pl.program_id(2) == 0)
    def _():
        kx = _rms_head(kx_ref[0].astype(jnp.float32), gk_ref[...])
        k_s[0:n_t, :] = _rope(kx, cosk_ref[...], sink_ref[...]).astype(jnp.bfloat16)
        k_s[n_t:, :] = _rms_head(kc_ref[0].astype(jnp.float32), gk_ref[...]).astype(jnp.bfloat16)
        v_s[0:n_t, :] = vx_ref[0]
        v_s[n_t:, :] = vc_ref[0]

    scale = HEAD_DIM ** -0.5
    kk = k_s[...]
    vv = v_s[...]
    for g in range(Q_GROUP):
        q = q_ref[0, :, g * HEAD_DIM:(g + 1) * HEAD_DIM].astype(jnp.float32)
        q = _rope(_rms_head(q, gq_ref[...]), cosq_ref[...], sinq_ref[...]) * scale
        s = lax.dot_general(q.astype(jnp.bfloat16), kk, (((1,), (1,)), ((), ())),
                            preferred_element_type=jnp.float32)
        p = jnp.exp(s - jnp.max(s, axis=-1, keepdims=True))
        l = jnp.sum(p, axis=-1, keepdims=True)
        o = jnp.dot(p.astype(jnp.bfloat16), vv, preferred_element_type=jnp.float32) / l
        o_ref[0, :, g * HEAD_DIM:(g + 1) * HEAD_DIM] = o.astype(o_ref.dtype)


def _attention(px, pc, cos_e, sin_s, g_q, g_k, tq=256):
    n_b, n_t, _ = px.shape
    n_c = pc.shape[1]
    gw = Q_GROUP * HEAD_DIM
    return pl.pallas_call(
        _attn_kernel,
        grid=(n_b, KV_HEADS, n_t // tq),
        in_specs=[pl.BlockSpec((1, tq, gw), lambda b, h, i: (b, i, _P_AQ // gw + h)),
                  pl.BlockSpec((1, n_t, HEAD_DIM), lambda b, h, i: (b, 0, _P_AK // HEAD_DIM + h)),
                  pl.BlockSpec((1, n_t, HEAD_DIM), lambda b, h, i: (b, 0, _P_AV // HEAD_DIM + h)),
                  pl.BlockSpec((1, n_c, HEAD_DIM), lambda b, h, i: (b, 0, _C_AK // HEAD_DIM + h)),
                  pl.BlockSpec((1, n_c, HEAD_DIM), lambda b, h, i: (b, 0, _C_AV // HEAD_DIM + h)),
                  pl.BlockSpec((tq, HEAD_DIM), lambda b, h, i: (i, 0)),
                  pl.BlockSpec((tq, HEAD_DIM), lambda b, h, i: (i, 0)),
                  pl.BlockSpec((n_t, HEAD_DIM), lambda b, h, i: (0, 0)),
                  pl.BlockSpec((n_t, HEAD_DIM), lambda b, h, i: (0, 0)),
                  pl.BlockSpec((1, HEAD_DIM), lambda b, h, i: (0, 0)),
                  pl.BlockSpec((1, HEAD_DIM), lambda b, h, i: (0, 0))],
        out_specs=pl.BlockSpec((1, tq, gw), lambda b, h, i: (b, i, h)),
        out_shape=jax.ShapeDtypeStruct((n_b, n_t, A_HEADS * HEAD_DIM), jnp.bfloat16),
        scratch_shapes=[pltpu.VMEM((n_t + n_c, HEAD_DIM), jnp.bfloat16),
                        pltpu.VMEM((n_t + n_c, HEAD_DIM), jnp.bfloat16)],
        compiler_params=_cparams(("arbitrary", "arbitrary", "arbitrary")),
        name="attention",
    )(px, px, px, pc, pc, cos_e, sin_s, cos_e, sin_s, g_q, g_k)


def _log_sigmoid(x):
    return jnp.minimum(x, 0.0) - jnp.log(1.0 + jnp.exp(-jnp.abs(x)))


def _mlstm_chunk(q, k, v, i_row, f_row, i_col, f_col, ct_ref, n_ref, m_ref, backward, want_h):
    n_l = k.shape[0]
    lf_row = _log_sigmoid(f_row)
    lf_col = _log_sigmoid(f_col)
    t_idx = lax.broadcasted_iota(jnp.int32, (n_l, n_l), 0)
    s_idx = lax.broadcasted_iota(jnp.int32, (n_l, n_l), 1)
    seen = (s_idx >= t_idx) if backward else (s_idx <= t_idx)
    b_col = jnp.sum(jnp.where(seen, lf_row, 0.0), axis=1, keepdims=True)
    seen_t = (t_idx >= s_idx) if backward else (t_idx <= s_idx)
    b_row = jnp.sum(jnp.where(seen_t, lf_col, 0.0), axis=0, keepdims=True)
    b_end = jnp.sum(lf_row, axis=1, keepdims=True)
    m_prev = m_ref[...]
    ct = ct_ref[...]
    n_vec = n_ref[...]
    kf = k.astype(jnp.float32)

    h = None
    if want_h:
        d_intra = jnp.where(seen, b_col - b_row + i_row, NEG_BIG)
        d_inter = b_col + m_prev
        m_t = jnp.maximum(d_inter, jnp.max(d_intra, axis=1, keepdims=True))
        qk = lax.dot_general(q, k, (((1,), (1,)), ((), ())), preferred_element_type=jnp.float32)
        s = qk * jnp.exp(d_intra - m_t)
        w_inter = jnp.exp(d_inter - m_t)
        num = (jnp.dot(s.astype(jnp.bfloat16), v, preferred_element_type=jnp.float32)
               + w_inter * jnp.dot(q, ct.astype(jnp.bfloat16), preferred_element_type=jnp.float32))
        den = (jnp.sum(s, axis=1, keepdims=True)
               + w_inter * jnp.sum(q.astype(jnp.float32) * n_vec, axis=1, keepdims=True))
        h = num / jnp.maximum(jnp.abs(den), jnp.exp(-m_t))

    g_col = b_end - b_col + i_col
    m_new = jnp.maximum(b_end + m_prev, jnp.max(g_col, axis=0, keepdims=True))
    w_s = jnp.exp(g_col - m_new)
    w_c = jnp.exp(b_end + m_prev - m_new)
    wv = (w_s * v.astype(jnp.float32)).astype(jnp.bfloat16)
    ct_ref[...] = w_c * ct + lax.dot_general(k, wv, (((0,), (0,)), ((), ())),
                                             preferred_element_type=jnp.float32)
    n_ref[...] = w_c * n_vec + jnp.sum(w_s * kf, axis=0, keepdims=True)
    m_ref[...] = m_new
    return h


def _mlstm_kernel(backward, *refs):
    if backward:
        (q_ref, k_ref, v_ref, kc_ref, vc_ref, gr_ref, gc_ref, grc_ref, gcc_ref,
         hf_ref, op_ref, gmh_ref, o_ref, ct_ref, n_ref, m_ref) = refs
    else:
        (q_ref, k_ref, v_ref, kc_ref, vc_ref, gr_ref, gc_ref, grc_ref, gcc_ref,
         o_ref, ct_ref, n_ref, m_ref) = refs
    gi, gf = (2, 3) if backward else (0, 1)
    step = pl.program_id(2)

    @pl.when(step == 0)
    def _():
        ct_ref[...] = jnp.zeros_like(ct_ref)
        n_ref[...] = jnp.zeros_like(n_ref)
        m_ref[...] = jnp.zeros_like(m_ref)
        _mlstm_chunk(None, kc_ref[0], vc_ref[0],
                     grc_ref[0, 0, gi:gi + 1, :], grc_ref[0, 0, gf:gf + 1, :],
                     gcc_ref[0, 0, :, gi:gi + 1], gcc_ref[0, 0, :, gf:gf + 1],
                     ct_ref, n_ref, m_ref, backward, False)

    @pl.when(step > 0)
    def _():
        q = (q_ref[0].astype(jnp.float32) * (MQK_DIM ** -0.5)).astype(jnp.bfloat16)
        h = _mlstm_chunk(q, k_ref[0], v_ref[0],
                         gr_ref[0, 0, gi:gi + 1, :], gr_ref[0, 0, gf:gf + 1, :],
                         gc_ref[0, 0, :, gi:gi + 1], gc_ref[0, 0, :, gf:gf + 1],
                         ct_ref, n_ref, m_ref, backward, True)
        if backward:
            ht = h + hf_ref[0]
            y = ht * lax.rsqrt(jnp.mean(ht * ht, axis=-1, keepdims=True) + EPS) * gmh_ref[...]
            o_ref[0] = (y * jax.nn.sigmoid(op_ref[0].astype(jnp.float32))).astype(o_ref.dtype)
        else:
            o_ref[0] = h


def _mlstm(px, pc, g_rows, g_cols, g_rows_c, g_cols_c, backward, h_fwd=None, g_mh=None):
    n_b, n_t, _ = px.shape
    n_c = pc.shape[1]
    n_l = M_CHUNK
    assert n_c == n_l and n_t % n_l == 0
    n_chunk = n_t // n_l

    if backward:
        def cidx(s):
            return jnp.minimum(n_chunk - s, n_chunk - 1)
    else:
        def cidx(s):
            return jnp.maximum(s - 1, 0)

    in_specs = [
        pl.BlockSpec((1, n_l, MQK_DIM), lambda b, h, s: (b, cidx(s), _P_MQ // MQK_DIM + h)),
        pl.BlockSpec((1, n_l, MQK_DIM), lambda b, h, s: (b, cidx(s), _P_MK // MQK_DIM + h)),
        pl.BlockSpec((1, n_l, MV_DIM), lambda b, h, s: (b, cidx(s), _P_MV // MV_DIM + h)),
        pl.BlockSpec((1, n_c, MQK_DIM), lambda b, h, s: (b, 0, _C_MK // MQK_DIM + h)),
        pl.BlockSpec((1, n_c, MV_DIM), lambda b, h, s: (b, 0, _C_MV // MV_DIM + h)),
        pl.BlockSpec((1, 1, 4, n_l), lambda b, h, s: (b, h, 0, cidx(s))),
        pl.BlockSpec((1, 1, n_l, 4), lambda b, h, s: (b, h, cidx(s), 0)),
        pl.BlockSpec((1, 1, 4, n_c), lambda b, h, s: (b, h, 0, 0)),
        pl.BlockSpec((1, 1, n_c, 4), lambda b, h, s: (b, h, 0, 0)),
    ]
    args = [px, px, px, pc, pc, g_rows, g_cols, g_rows_c, g_cols_c]
    if backward:
        in_specs += [
            pl.BlockSpec((1, n_l, MV_DIM), lambda b, h, s: (b, cidx(s), h)),
            pl.BlockSpec((1, n_l, MV_DIM), lambda b, h, s: (b, cidx(s), _P_MO // MV_DIM + h)),
            pl.BlockSpec((1, MV_DIM), lambda b, h, s: (0, h)),
        ]
        args += [h_fwd, px, g_mh]
        out_dtype = jnp.bfloat16
    else:
        out_dtype = jnp.float32
    return pl.pallas_call(
        functools.partial(_mlstm_kernel, backward),
        grid=(n_b, M_HEADS, n_chunk + 1),
        in_specs=in_specs,
        out_specs=pl.BlockSpec((1, n_l, MV_DIM), lambda b, h, s: (b, cidx(s), h)),
        out_shape=jax.ShapeDtypeStruct((n_b, n_t, M_HEADS * MV_DIM), out_dtype),
        scratch_shapes=[pltpu.VMEM((MQK_DIM, MV_DIM), jnp.float32),
                        pltpu.VMEM((1, MQK_DIM), jnp.float32),
                        pltpu.VMEM((1, 1), jnp.float32)],
        compiler_params=_cparams(("arbitrary", "arbitrary", "arbitrary")),
        name="mlstm_bwd" if backward else "mlstm_fwd",
    )(*args)


def _merge_kernel(m_ref, a_ref, gm_ref, ga_ref, wm_ref, wa_ref, o_ref):
    zm = jnp.dot(m_ref[...], wm_ref[...], preferred_element_type=jnp.float32)
    za = jnp.dot(a_ref[...], wa_ref[...], preferred_element_type=jnp.float32)
    z = (jax.nn.sigmoid(gm_ref[...].astype(jnp.float32)) * zm
         + jax.nn.sigmoid(ga_ref[...].astype(jnp.float32)) * za)
    o_ref[...] = z.astype(o_ref.dtype)


def _merge(m_out, a_out, px2, w_br_m, w_br_a, tm=1024, tn=512):
    n, d = m_out.shape
    return pl.pallas_call(
        _merge_kernel,
        grid=(n // tm, d // tn),
        in_specs=[pl.BlockSpec((tm, d), lambda i, j: (i, 0)),
                  pl.BlockSpec((tm, d), lambda i, j: (i, 0)),
                  pl.BlockSpec((tm, tn), lambda i, j: (i, _P_GM // tn + j)),
                  pl.BlockSpec((tm, tn), lambda i, j: (i, _P_GA // tn + j)),
                  pl.BlockSpec((d, tn), lambda i, j: (0, j)),
                  pl.BlockSpec((d, tn), lambda i, j: (0, j))],
        out_specs=pl.BlockSpec((tm, tn), lambda i, j: (i, j)),
        out_shape=jax.ShapeDtypeStruct((n, d), jnp.bfloat16),
        compiler_params=_cparams(("arbitrary", "arbitrary")),
        name="merge",
    )(m_out, a_out, px2, px2, w_br_m, w_br_a)


def _split_bf16(x):
    hi = x.astype(jnp.bfloat16)
    lo = (x - hi.astype(jnp.float32)).astype(jnp.bfloat16)
    return hi, lo


def _outproj_router_kernel(z_ref, x_ref, wo_ref, gt_ref, g2_ref, sc_ref, sh_ref, wrh_ref, wrl_ref, br_ref,
                           hx_ref, u2_ref, idx_ref, wgt_ref, rank_ref, cnt_ref, carry_ref):
    tm = z_ref.shape[0]

    @pl.when(pl.program_id(0) == 0)
    def _():
        carry_ref[...] = jnp.zeros_like(carry_ref)

    y = jnp.dot(z_ref[...], wo_ref[...], preferred_element_type=jnp.float32)
    hx = x_ref[...] + gt_ref[0] * y
    hx_ref[...] = hx
    u2 = (hx * lax.rsqrt(jnp.mean(hx * hx, axis=-1, keepdims=True) + EPS) * g2_ref[...]
          * (1.0 + sc_ref[0]) + sh_ref[0])
    u2_ref[...] = u2

    u_hi, u_lo = _split_bf16(u2)
    logits = (jnp.dot(u_hi, wrh_ref[...], preferred_element_type=jnp.float32)
              + jnp.dot(u_lo, wrh_ref[...], preferred_element_type=jnp.float32)
              + jnp.dot(u_hi, wrl_ref[...], preferred_element_type=jnp.float32)) + br_ref[...]
    lane = lax.broadcasted_iota(jnp.int32, (tm, LANES), 1)
    logits = jnp.where(lane < N_EXPERTS, logits, NEG_BIG)

    idx_out = jnp.zeros((tm, LANES), jnp.int32)
    val_out = jnp.zeros((tm, LANES), jnp.float32)
    chosen = jnp.zeros((tm, LANES), jnp.float32)
    sel = []
    top0 = None
    for k in range(TOP_K):
        mx = jnp.max(logits, axis=-1, keepdims=True)
        ix = jnp.min(jnp.where(logits == mx, lane, LANES), axis=-1, keepdims=True)
        hit = lane == ix
        if k == 0:
            top0 = mx
        idx_out = jnp.where(lane == k, ix, idx_out)
        val_out = jnp.where(lane == k, jnp.exp(mx - top0), val_out)
        chosen = jnp.where(hit, 1.0, chosen)
        sel.append(hit)
        logits = jnp.where(hit, NEG_BIG, logits)
    idx_ref[...] = idx_out
    wgt_ref[...] = val_out / jnp.sum(val_out, axis=-1, keepdims=True)

    r_idx = lax.broadcasted_iota(jnp.int32, (tm, tm), 0)
    c_idx = lax.broadcasted_iota(jnp.int32, (tm, tm), 1)
    before = jnp.where(c_idx < r_idx, 1.0, 0.0).astype(jnp.bfloat16)
    prior = jnp.dot(before, chosen.astype(jnp.bfloat16), preferred_element_type=jnp.float32) + carry_ref[...]
    rank_out = jnp.zeros((tm, LANES), jnp.int32)
    for k in range(TOP_K):
        rk = jnp.sum(jnp.where(sel[k], prior, 0.0), axis=-1, keepdims=True)
        rank_out = jnp.where(lane == k, rk.astype(jnp.int32), rank_out)
    rank_ref[...] = rank_out
    carry_ref[...] = carry_ref[...] + jnp.sum(chosen, axis=0, keepdims=True)
    cnt_ref[...] = jnp.broadcast_to(carry_ref[...], cnt_ref.shape)


def _outproj_router(z, x2, w_out, gt1, g2, sc2, sh2, wr_hi, wr_lo, br, rows_per_mod, tm=256):
    n, d = z.shape
    tiles_per_mod = rows_per_mod // tm
    row = lambda i: (i, 0)
    fixed = lambda i: (0, 0)
    modi = lambda i: (i // tiles_per_mod, 0, 0)
    return pl.pallas_call(
        _outproj_router_kernel,
        grid=(n // tm,),
        in_specs=[pl.BlockSpec((tm, d), row),
                  pl.BlockSpec((tm, d), row),
                  pl.BlockSpec((d, d), fixed),
                  pl.BlockSpec((1, 1, d), modi),
                  pl.BlockSpec((1, d), fixed),
                  pl.BlockSpec((1, 1, d), modi),
                  pl.BlockSpec((1, 1, d), modi),
                  pl.BlockSpec((d, LANES), fixed),
                  pl.BlockSpec((d, LANES), fixed),
                  pl.BlockSpec((1, LANES), fixed)],
        out_specs=[pl.BlockSpec((tm, d), row),
                   pl.BlockSpec((tm, d), row),
                   pl.BlockSpec((tm, LANES), row),
                   pl.BlockSpec((tm, LANES), row),
                   pl.BlockSpec((tm, LANES), row),
                   pl.BlockSpec((8, LANES), fixed)],
        out_shape=[jax.ShapeDtypeStruct((n, d), jnp.float32),
                   jax.ShapeDtypeStruct((n, d), jnp.float32),
                   jax.ShapeDtypeStruct((n, LANES), jnp.int32),
                   jax.ShapeDtypeStruct((n, LANES), jnp.float32),
                   jax.ShapeDtypeStruct((n, LANES), jnp.int32),
                   jax.ShapeDtypeStruct((8, LANES), jnp.float32)],
        scratch_shapes=[pltpu.VMEM((1, LANES), jnp.float32)],
        compiler_params=_cparams(("arbitrary",)),
        name="outproj_router",
    )(z, x2, w_out, gt1, g2, sc2, sh2, wr_hi, wr_lo, br)


def _start_zero_rows(zbuf, dst_rows, start, count, sem, wait):
    n_full = count // EXPERT_ROWS
    rem = count % EXPERT_ROWS
    pieces = [(i < n_full, start + i * EXPERT_ROWS, EXPERT_ROWS) for i in range(3)]
    off = start + n_full * EXPERT_ROWS
    p = EXPERT_ROWS // 2
    while p >= ROW_ALIGN:
        pieces.append(((rem & p) != 0, off + (rem // (2 * p)) * (2 * p), p))
        p //= 2
    for cond, row, size in pieces:
        @pl.when(cond)
        def _(row=row, size=size):
            cp = pltpu.make_async_copy(zbuf.at[pl.ds(0, size)], dst_rows(pl.multiple_of(row, ROW_ALIGN), size), sem)
            cp.wait() if wait else cp.start()


def _zero_rows(zbuf, dst_rows, start, count, sem):
    _start_zero_rows(zbuf, dst_rows, start, count, sem, wait=False)
    _start_zero_rows(zbuf, dst_rows, start, count, sem, wait=True)


def _dispatch_kernel(slot_ref, gap_ref, tail_ref, u_ref, xs_ref, zbuf, sem, zsem):
    tm = u_ref.shape[0]
    n_rows = xs_ref.shape[0]
    base = pl.program_id(0) * (tm * TOP_K)

    @pl.when(pl.program_id(0) == 0)
    def _():
        zbuf[...] = jnp.zeros_like(zbuf)
        for wait in (False, True):
            for e in range(N_EXPERTS):
                @pl.when(gap_ref[e] >= 0)
                def _(e=e, wait=wait):
                    cp = pltpu.make_async_copy(
                        zbuf.at[pl.ds(0, ROW_ALIGN)],
                        xs_ref.at[pl.ds(pl.multiple_of(gap_ref[e], ROW_ALIGN), ROW_ALIGN)], zsem)
                    cp.wait() if wait else cp.start()
        _zero_rows(zbuf, lambda r, s: xs_ref.at[pl.ds(r, s)], tail_ref[0], n_rows - tail_ref[0], zsem)

    def issue(r, carry):
        for k in range(TOP_K):
            s = slot_ref[base + r * TOP_K + k]
            pltpu.make_async_copy(u_ref.at[pl.ds(r, 1)], xs_ref.at[pl.ds(s, 1)], sem).start(priority=k % 2)
        return carry

    lax.fori_loop(0, tm, issue, 0)

    def drain(r, carry):
        for k in range(TOP_K):
            pltpu.make_async_copy(u_ref.at[pl.ds(0, 1)], xs_ref.at[pl.ds(0, 1)], sem).wait()
        return carry

    lax.fori_loop(0, tm, drain, 0)


def _dispatch(slots_flat, gap_rows, tail, u2, n_rows, tm=256):
    n, d = u2.shape
    return pl.pallas_call(
        _dispatch_kernel,
        grid_spec=pltpu.PrefetchScalarGridSpec(
            num_scalar_prefetch=3,
            grid=(n // tm,),
            in_specs=[pl.BlockSpec((tm, d), lambda i, *_: (i, 0))],
            out_specs=pl.BlockSpec(memory_space=pl.ANY),
            scratch_shapes=[pltpu.VMEM((EXPERT_ROWS, d), jnp.float32),
                            pltpu.SemaphoreType.DMA(()), pltpu.SemaphoreType.DMA(())]),
        out_shape=jax.ShapeDtypeStruct((n_rows, d), jnp.float32),
        compiler_params=_cparams(("arbitrary",)),
        name="moe_dispatch",
    )(slots_flat, gap_rows, tail, u2)


class _CopyGroup:
    def __init__(self, copies):
        self.copies = copies

    def start(self, priority=0):
        for cp in self.copies:
            cp.start(priority=priority)

    def wait(self):
        for cp in self.copies:
            cp.wait()


def _expert_rows_loop(start_ref, nblk_ref, tail_ref, order_ref, src_ref, dst_ref, ibuf, obuf, zbuf,
                      sem_in, sem_out, sem_z, prepare, compute):
    e = pl.program_id(0)
    f = pl.program_id(1)
    n_f = pl.num_programs(1)
    tile = obuf.shape[2]
    col = pl.multiple_of(f * tile, tile)
    nb = nblk_ref[e]
    base = start_ref[e]
    g0 = order_ref[0, e] + f * nb
    nxt = order_ref[1, e]

    part = EXPERT_ROWS // BLOCK_DMA_PARTS

    def rows(first_row, j):
        return pl.ds(pl.multiple_of(first_row + j * part, ROW_ALIGN), part)

    def in_copy(first_row, slot):
        return _CopyGroup([pltpu.make_async_copy(src_ref.at[rows(first_row, j)],
                                                 ibuf.at[slot, pl.ds(j * part, part)], sem_in.at[slot])
                           for j in range(BLOCK_DMA_PARTS)])

    def out_copy(k, slot):
        return _CopyGroup([pltpu.make_async_copy(obuf.at[slot, pl.ds(j * part, part)],
                                                 dst_ref.at[rows(base + k * EXPERT_ROWS, j), pl.ds(col, tile)],
                                                 sem_out.at[slot])
                           for j in range(BLOCK_DMA_PARTS)])

    @pl.when((nb > 0) & (g0 == 0))
    def _():
        in_copy(base, 0).start(priority=ROW_DMA_PRIORITY)

    prepare()

    def body(k, carry):
        slot = lax.rem(g0 + k, 2)
        oslot = lax.rem(k, 2)

        @pl.when(k + 1 < nb)
        def _():
            in_copy(base + (k + 1) * EXPERT_ROWS, 1 - slot).start(priority=ROW_DMA_PRIORITY)

        @pl.when((k + 1 == nb) & (f + 1 < n_f))
        def _():
            in_copy(base, 1 - slot).start(priority=ROW_DMA_PRIORITY)

        @pl.when((k + 1 == nb) & (f + 1 == n_f) & (nxt >= 0))
        def _():
            in_copy(start_ref[jnp.maximum(nxt, 0)], 1 - slot).start(priority=ROW_DMA_PRIORITY)

        in_copy(base, slot).wait()

        @pl.when(k >= 2)
        def _():
            out_copy(k - 2, oslot).wait()

        obuf[oslot] = compute(ibuf[slot]).astype(obuf.dtype)
        out_copy(k, oslot).start()
        return carry

    lax.fori_loop(0, nb, body, 0)

    @pl.when(nb >= 2)
    def _():
        out_copy(nb - 2, lax.rem(nb, 2)).wait()

    @pl.when(nb >= 1)
    def _():
        out_copy(nb - 1, lax.rem(nb + 1, 2)).wait()

    @pl.when(e == pl.num_programs(0) - 1)
    def _():
        zbuf[...] = jnp.zeros_like(zbuf)
        _zero_rows(zbuf, lambda r, s: dst_ref.at[pl.ds(r, s), pl.ds(col, tile)],
                   tail_ref[1], dst_ref.shape[0] - tail_ref[1], sem_z)


def _convert_slabs(slab_refs, dst):
    rows = dst.shape[0] // len(slab_refs)
    for s, ref in enumerate(slab_refs):
        dst[s * rows:(s + 1) * rows, :] = ref[...].astype(dst.dtype)


def _expert_up_kernel(start_ref, nblk_ref, tail_ref, order_ref, xs_ref, *refs):
    wg_refs, refs = refs[:WEIGHT_DMA_PARTS], refs[WEIGHT_DMA_PARTS:]
    wu_refs, refs = refs[:WEIGHT_DMA_PARTS], refs[WEIGHT_DMA_PARTS:]
    bg_ref, bu_ref, hid_ref, wg_s, wu_s, ibuf, obuf, zbuf, sem_in, sem_out, sem_z = refs

    def prepare():
        _convert_slabs(wg_refs, wg_s)
        _convert_slabs(wu_refs, wu_s)

    def compute(xb):
        x = xb.astype(jnp.bfloat16)
        gate = jnp.dot(x, wg_s[...], preferred_element_type=jnp.float32) + bg_ref[...]
        up = jnp.dot(x, wu_s[...], preferred_element_type=jnp.float32) + bu_ref[...]
        gate = jnp.minimum(gate, SWIGLU_LIMIT)
        up = jnp.clip(up, -SWIGLU_LIMIT, SWIGLU_LIMIT)
        return (up + 1.0) * gate * jax.nn.sigmoid(SWIGLU_ALPHA * gate)

    _expert_rows_loop(start_ref, nblk_ref, tail_ref, order_ref, xs_ref, hid_ref, ibuf, obuf, zbuf,
                      sem_in, sem_out, sem_z, prepare, compute)


def _expert_up(meta, xs, w_gu, b_gu):
    n_rows, d = xs.shape
    tf = D_FF // UP_COL_TILES
    nfc = UP_COL_TILES
    b3 = b_gu.reshape(N_EXPERTS, 1, 2 * D_FF)
    slab = d // WEIGHT_DMA_PARTS
    gate_slabs = [pl.BlockSpec((None, slab, tf), functools.partial(lambda s, e, f, *_: (e, s, f), s))
                  for s in range(WEIGHT_DMA_PARTS)]
    up_slabs = [pl.BlockSpec((None, slab, tf), functools.partial(lambda s, e, f, *_: (e, s, nfc + f), s))
                for s in range(WEIGHT_DMA_PARTS)]
    return pl.pallas_call(
        _expert_up_kernel,
        grid_spec=pltpu.PrefetchScalarGridSpec(
            num_scalar_prefetch=4,
            grid=(N_EXPERTS, nfc),
            in_specs=[pl.BlockSpec(memory_space=pl.ANY)] + gate_slabs + up_slabs + [
                      pl.BlockSpec((None, 1, tf), lambda e, f, *_: (e, 0, f)),
                      pl.BlockSpec((None, 1, tf), lambda e, f, *_: (e, 0, nfc + f))],
            out_specs=pl.BlockSpec(memory_space=pl.ANY),
            scratch_shapes=[pltpu.VMEM((d, tf), jnp.bfloat16), pltpu.VMEM((d, tf), jnp.bfloat16),
                            pltpu.VMEM((2, EXPERT_ROWS, d), jnp.float32),
                            pltpu.VMEM((2, EXPERT_ROWS, tf), jnp.bfloat16),
                            pltpu.VMEM((EXPERT_ROWS, tf), jnp.bfloat16),
                            pltpu.SemaphoreType.DMA((2,)), pltpu.SemaphoreType.DMA((2,)),
                            pltpu.SemaphoreType.DMA(())]),
        out_shape=jax.ShapeDtypeStruct((n_rows, D_FF), jnp.bfloat16),
        compiler_params=_cparams(("arbitrary", "arbitrary")),
        name="expert_up",
    )(*meta, xs, *([w_gu] * (2 * WEIGHT_DMA_PARTS)), b3, b3)


def _expert_down_kernel(start_ref, nblk_ref, tail_ref, order_ref, hid_ref, *refs):
    wd_refs, refs = refs[:WEIGHT_DMA_PARTS], refs[WEIGHT_DMA_PARTS:]
    bd_ref, ys_ref, wd_s, ibuf, obuf, zbuf, sem_in, sem_out, sem_z = refs

    def prepare():
        _convert_slabs(wd_refs, wd_s)

    def compute(hb):
        return jnp.dot(hb, wd_s[...], preferred_element_type=jnp.float32) + bd_ref[...]

    _expert_rows_loop(start_ref, nblk_ref, tail_ref, order_ref, hid_ref, ys_ref, ibuf, obuf, zbuf,
                      sem_in, sem_out, sem_z, prepare, compute)


def _expert_down(meta, hid, w_dn, b_dn):
    n_rows, dff = hid.shape
    d = w_dn.shape[2]
    tn = d // DOWN_COL_TILES
    b3 = b_dn.reshape(N_EXPERTS, 1, d)
    slab = dff // WEIGHT_DMA_PARTS
    down_slabs = [pl.BlockSpec((None, slab, tn), functools.partial(lambda s, e, f, *_: (e, s, f), s))
                  for s in range(WEIGHT_DMA_PARTS)]
    return pl.pallas_call(
        _expert_down_kernel,
        grid_spec=pltpu.PrefetchScalarGridSpec(
            num_scalar_prefetch=4,
            grid=(N_EXPERTS, d // tn),
            in_specs=[pl.BlockSpec(memory_space=pl.ANY)] + down_slabs + [
                      pl.BlockSpec((None, 1, tn), lambda e, f, *_: (e, 0, f))],
            out_specs=pl.BlockSpec(memory_space=pl.ANY),
            scratch_shapes=[pltpu.VMEM((dff, tn), jnp.bfloat16),
                            pltpu.VMEM((2, EXPERT_ROWS, dff), jnp.bfloat16),
                            pltpu.VMEM((2, EXPERT_ROWS, tn), jnp.float32),
                            pltpu.VMEM((EXPERT_ROWS, tn), jnp.float32),
                            pltpu.SemaphoreType.DMA((2,)), pltpu.SemaphoreType.DMA((2,)),
                            pltpu.SemaphoreType.DMA(())]),
        out_shape=jax.ShapeDtypeStruct((n_rows, d), jnp.float32),
        compiler_params=_cparams(("arbitrary", "arbitrary")),
        name="expert_down",
    )(*meta, hid, *([w_dn] * WEIGHT_DMA_PARTS), b3)


def _row_layout(counts, n_assign):
    c_al = (counts + ROW_ALIGN - 1) // ROW_ALIGN * ROW_ALIGN
    start = jnp.cumsum(c_al) - c_al
    total = jnp.sum(c_al)
    nblk = (counts + EXPERT_ROWS - 1) // EXPERT_ROWS
    covered = jnp.max(start + nblk * EXPERT_ROWS)
    gap_rows = jnp.where(counts > 0, start + c_al - ROW_ALIGN, -1)
    n_rows = n_assign + N_EXPERTS * ROW_ALIGN + EXPERT_ROWS
    i32 = lambda a: a.astype(jnp.int32)
    tail = i32(jnp.stack([total, covered]))
    blocks_before = jnp.cumsum(nblk) - nblk
    ids = jnp.arange(N_EXPERTS)
    later = lax.cummin(jnp.where(nblk > 0, ids, N_EXPERTS), reverse=True)
    nxt = jnp.concatenate([later[1:], jnp.full((1,), N_EXPERTS)])
    nxt = jnp.where(nxt < N_EXPERTS, nxt, -1)

    def meta(col_tiles):
        return i32(start), i32(nblk), tail, i32(jnp.stack([col_tiles * blocks_before, nxt]))

    return meta(UP_COL_TILES), meta(DOWN_COL_TILES), i32(gap_rows), n_rows


def _combine_kernel(slot_ref, ys_ref, hx_ref, wgt_ref, gt_ref, o_ref, buf, sem):
    tm = hx_ref.shape[0]
    i = pl.program_id(0)
    n_i = pl.num_programs(0)

    def issue(tile, b):
        base = tile * (tm * TOP_K)

        def body(r, carry):
            for k in range(TOP_K):
                s = slot_ref[base + r * TOP_K + k]
                pltpu.make_async_copy(ys_ref.at[pl.ds(s, 1)], buf.at[b, k, pl.ds(r, 1)],
                                      sem.at[b]).start(priority=k % 2)
            return carry

        lax.fori_loop(0, tm, body, 0)

    @pl.when(i == 0)
    def _():
        issue(0, 0)

    @pl.when(i + 1 < n_i)
    def _():
        issue(i + 1, (i + 1) % 2)

    cur = i % 2

    def drain(r, carry):
        for k in range(TOP_K):
            pltpu.make_async_copy(ys_ref.at[pl.ds(0, 1)], buf.at[cur, k, pl.ds(0, 1)], sem.at[cur]).wait()
        return carry

    lax.fori_loop(0, tm, drain, 0)

    wgt = wgt_ref[...]
    acc = wgt[:, 0:1] * buf[cur, 0]
    for k in range(1, TOP_K):
        acc = acc + wgt[:, k:k + 1] * buf[cur, k]
    o_ref[...] = hx_ref[...] + gt_ref[0] * acc


def _combine(slots_flat, ys, hx, wgt, gt2, rows_per_mod, tm=128):
    n, d = hx.shape
    tiles_per_mod = rows_per_mod // tm
    return pl.pallas_call(
        _combine_kernel,
        grid_spec=pltpu.PrefetchScalarGridSpec(
            num_scalar_prefetch=1,
            grid=(n // tm,),
            in_specs=[pl.BlockSpec(memory_space=pl.ANY),
                      pl.BlockSpec((tm, d), lambda i, s: (i, 0)),
                      pl.BlockSpec((tm, LANES), lambda i, s: (i, 0)),
                      pl.BlockSpec((1, 1, d), lambda i, s: (i // tiles_per_mod, 0, 0))],
            out_specs=pl.BlockSpec((tm, d), lambda i, s: (i, 0)),
            scratch_shapes=[pltpu.VMEM((2, TOP_K, tm, d), jnp.float32),
                            pltpu.SemaphoreType.DMA((2,))]),
        out_shape=jax.ShapeDtypeStruct((n, d), jnp.float32),
        compiler_params=_cparams(("arbitrary",)),
        name="moe_combine",
    )(slots_flat, ys, hx, wgt, gt2)


def _rope_tables(n_t):
    rows = n_t // GRID_W
    row_ids = jnp.repeat(jnp.arange(rows), GRID_W).astype(jnp.float32)
    col_ids = jnp.tile(jnp.arange(GRID_W), rows).astype(jnp.float32)
    freqs = jnp.exp(-math.log(ROPE_THETA) * jnp.arange(ROPE_PAIRS_AXIS, dtype=jnp.float32) / ROPE_PAIRS_AXIS)
    ang = jnp.concatenate([row_ids[:, None] * freqs, col_ids[:, None] * freqs], axis=-1)
    cos_e = jnp.repeat(jnp.cos(ang), 2, axis=-1)
    sin = jnp.sin(ang)
    sin_s = jnp.stack([-sin, sin], axis=-1).reshape(n_t, HEAD_DIM)
    return cos_e, sin_s


def _gate_layouts(og, n_b, n_t):
    g = og[:, :4 * M_HEADS].reshape(n_b, n_t, 4, M_HEADS)
    return g.transpose(0, 3, 2, 1), g.transpose(0, 3, 1, 2)


def _layer(x, c, ctx, c_ctx, w_mod, b_mod, g_norm1, g_norm2, w_in, b_in, g_q, g_k, g_mh,
           w_br_m, w_br_a, w_out, w_router, b_router, w_gu, b_gu, w_dn, b_dn):
    n_b, n_t, d = x.shape
    n_c = ctx.shape[1]
    bf = jnp.bfloat16

    b_in2 = b_in.reshape(1, _F_IN)
    wr = jnp.pad(w_router, ((0, 0), (0, LANES - N_EXPERTS)))
    wr_hi = wr.astype(bf)
    wr_lo = (wr - wr_hi.astype(jnp.float32)).astype(bf)
    br = jnp.pad(b_router, (0, LANES - N_EXPERTS)).reshape(1, LANES)

    c8 = jnp.zeros((8, d), jnp.float32).at[:n_b].set(c).at[n_b].set(c_ctx)
    mod = _modulation(c8, w_mod, b_mod)
    mod6 = mod.reshape(8, 6, d)
    sh1x, sc1x, gt1x, sh2x, sc2x, gt2x = [mod6[:n_b, i].reshape(n_b, 1, d) for i in range(6)]
    sh1c, sc1c = [mod6[n_b:n_b + 1, i].reshape(1, 1, d) for i in range(2)]

    g1 = g_norm1.reshape(1, d)
    x2 = x.reshape(n_b * n_t, d)
    w_in_t = w_in.T
    b_gate = b_in[_O_GATES:_O_AQ].reshape(1, GATE_COLS)
    ux, ogx = _adaln_norm(x2, g1, sc1x, sh1x, w_in_t, b_gate, rows_per_mod=n_t)
    uc, ogc = _adaln_norm(ctx.reshape(n_b * n_c, d), g1, sc1c, sh1c, w_in_t, b_gate, rows_per_mod=n_b * n_c)
    px2 = _in_projection(ux, w_in_t, b_in2, jnp.arange(_P_COLS // PROJ_TN, dtype=jnp.int32))
    ctx_tiles = jnp.array([_P_MK // PROJ_TN, _P_MV // PROJ_TN, _P_MV // PROJ_TN + 1, _P_AK // PROJ_TN], jnp.int32)
    pc2 = _in_projection(uc, w_in_t, b_in2, ctx_tiles)
    px = px2.reshape(n_b, n_t, _P_COLS)
    pc = pc2.reshape(n_b, n_c, _C_COLS)

    cos_e, sin_s = _rope_tables(n_t)
    a_out = _attention(px, pc, cos_e, sin_s, g_q.reshape(1, HEAD_DIM), g_k.reshape(1, HEAD_DIM))

    g_rows, g_cols = _gate_layouts(ogx, n_b, n_t)
    g_rows_c, g_cols_c = _gate_layouts(ogc, n_b, n_c)
    h_fwd = _mlstm(px, pc, g_rows, g_cols, g_rows_c, g_cols_c, backward=False)
    m_out = _mlstm(px, pc, g_rows, g_cols, g_rows_c, g_cols_c, backward=True,
                   h_fwd=h_fwd, g_mh=g_mh.reshape(1, M_HEADS * MV_DIM))

    n = n_b * n_t
    z = _merge(m_out.reshape(n, d), a_out.reshape(n, d), px2, w_br_m.astype(bf), w_br_a.astype(bf))
    hx, u2, top_idx, top_w, rank, cnt = _outproj_router(
        z, x2, w_out.astype(bf), gt1x, g_norm2.reshape(1, d), sc2x, sh2x, wr_hi, wr_lo, br, rows_per_mod=n_t)

    counts = cnt[0, :N_EXPERTS].astype(jnp.int32)
    meta_up, meta_dn, gap_rows, n_rows = _row_layout(counts, n * TOP_K)
    slots = (meta_up[0][top_idx[:, :TOP_K]] + rank[:, :TOP_K]).astype(jnp.int32).reshape(n * TOP_K)

    xs = _dispatch(slots, gap_rows, meta_up[2], u2, n_rows)
    hid = _expert_up(meta_up, xs, w_gu, b_gu)
    ys = _expert_down(meta_dn, hid, w_dn, b_dn)
    out = _combine(slots, ys, hx, top_w, gt2x, rows_per_mod=n_t)
    return out.reshape(n_b, n_t, d)


def kernel(x, c, ctx, c_ctx, w_mod, b_mod, g_norm1, g_norm2, w_in, b_in, g_q, g_k, g_mh, w_br_m, w_br_a, w_out,
           w_router, b_router, w_gu, b_gu, w_dn, b_dn):
    assert w_mod.shape[0] == 1, "single layer: the context stream has no consumer after it"
    return _layer(x, c, ctx, c_ctx, w_mod[0], b_mod[0], g_norm1[0], g_norm2[0], w_in[0], b_in[0], g_q[0], g_k[0],
                  g_mh[0], w_br_m[0], w_br_a[0], w_out[0], w_router[0], b_router[0], w_gu[0], b_gu[0],
                  w_dn[0], b_dn[0])
```

```python
import functools
import math

import jax
import jax.numpy as jnp
from jax import lax
from jax.experimental import pallas as pl
from jax.experimental.pallas import tpu as pltpu

D_MODEL = 2048
GRID_W = 64
HEAD_DIM = 128
A_HEADS = 16
KV_HEADS = 4
Q_GROUP = A_HEADS // KV_HEADS
ROPE_THETA = 10000.0
ROPE_PAIRS_AXIS = HEAD_DIM // 4
M_HEADS = 4
MV_DIM = D_MODEL // M_HEADS
MQK_DIM = MV_DIM // 2
N_EXPERTS = 32
TOP_K = 4
D_FF = D_MODEL
SWIGLU_LIMIT = 7.0
SWIGLU_ALPHA = 1.702
EPS = 1e-6

_O_MQ, _O_MK, _O_MV, _O_MO = 0, 1024, 2048, 4096
_O_GATES = 6144
GATE_COLS = 4 * M_HEADS
_O_AQ, _O_AK, _O_AV, _O_MG = 6160, 8208, 8720, 9232
_F_IN = 13328
_P_MQ, _P_MK, _P_MV, _P_MO, _P_AQ, _P_AK, _P_AV, _P_GM, _P_GA = (
    0, 1024, 2048, 4096, 6144, 8192, 8704, 9216, 11264)
_P_COLS = 13312
_C_MK, _C_MV, _C_AK, _C_AV = 0, 1024, 3072, 3584
_C_COLS = 4096

LANES = 128
M_CHUNK = 256
EXPERT_ROWS = 256
ROW_ALIGN = 16
BLOCK_DMA_PARTS = 4
WEIGHT_DMA_PARTS = 8
ROW_DMA_PRIORITY = 1
UP_COL_TILES = 2
DOWN_COL_TILES = 1
NEG_BIG = -1e30
VMEM_LIMIT = 56 * 1024 * 1024

_HI = lax.Precision.HIGHEST


def _cparams(sem, vmem=VMEM_LIMIT):
    return pltpu.CompilerParams(dimension_semantics=sem, vmem_limit_bytes=vmem)


def _mod_kernel(c_ref, w_ref, b_ref, o_ref):
    c = c_ref[...]
    a = c * jax.nn.sigmoid(c)
    o_ref[...] = lax.dot_general(a, w_ref[...], (((1,), (0,)), ((), ())), precision=_HI,
                                 preferred_element_type=jnp.float32) + b_ref[...]


def _modulation(c8, w_mod, b_mod):
    d, n = w_mod.shape
    tn = 1024
    return pl.pallas_call(
        _mod_kernel,
        grid=(n // tn,),
        in_specs=[pl.BlockSpec((8, d), lambda j: (0, 0)),
                  pl.BlockSpec((d, tn), lambda j: (0, j)),
                  pl.BlockSpec((1, tn), lambda j: (0, j))],
        out_specs=pl.BlockSpec((8, tn), lambda j: (0, j)),
        out_shape=jax.ShapeDtypeStruct((8, n), jnp.float32),
        compiler_params=_cparams(("arbitrary",)),
        name="modulation",
    )(c8, w_mod, b_mod.reshape(1, n))


def _adaln_kernel(x_ref, g_ref, sc_ref, sh_ref, wg_ref, bg_ref, u_ref, og_ref):
    x = x_ref[...]
    y = x * lax.rsqrt(jnp.mean(x * x, axis=-1, keepdims=True) + EPS) * g_ref[...]
    u = y * (1.0 + sc_ref[0]) + sh_ref[0]
    u_ref[...] = u.astype(jnp.bfloat16)
    og_ref[...] = lax.dot_general(u, wg_ref[...], (((1,), (1,)), ((), ())), precision=_HI,
                                  preferred_element_type=jnp.float32) + bg_ref[...]


def _adaln_norm(x2, g, sc, sh, w_in_t, b_gate, rows_per_mod, tm=512):
    n, d = x2.shape
    tiles_per_mod = rows_per_mod // tm
    gate_blk = _O_GATES // GATE_COLS
    return pl.pallas_call(
        _adaln_kernel,
        grid=(n // tm,),
        in_specs=[pl.BlockSpec((tm, d), lambda i: (i, 0)),
                  pl.BlockSpec((1, d), lambda i: (0, 0)),
                  pl.BlockSpec((1, 1, d), lambda i: (i // tiles_per_mod, 0, 0)),
                  pl.BlockSpec((1, 1, d), lambda i: (i // tiles_per_mod, 0, 0)),
                  pl.BlockSpec((GATE_COLS, d), lambda i: (gate_blk, 0)),
                  pl.BlockSpec((1, GATE_COLS), lambda i: (0, 0))],
        out_specs=[pl.BlockSpec((tm, d), lambda i: (i, 0)),
                   pl.BlockSpec((tm, GATE_COLS), lambda i: (i, 0))],
        out_shape=[jax.ShapeDtypeStruct((n, d), jnp.bfloat16),
                   jax.ShapeDtypeStruct((n, GATE_COLS), jnp.float32)],
        compiler_params=_cparams(("arbitrary",)),
        name="adaln_norm",
    )(x2, g, sc, sh, w_in_t, b_gate)


PROJ_TN = 1024


def _proj_kernel(tiles_ref, u_ref, w_ref, wx_ref, b_ref, bx_ref, o_ref, w_s, b_s):
    t = tiles_ref[pl.program_id(0)]
    shifted = t * PROJ_TN >= _O_GATES
    keep = PROJ_TN - GATE_COLS

    @pl.when((pl.program_id(1) == 0) & jnp.logical_not(shifted))
    def _():
        w_s[...] = w_ref[...].astype(jnp.bfloat16)
        b_s[...] = b_ref[...]

    @pl.when((pl.program_id(1) == 0) & shifted)
    def _():
        w_s[0:keep, :] = w_ref[GATE_COLS:, :].astype(jnp.bfloat16)
        w_s[keep:, :] = wx_ref[...].astype(jnp.bfloat16)
        b_s[...] = jnp.concatenate([b_ref[:, GATE_COLS:], bx_ref[:, :GATE_COLS]], axis=1)

    acc = lax.dot_general(u_ref[...], w_s[...], (((1,), (1,)), ((), ())), preferred_element_type=jnp.float32)
    o_ref[...] = (acc + b_s[...]).astype(o_ref.dtype)


def _in_projection(u, w_in_t, b_in2, tiles, tm=1024):
    n, d = u.shape
    n_tiles = tiles.shape[0]
    tn = PROJ_TN
    return pl.pallas_call(
        _proj_kernel,
        grid_spec=pltpu.PrefetchScalarGridSpec(
            num_scalar_prefetch=1,
            grid=(n_tiles, n // tm),
            in_specs=[pl.BlockSpec((tm, d), lambda j, i, t: (i, 0)),
                      pl.BlockSpec((tn, d), lambda j, i, t: (t[j], 0)),
                      pl.BlockSpec((GATE_COLS, d), lambda j, i, t: ((tn // GATE_COLS) * (t[j] + 1), 0)),
                      pl.BlockSpec((1, tn), lambda j, i, t: (0, t[j])),
                      pl.BlockSpec((1, LANES), lambda j, i, t: (0, (tn // LANES) * (t[j] + 1)))],
            out_specs=pl.BlockSpec((tm, tn), lambda j, i, t: (i, j)),
            scratch_shapes=[pltpu.VMEM((tn, d), jnp.bfloat16), pltpu.VMEM((1, tn), jnp.float32)]),
        out_shape=jax.ShapeDtypeStruct((n, n_tiles * tn), jnp.bfloat16),
        compiler_params=_cparams(("arbitrary", "arbitrary")),
        name="in_projection",
    )(tiles, u, w_in_t, w_in_t, b_in2, b_in2)


def _rms_head(x, g):
    return x * lax.rsqrt(jnp.mean(x * x, axis=-1, keepdims=True) + EPS) * g


def _rope(x, cos_e, sin_s):
    lane = lax.broadcasted_iota(jnp.int32, x.shape, 1)
    swapped = jnp.where(lane % 2 == 0, pltpu.roll(x, LANES - 1, 1), pltpu.roll(x, 1, 1))
    return x * cos_e + swapped * sin_s


def _attn_kernel(q_ref, kx_ref, vx_ref, kc_ref, vc_ref, cosq_ref, sinq_ref, cosk_ref, sink_ref,
                 gq_ref, gk_ref, o_ref, k_s, v_s):
    n_t = kx_ref.shape[1]

    @pl.when(pl.program_id(2) == 0)
    def _():
        kx = _rms_head(kx_ref[0].astype(jnp.float32), gk_ref[...])
        k_s[0:n_t, :] = _rope(kx, cosk_ref[...], sink_ref[...]).astype(jnp.bfloat16)
        k_s[n_t:, :] = _rms_head(kc_ref[0].astype(jnp.float32), gk_ref[...]).astype(jnp.bfloat16)
        v_s[0:n_t, :] = vx_ref[0]
        v_s[n_t:, :] = vc_ref[0]

    scale = HEAD_DIM ** -0.5
    kk = k_s[...]
    vv = v_s[...]
    for g in range(Q_GROUP):
        q = q_ref[0, :, g * HEAD_DIM:(g + 1) * HEAD_DIM].astype(jnp.float32)
        q = _rope(_rms_head(q, gq_ref[...]), cosq_ref[...], sinq_ref[...]) * scale
        s = lax.dot_general(q.astype(jnp.bfloat16), kk, (((1,), (1,)), ((), ())),
                            preferred_element_type=jnp.float32)
        p = jnp.exp(s - jnp.max(s, axis=-1, keepdims=True))
        l = jnp.sum(p, axis=-1, keepdims=True)
        o = jnp.dot(p.astype(jnp.bfloat16), vv, preferred_element_type=jnp.float32) / l
        o_ref[0, :, g * HEAD_DIM:(g + 1) * HEAD_DIM] = o.astype(o_ref.dtype)


def _attention(px, pc, cos_e, sin_s, g_q, g_k, tq=256):
    n_b, n_t, _ = px.shape
    n_c = pc.shape[1]
    gw = Q_GROUP * HEAD_DIM
    return pl.pallas_call(
        _attn_kernel,
        grid=(n_b, KV_HEADS, n_t // tq),
        in_specs=[pl.BlockSpec((1, tq, gw), lambda b, h, i: (b, i, _P_AQ // gw + h)),
                  pl.BlockSpec((1, n_t, HEAD_DIM), lambda b, h, i: (b, 0, _P_AK // HEAD_DIM + h)),
                  pl.BlockSpec((1, n_t, HEAD_DIM), lambda b, h, i: (b, 0, _P_AV // HEAD_DIM + h)),
                  pl.BlockSpec((1, n_c, HEAD_DIM), lambda b, h, i: (b, 0, _C_AK // HEAD_DIM + h)),
                  pl.BlockSpec((1, n_c, HEAD_DIM), lambda b, h, i: (b, 0, _C_AV // HEAD_DIM + h)),
                  pl.BlockSpec((tq, HEAD_DIM), lambda b, h, i: (i, 0)),
                  pl.BlockSpec((tq, HEAD_DIM), lambda b, h, i: (i, 0)),
                  pl.BlockSpec((n_t, HEAD_DIM), lambda b, h, i: (0, 0)),
                  pl.BlockSpec((n_t, HEAD_DIM), lambda b, h, i: (0, 0)),
                  pl.BlockSpec((1, HEAD_DIM), lambda b, h, i: (0, 0)),
                  pl.BlockSpec((1, HEAD_DIM), lambda b, h, i: (0, 0))],
        out_specs=pl.BlockSpec((1, tq, gw), lambda b, h, i: (b, i, h)),
        out_shape=jax.ShapeDtypeStruct((n_b, n_t, A_HEADS * HEAD_DIM), jnp.bfloat16),
        scratch_shapes=[pltpu.VMEM((n_t + n_c, HEAD_DIM), jnp.bfloat16),
                        pltpu.VMEM((n_t + n_c, HEAD_DIM), jnp.bfloat16)],
        compiler_params=_cparams(("arbitrary", "arbitrary", "arbitrary")),
        name="attention",
    )(px, px, px, pc, pc, cos_e, sin_s, cos_e, sin_s, g_q, g_k)


def _log_sigmoid(x):
    return jnp.minimum(x, 0.0) - jnp.log(1.0 + jnp.exp(-jnp.abs(x)))


def _mlstm_chunk(q, k, v, i_row, f_row, i_col, f_col, ct_ref, n_ref, m_ref, backward, want_h):
    n_l = k.shape[0]
    lf_row = _log_sigmoid(f_row)
    lf_col = _log_sigmoid(f_col)
    t_idx = lax.broadcasted_iota(jnp.int32, (n_l, n_l), 0)
    s_idx = lax.broadcasted_iota(jnp.int32, (n_l, n_l), 1)
    seen = (s_idx >= t_idx) if backward else (s_idx <= t_idx)
    b_col = jnp.sum(jnp.where(seen, lf_row, 0.0), axis=1, keepdims=True)
    seen_t = (t_idx >= s_idx) if backward else (t_idx <= s_idx)
    b_row = jnp.sum(jnp.where(seen_t, lf_col, 0.0), axis=0, keepdims=True)
    b_end = jnp.sum(lf_row, axis=1, keepdims=True)
    m_prev = m_ref[...]
    ct = ct_ref[...]
    n_vec = n_ref[...]
    kf = k.astype(jnp.float32)

    h = None
    if want_h:
        d_intra = jnp.where(seen, b_col - b_row + i_row, NEG_BIG)
        d_inter = b_col + m_prev
        m_t = jnp.maximum(d_inter, jnp.max(d_intra, axis=1, keepdims=True))
        qk = lax.dot_general(q, k, (((1,), (1,)), ((), ())), preferred_element_type=jnp.float32)
        s = qk * jnp.exp(d_intra - m_t)
        w_inter = jnp.exp(d_inter - m_t)
        num = (jnp.dot(s.astype(jnp.bfloat16), v, preferred_element_type=jnp.float32)
               + w_inter * jnp.dot(q, ct.astype(jnp.bfloat16), preferred_element_type=jnp.float32))
        den = (jnp.sum(s, axis=1, keepdims=True)
               + w_inter * jnp.sum(q.astype(jnp.float32) * n_vec, axis=1, keepdims=True))
        h = num / jnp.maximum(jnp.abs(den), jnp.exp(-m_t))

    g_col = b_end - b_col + i_col
    m_new = jnp.maximum(b_end + m_prev, jnp.max(g_col, axis=0, keepdims=True))
    w_s = jnp.exp(g_col - m_new)
    w_c = jnp.exp(b_end + m_prev - m_new)
    wv = (w_s * v.astype(jnp.float32)).astype(jnp.bfloat16)
    ct_ref[...] = w_c * ct + lax.dot_general(k, wv, (((0,), (0,)), ((), ())),
                                             preferred_element_type=jnp.float32)
    n_ref[...] = w_c * n_vec + jnp.sum(w_s * kf, axis=0, keepdims=True)
    m_ref[...] = m_new
    return h


def _mlstm_kernel(backward, *refs):
    if backward:
        (q_ref, k_ref, v_ref, kc_ref, vc_ref, gr_ref, gc_ref, grc_ref, gcc_ref,
         hf_ref, op_ref, gmh_ref, o_ref, ct_ref, n_ref, m_ref) = refs
    else:
        (q_ref, k_ref, v_ref, kc_ref, vc_ref, gr_ref, gc_ref, grc_ref, gcc_ref,
         o_ref, ct_ref, n_ref, m_ref) = refs
    gi, gf = (2, 3) if backward else (0, 1)
    step = pl.program_id(2)

    @pl.when(step == 0)
    def _():
        ct_ref[...] = jnp.zeros_like(ct_ref)
        n_ref[...] = jnp.zeros_like(n_ref)
        m_ref[...] = jnp.zeros_like(m_ref)
        _mlstm_chunk(None, kc_ref[0], vc_ref[0],
                     grc_ref[0, 0, gi:gi + 1, :], grc_ref[0, 0, gf:gf + 1, :],
                     gcc_ref[0, 0, :, gi:gi + 1], gcc_ref[0, 0, :, gf:gf + 1],
                     ct_ref, n_ref, m_ref, backward, False)

    @pl.when(step > 0)
    def _():
        q = (q_ref[0].astype(jnp.float32) * (MQK_DIM ** -0.5)).astype(jnp.bfloat16)
        h = _mlstm_chunk(q, k_ref[0], v_ref[0],
                         gr_ref[0, 0, gi:gi + 1, :], gr_ref[0, 0, gf:gf + 1, :],
                         gc_ref[0, 0, :, gi:gi + 1], gc_ref[0, 0, :, gf:gf + 1],
                         ct_ref, n_ref, m_ref, backward, True)
        if backward:
            ht = h + hf_ref[0]
            y = ht * lax.rsqrt(jnp.mean(ht * ht, axis=-1, keepdims=True) + EPS) * gmh_ref[...]
            o_ref[0] = (y * jax.nn.sigmoid(op_ref[0].astype(jnp.float32))).astype(o_ref.dtype)
        else:
            o_ref[0] = h


def _mlstm(px, pc, g_rows, g_cols, g_rows_c, g_cols_c, backward, h_fwd=None, g_mh=None):
    n_b, n_t, _ = px.shape
    n_c = pc.shape[1]
    n_l = M_CHUNK
    assert n_c == n_l and n_t % n_l == 0
    n_chunk = n_t // n_l

    if backward:
        def cidx(s):
            return jnp.minimum(n_chunk - s, n_chunk - 1)
    else:
        def cidx(s):
            return jnp.maximum(s - 1, 0)

    in_specs = [
        pl.BlockSpec((1, n_l, MQK_DIM), lambda b, h, s: (b, cidx(s), _P_MQ // MQK_DIM + h)),
        pl.BlockSpec((1, n_l, MQK_DIM), lambda b, h, s: (b, cidx(s), _P_MK // MQK_DIM + h)),
        pl.BlockSpec((1, n_l, MV_DIM), lambda b, h, s: (b, cidx(s), _P_MV // MV_DIM + h)),
        pl.BlockSpec((1, n_c, MQK_DIM), lambda b, h, s: (b, 0, _C_MK // MQK_DIM + h)),
        pl.BlockSpec((1, n_c, MV_DIM), lambda b, h, s: (b, 0, _C_MV // MV_DIM + h)),
        pl.BlockSpec((1, 1, 4, n_l), lambda b, h, s: (b, h, 0, cidx(s))),
        pl.BlockSpec((1, 1, n_l, 4), lambda b, h, s: (b, h, cidx(s), 0)),
        pl.BlockSpec((1, 1, 4, n_c), lambda b, h, s: (b, h, 0, 0)),
        pl.BlockSpec((1, 1, n_c, 4), lambda b, h, s: (b, h, 0, 0)),
    ]
    args = [px, px, px, pc, pc, g_rows, g_cols, g_rows_c, g_cols_c]
    if backward:
        in_specs += [
            pl.BlockSpec((1, n_l, MV_DIM), lambda b, h, s: (b, cidx(s), h)),
            pl.BlockSpec((1, n_l, MV_DIM), lambda b, h, s: (b, cidx(s), _P_MO // MV_DIM + h)),
            pl.BlockSpec((1, MV_DIM), lambda b, h, s: (0, h)),
        ]
        args += [h_fwd, px, g_mh]
        out_dtype = jnp.bfloat16
    else:
        out_dtype = jnp.float32
    return pl.pallas_call(
        functools.partial(_mlstm_kernel, backward),
        grid=(n_b, M_HEADS, n_chunk + 1),
        in_specs=in_specs,
        out_specs=pl.BlockSpec((1, n_l, MV_DIM), lambda b, h, s: (b, cidx(s), h)),
        out_shape=jax.ShapeDtypeStruct((n_b, n_t, M_HEADS * MV_DIM), out_dtype),
        scratch_shapes=[pltpu.VMEM((MQK_DIM, MV_DIM), jnp.float32),
                        pltpu.VMEM((1, MQK_DIM), jnp.float32),
                        pltpu.VMEM((1, 1), jnp.float32)],
        compiler_params=_cparams(("arbitrary", "arbitrary", "arbitrary")),
        name="mlstm_bwd" if backward else "mlstm_fwd",
    )(*args)


def _merge_kernel(m_ref, a_ref, gm_ref, ga_ref, wm_ref, wa_ref, o_ref):
    zm = jnp.dot(m_ref[...], wm_ref[...], preferred_element_type=jnp.float32)
    za = jnp.dot(a_ref[...], wa_ref[...], preferred_element_type=jnp.float32)
    z = (jax.nn.sigmoid(gm_ref[...].astype(jnp.float32)) * zm
         + jax.nn.sigmoid(ga_ref[...].astype(jnp.float32)) * za)
    o_ref[...] = z.astype(o_ref.dtype)


def _merge(m_out, a_out, px2, w_br_m, w_br_a, tm=1024, tn=512):
    n, d = m_out.shape
    return pl.pallas_call(
        _merge_kernel,
        grid=(n // tm, d // tn),
        in_specs=[pl.BlockSpec((tm, d), lambda i, j: (i, 0)),
                  pl.BlockSpec((tm, d), lambda i, j: (i, 0)),
                  pl.BlockSpec((tm, tn), lambda i, j: (i, _P_GM // tn + j)),
                  pl.BlockSpec((tm, tn), lambda i, j: (i, _P_GA // tn + j)),
                  pl.BlockSpec((d, tn), lambda i, j: (0, j)),
                  pl.BlockSpec((d, tn), lambda i, j: (0, j))],
        out_specs=pl.BlockSpec((tm, tn), lambda i, j: (i, j)),
        out_shape=jax.ShapeDtypeStruct((n, d), jnp.bfloat16),
        compiler_params=_cparams(("arbitrary", "arbitrary")),
        name="merge",
    )(m_out, a_out, px2, px2, w_br_m, w_br_a)


def _split_bf16(x):
    hi = x.astype(jnp.bfloat16)
    lo = (x - hi.astype(jnp.float32)).astype(jnp.bfloat16)
    return hi, lo


def _outproj_router_kernel(z_ref, x_ref, wo_ref, gt_ref, g2_ref, sc_ref, sh_ref, wrh_ref, wrl_ref, br_ref,
                           hx_ref, u2_ref, idx_ref, wgt_ref, rank_ref, cnt_ref, carry_ref):
    tm = z_ref.shape[0]

    @pl.when(pl.program_id(0) == 0)
    def _():
        carry_ref[...] = jnp.zeros_like(carry_ref)

    y = jnp.dot(z_ref[...], wo_ref[...], preferred_element_type=jnp.float32)
    hx = x_ref[...] + gt_ref[0] * y
    hx_ref[...] = hx
    u2 = (hx * lax.rsqrt(jnp.mean(hx * hx, axis=-1, keepdims=True) + EPS) * g2_ref[...]
          * (1.0 + sc_ref[0]) + sh_ref[0])
    u2_ref[...] = u2

    u_hi, u_lo = _split_bf16(u2)
    logits = (jnp.dot(u_hi, wrh_ref[...], preferred_element_type=jnp.float32)
              + jnp.dot(u_lo, wrh_ref[...], preferred_element_type=jnp.float32)
              + jnp.dot(u_hi, wrl_ref[...], preferred_element_type=jnp.float32)) + br_ref[...]
    lane = lax.broadcasted_iota(jnp.int32, (tm, LANES), 1)
    logits = jnp.where(lane < N_EXPERTS, logits, NEG_BIG)

    idx_out = jnp.zeros((tm, LANES), jnp.int32)
    val_out = jnp.zeros((tm, LANES), jnp.float32)
    chosen = jnp.zeros((tm, LANES), jnp.float32)
    sel = []
    top0 = None
    for k in range(TOP_K):
        mx = jnp.max(logits, axis=-1, keepdims=True)
        ix = jnp.min(jnp.where(logits == mx, lane, LANES), axis=-1, keepdims=True)
        hit = lane == ix
        if k == 0:
            top0 = mx
        idx_out = jnp.where(lane == k, ix, idx_out)
        val_out = jnp.where(lane == k, jnp.exp(mx - top0), val_out)
        chosen = jnp.where(hit, 1.0, chosen)
        sel.append(hit)
        logits = jnp.where(hit, NEG_BIG, logits)
    idx_ref[...] = idx_out
    wgt_ref[...] = val_out / jnp.sum(val_out, axis=-1, keepdims=True)

    r_idx = lax.broadcasted_iota(jnp.int32, (tm, tm), 0)
    c_idx = lax.broadcasted_iota(jnp.int32, (tm, tm), 1)
    before = jnp.where(c_idx < r_idx, 1.0, 0.0).astype(jnp.bfloat16)
    prior = jnp.dot(before, chosen.astype(jnp.bfloat16), preferred_element_type=jnp.float32) + carry_ref[...]
    rank_out = jnp.zeros((tm, LANES), jnp.int32)
    for k in range(TOP_K):
        rk = jnp.sum(jnp.where(sel[k], prior, 0.0), axis=-1, keepdims=True)
        rank_out = jnp.where(lane == k, rk.astype(jnp.int32), rank_out)
    rank_ref[...] = rank_out
    carry_ref[...] = carry_ref[...] + jnp.sum(chosen, axis=0, keepdims=True)
    cnt_ref[...] = jnp.broadcast_to(carry_ref[...], cnt_ref.shape)


def _outproj_router(z, x2, w_out, gt1, g2, sc2, sh2, wr_hi, wr_lo, br, rows_per_mod, tm=256):
    n, d = z.shape
    tiles_per_mod = rows_per_mod // tm
    row = lambda i: (i, 0)
    fixed = lambda i: (0, 0)
    modi = lambda i: (i // tiles_per_mod, 0, 0)
    return pl.pallas_call(
        _outproj_router_kernel,
        grid=(n // tm,),
        in_specs=[pl.BlockSpec((tm, d), row),
                  pl.BlockSpec((tm, d), row),
                  pl.BlockSpec((d, d), fixed),
                  pl.BlockSpec((1, 1, d), modi),
                  pl.BlockSpec((1, d), fixed),
                  pl.BlockSpec((1, 1, d), modi),
                  pl.BlockSpec((1, 1, d), modi),
                  pl.BlockSpec((d, LANES), fixed),
                  pl.BlockSpec((d, LANES), fixed),
                  pl.BlockSpec((1, LANES), fixed)],
        out_specs=[pl.BlockSpec((tm, d), row),
                   pl.BlockSpec((tm, d), row),
                   pl.BlockSpec((tm, LANES), row),
                   pl.BlockSpec((tm, LANES), row),
                   pl.BlockSpec((tm, LANES), row),
                   pl.BlockSpec((8, LANES), fixed)],
        out_shape=[jax.ShapeDtypeStruct((n, d), jnp.float32),
                   jax.ShapeDtypeStruct((n, d), jnp.float32),
                   jax.ShapeDtypeStruct((n, LANES), jnp.int32),
                   jax.ShapeDtypeStruct((n, LANES), jnp.float32),
                   jax.ShapeDtypeStruct((n, LANES), jnp.int32),
                   jax.ShapeDtypeStruct((8, LANES), jnp.float32)],
        scratch_shapes=[pltpu.VMEM((1, LANES), jnp.float32)],
        compiler_params=_cparams(("arbitrary",)),
        name="outproj_router",
    )(z, x2, w_out, gt1, g2, sc2, sh2, wr_hi, wr_lo, br)


def _start_zero_rows(zbuf, dst_rows, start, count, sem, wait):
    n_full = count // EXPERT_ROWS
    rem = count % EXPERT_ROWS
    pieces = [(i < n_full, start + i * EXPERT_ROWS, EXPERT_ROWS) for i in range(3)]
    off = start + n_full * EXPERT_ROWS
    p = EXPERT_ROWS // 2
    while p >= ROW_ALIGN:
        pieces.append(((rem & p) != 0, off + (rem // (2 * p)) * (2 * p), p))
        p //= 2
    for cond, row, size in pieces:
        @pl.when(cond)
        def _(row=row, size=size):
            cp = pltpu.make_async_copy(zbuf.at[pl.ds(0, size)], dst_rows(pl.multiple_of(row, ROW_ALIGN), size), sem)
            cp.wait() if wait else cp.start()


def _zero_rows(zbuf, dst_rows, start, count, sem):
    _start_zero_rows(zbuf, dst_rows, start, count, sem, wait=False)
    _start_zero_rows(zbuf, dst_rows, start, count, sem, wait=True)


def _dispatch_kernel(slot_ref, gap_ref, tail_ref, u_ref, xs_ref, zbuf, sem, zsem):
    tm = u_ref.shape[0]
    n_rows = xs_ref.shape[0]
    base = pl.program_id(0) * (tm * TOP_K)

    @pl.when(pl.program_id(0) == 0)
    def _():
        zbuf[...] = jnp.zeros_like(zbuf)
        for wait in (False, True):
            for e in range(N_EXPERTS):
                @pl.when(gap_ref[e] >= 0)
                def _(e=e, wait=wait):
                    cp = pltpu.make_async_copy(
                        zbuf.at[pl.ds(0, ROW_ALIGN)],
                        xs_ref.at[pl.ds(pl.multiple_of(gap_ref[e], ROW_ALIGN), ROW_ALIGN)], zsem)
                    cp.wait() if wait else cp.start()
        _zero_rows(zbuf, lambda r, s: xs_ref.at[pl.ds(r, s)], tail_ref[0], n_rows - tail_ref[0], zsem)

    def issue(r, carry):
        for k in range(TOP_K):
            s = slot_ref[base + r * TOP_K + k]
            pltpu.make_async_copy(u_ref.at[pl.ds(r, 1)], xs_ref.at[pl.ds(s, 1)], sem).start(priority=k % 2)
        return carry

    lax.fori_loop(0, tm, issue, 0)

    def drain(r, carry):
        for k in range(TOP_K):
            pltpu.make_async_copy(u_ref.at[pl.ds(0, 1)], xs_ref.at[pl.ds(0, 1)], sem).wait()
        return carry

    lax.fori_loop(0, tm, drain, 0)


def _dispatch(slots_flat, gap_rows, tail, u2, n_rows, tm=256):
    n, d = u2.shape
    return pl.pallas_call(
        _dispatch_kernel,
        grid_spec=pltpu.PrefetchScalarGridSpec(
            num_scalar_prefetch=3,
            grid=(n // tm,),
            in_specs=[pl.BlockSpec((tm, d), lambda i, *_: (i, 0))],
            out_specs=pl.BlockSpec(memory_space=pl.ANY),
            scratch_shapes=[pltpu.VMEM((EXPERT_ROWS, d), jnp.float32),
                            pltpu.SemaphoreType.DMA(()), pltpu.SemaphoreType.DMA(())]),
        out_shape=jax.ShapeDtypeStruct((n_rows, d), jnp.float32),
        compiler_params=_cparams(("arbitrary",)),
        name="moe_dispatch",
    )(slots_flat, gap_rows, tail, u2)


class _CopyGroup:
    def __init__(self, copies):
        self.copies = copies

    def start(self, priority=0):
        for cp in self.copies:
            cp.start(priority=priority)

    def wait(self):
        for cp in self.copies:
            cp.wait()


class _WeightStream:
    def __init__(self, w_hbm, wbuf, sem, col_starts):
        self.w_hbm, self.wbuf, self.sem, self.col_starts = w_hbm, wbuf, sem, col_starts
        self.tile = wbuf.shape[3]
        self.slab = wbuf.shape[2] // WEIGHT_DMA_PARTS
        n_f = pl.num_programs(1)
        step = pl.program_id(0) * n_f + pl.program_id(1)
        self.first = step == 0
        self.has_next = step + 1 < pl.num_programs(0) * n_f
        self.cur = lax.rem(step, 2)
        self.e, self.f = pl.program_id(0), pl.program_id(1)
        self.e_next, self.f_next = (step + 1) // n_f, lax.rem(step + 1, n_f)

    def _slab(self, j, e, f, buf):
        def aligned(x, m):
            return x if isinstance(x, int) else pl.multiple_of(x, m)

        rows = pl.ds(aligned(j * self.slab, self.slab), self.slab)
        return _CopyGroup([
            pltpu.make_async_copy(
                self.w_hbm.at[e, rows, pl.ds(aligned(c0 + f * self.tile, LANES), self.tile)],
                self.wbuf.at[buf, m, rows], self.sem.at[buf])
            for m, c0 in enumerate(self.col_starts)])

    def wait_current(self):
        @pl.when(self.first)
        def _():
            for j in range(WEIGHT_DMA_PARTS):
                self._slab(j, 0, 0, 0).start()

        for j in range(WEIGHT_DMA_PARTS):
            self._slab(j, self.e, self.f, self.cur).wait()

    def start_next_slab(self, j):
        @pl.when(self.has_next)
        def _():
            self._slab(j, self.e_next, self.f_next, 1 - self.cur).start()

    def current(self, m):
        return self.wbuf[self.cur, m]


def _expert_rows_loop(start_ref, nblk_ref, tail_ref, order_ref, src_ref, dst_ref, ibuf, obuf, zbuf,
                      sem_in, sem_out, sem_z, weights, prepare, compute):
    e = pl.program_id(0)
    f = pl.program_id(1)
    n_f = pl.num_programs(1)
    tile = obuf.shape[2]
    col = pl.multiple_of(f * tile, tile)
    nb = nblk_ref[e]
    base = start_ref[e]
    g0 = order_ref[0, e] + f * nb
    nxt = order_ref[1, e]

    part = EXPERT_ROWS // BLOCK_DMA_PARTS

    def rows(first_row, j):
        return pl.ds(pl.multiple_of(first_row + j * part, ROW_ALIGN), part)

    def in_copy(first_row, slot):
        return _CopyGroup([pltpu.make_async_copy(src_ref.at[rows(first_row, j)],
                                                 ibuf.at[slot, pl.ds(j * part, part)], sem_in.at[slot])
                           for j in range(BLOCK_DMA_PARTS)])

    def out_copy(k, slot):
        return _CopyGroup([pltpu.make_async_copy(obuf.at[slot, pl.ds(j * part, part)],
                                                 dst_ref.at[rows(base + k * EXPERT_ROWS, j), pl.ds(col, tile)],
                                                 sem_out.at[slot])
                           for j in range(BLOCK_DMA_PARTS)])

    @pl.when((nb > 0) & (g0 == 0))
    def _():
        in_copy(base, 0).start(priority=ROW_DMA_PRIORITY)

    weights.wait_current()
    prepare()

    def body(k, carry):
        slot = lax.rem(g0 + k, 2)
        oslot = lax.rem(k, 2)

        @pl.when(k + 1 < nb)
        def _():
            in_copy(base + (k + 1) * EXPERT_ROWS, 1 - slot).start(priority=ROW_DMA_PRIORITY)

        @pl.when((k + 1 == nb) & (f + 1 < n_f))
        def _():
            in_copy(base, 1 - slot).start(priority=ROW_DMA_PRIORITY)

        @pl.when((k + 1 == nb) & (f + 1 == n_f) & (nxt >= 0))
        def _():
            in_copy(start_ref[jnp.maximum(nxt, 0)], 1 - slot).start(priority=ROW_DMA_PRIORITY)

        @pl.when(k < WEIGHT_DMA_PARTS)
        def _():
            weights.start_next_slab(k)

        in_copy(base, slot).wait()

        @pl.when(k >= 2)
        def _():
            out_copy(k - 2, oslot).wait()

        obuf[oslot] = compute(ibuf[slot]).astype(obuf.dtype)
        out_copy(k, oslot).start()
        return carry

    lax.fori_loop(0, nb, body, 0)

    def rest(j, carry):
        weights.start_next_slab(j)
        return carry

    lax.fori_loop(jnp.minimum(nb, WEIGHT_DMA_PARTS), WEIGHT_DMA_PARTS, rest, 0)

    @pl.when(nb >= 2)
    def _():
        out_copy(nb - 2, lax.rem(nb, 2)).wait()

    @pl.when(nb >= 1)
    def _():
        out_copy(nb - 1, lax.rem(nb + 1, 2)).wait()

    @pl.when(e == pl.num_programs(0) - 1)
    def _():
        zbuf[...] = jnp.zeros_like(zbuf)
        _zero_rows(zbuf, lambda r, s: dst_ref.at[pl.ds(r, s), pl.ds(col, tile)],
                   tail_ref[1], dst_ref.shape[0] - tail_ref[1], sem_z)


def _expert_up_kernel(start_ref, nblk_ref, tail_ref, order_ref, xs_ref, w_ref, bg_ref, bu_ref, hid_ref,
                      wbuf, wg_s, wu_s, ibuf, obuf, zbuf, sem_w, sem_in, sem_out, sem_z):
    weights = _WeightStream(w_ref, wbuf, sem_w, (0, D_FF))

    def prepare():
        wg_s[...] = weights.current(0).astype(jnp.bfloat16)
        wu_s[...] = weights.current(1).astype(jnp.bfloat16)

    def compute(xb):
        x = xb.astype(jnp.bfloat16)
        gate = jnp.dot(x, wg_s[...], preferred_element_type=jnp.float32) + bg_ref[...]
        up = jnp.dot(x, wu_s[...], preferred_element_type=jnp.float32) + bu_ref[...]
        gate = jnp.minimum(gate, SWIGLU_LIMIT)
        up = jnp.clip(up, -SWIGLU_LIMIT, SWIGLU_LIMIT)
        return (up + 1.0) * gate * jax.nn.sigmoid(SWIGLU_ALPHA * gate)

    _expert_rows_loop(start_ref, nblk_ref, tail_ref, order_ref, xs_ref, hid_ref, ibuf, obuf, zbuf,
                      sem_in, sem_out, sem_z, weights, prepare, compute)


def _expert_up(meta, xs, w_gu, b_gu):
    n_rows, d = xs.shape
    tf = D_FF // UP_COL_TILES
    nfc = UP_COL_TILES
    b3 = b_gu.reshape(N_EXPERTS, 1, 2 * D_FF)
    return pl.pallas_call(
        _expert_up_kernel,
        grid_spec=pltpu.PrefetchScalarGridSpec(
            num_scalar_prefetch=4,
            grid=(N_EXPERTS, nfc),
            in_specs=[pl.BlockSpec(memory_space=pl.ANY),
                      pl.BlockSpec(memory_space=pl.ANY),
                      pl.BlockSpec((None, 1, tf), lambda e, f, *_: (e, 0, f)),
                      pl.BlockSpec((None, 1, tf), lambda e, f, *_: (e, 0, nfc + f))],
            out_specs=pl.BlockSpec(memory_space=pl.ANY),
            scratch_shapes=[pltpu.VMEM((2, 2, d, tf), jnp.float32),
                            pltpu.VMEM((d, tf), jnp.bfloat16), pltpu.VMEM((d, tf), jnp.bfloat16),
                            pltpu.VMEM((2, EXPERT_ROWS, d), jnp.float32),
                            pltpu.VMEM((2, EXPERT_ROWS, tf), jnp.bfloat16),
                            pltpu.VMEM((EXPERT_ROWS, tf), jnp.bfloat16),
                            pltpu.SemaphoreType.DMA((2,)),
                            pltpu.SemaphoreType.DMA((2,)), pltpu.SemaphoreType.DMA((2,)),
                            pltpu.SemaphoreType.DMA(())]),
        out_shape=jax.ShapeDtypeStruct((n_rows, D_FF), jnp.bfloat16),
        compiler_params=_cparams(("arbitrary", "arbitrary")),
        name="expert_up",
    )(*meta, xs, w_gu, b3, b3)


def _expert_down_kernel(start_ref, nblk_ref, tail_ref, order_ref, hid_ref, w_ref, bd_ref, ys_ref,
                        wbuf, wd_s, ibuf, obuf, zbuf, sem_w, sem_in, sem_out, sem_z):
    weights = _WeightStream(w_ref, wbuf, sem_w, (0,))

    def prepare():
        wd_s[...] = weights.current(0).astype(jnp.bfloat16)

    def compute(hb):
        return jnp.dot(hb, wd_s[...], preferred_element_type=jnp.float32) + bd_ref[...]

    _expert_rows_loop(start_ref, nblk_ref, tail_ref, order_ref, hid_ref, ys_ref, ibuf, obuf, zbuf,
                      sem_in, sem_out, sem_z, weights, prepare, compute)


def _expert_down(meta, hid, w_dn, b_dn):
    n_rows, dff = hid.shape
    d = w_dn.shape[2]
    tn = d // DOWN_COL_TILES
    b3 = b_dn.reshape(N_EXPERTS, 1, d)
    return pl.pallas_call(
        _expert_down_kernel,
        grid_spec=pltpu.PrefetchScalarGridSpec(
            num_scalar_prefetch=4,
            grid=(N_EXPERTS, d // tn),
            in_specs=[pl.BlockSpec(memory_space=pl.ANY),
                      pl.BlockSpec(memory_space=pl.ANY),
                      pl.BlockSpec((None, 1, tn), lambda e, f, *_: (e, 0, f))],
            out_specs=pl.BlockSpec(memory_space=pl.ANY),
            scratch_shapes=[pltpu.VMEM((2, 1, dff, tn), jnp.float32),
                            pltpu.VMEM((dff, tn), jnp.bfloat16),
                            pltpu.VMEM((2, EXPERT_ROWS, dff), jnp.bfloat16),
                            pltpu.VMEM((2, EXPERT_ROWS, tn), jnp.float32),
                            pltpu.VMEM((EXPERT_ROWS, tn), jnp.float32),
                            pltpu.SemaphoreType.DMA((2,)),
                            pltpu.SemaphoreType.DMA((2,)), pltpu.SemaphoreType.DMA((2,)),
                            pltpu.SemaphoreType.DMA(())]),
        out_shape=jax.ShapeDtypeStruct((n_rows, d), jnp.float32),
        compiler_params=_cparams(("arbitrary", "arbitrary")),
        name="expert_down",
    )(*meta, hid, w_dn, b3)


def _row_layout(counts, n_assign):
    c_al = (counts + ROW_ALIGN - 1) // ROW_ALIGN * ROW_ALIGN
    start = jnp.cumsum(c_al) - c_al
    total = jnp.sum(c_al)
    nblk = (counts + EXPERT_ROWS - 1) // EXPERT_ROWS
    covered = jnp.max(start + nblk * EXPERT_ROWS)
    gap_rows = jnp.where(counts > 0, start + c_al - ROW_ALIGN, -1)
    n_rows = n_assign + N_EXPERTS * ROW_ALIGN + EXPERT_ROWS
    i32 = lambda a: a.astype(jnp.int32)
    tail = i32(jnp.stack([total, covered]))
    blocks_before = jnp.cumsum(nblk) - nblk
    ids = jnp.arange(N_EXPERTS)
    later = lax.cummin(jnp.where(nblk > 0, ids, N_EXPERTS), reverse=True)
    nxt = jnp.concatenate([later[1:], jnp.full((1,), N_EXPERTS)])
    nxt = jnp.where(nxt < N_EXPERTS, nxt, -1)

    def meta(col_tiles):
        return i32(start), i32(nblk), tail, i32(jnp.stack([col_tiles * blocks_before, nxt]))

    return meta(UP_COL_TILES), meta(DOWN_COL_TILES), i32(gap_rows), n_rows


def _combine_kernel(slot_ref, ys_ref, hx_ref, wgt_ref, gt_ref, o_ref, buf, sem):
    tm = hx_ref.shape[0]
    i = pl.program_id(0)
    n_i = pl.num_programs(0)

    def issue(tile, b):
        base = tile * (tm * TOP_K)

        def body(r, carry):
            for k in range(TOP_K):
                s = slot_ref[base + r * TOP_K + k]
                pltpu.make_async_copy(ys_ref.at[pl.ds(s, 1)], buf.at[b, k, pl.ds(r, 1)],
                                      sem.at[b]).start(priority=k % 2)
            return carry

        lax.fori_loop(0, tm, body, 0)

    @pl.when(i == 0)
    def _():
        issue(0, 0)

    @pl.when(i + 1 < n_i)
    def _():
        issue(i + 1, (i + 1) % 2)

    cur = i % 2

    def drain(r, carry):
        for k in range(TOP_K):
            pltpu.make_async_copy(ys_ref.at[pl.ds(0, 1)], buf.at[cur, k, pl.ds(0, 1)], sem.at[cur]).wait()
        return carry

    lax.fori_loop(0, tm, drain, 0)

    wgt = wgt_ref[...]
    acc = wgt[:, 0:1] * buf[cur, 0]
    for k in range(1, TOP_K):
        acc = acc + wgt[:, k:k + 1] * buf[cur, k]
    o_ref[...] = hx_ref[...] + gt_ref[0] * acc


def _combine(slots_flat, ys, hx, wgt, gt2, rows_per_mod, tm=128):
    n, d = hx.shape
    tiles_per_mod = rows_per_mod // tm
    return pl.pallas_call(
        _combine_kernel,
        grid_spec=pltpu.PrefetchScalarGridSpec(
            num_scalar_prefetch=1,
            grid=(n // tm,),
            in_specs=[pl.BlockSpec(memory_space=pl.ANY),
                      pl.BlockSpec((tm, d), lambda i, s: (i, 0)),
                      pl.BlockSpec((tm, LANES), lambda i, s: (i, 0)),
                      pl.BlockSpec((1, 1, d), lambda i, s: (i // tiles_per_mod, 0, 0))],
            out_specs=pl.BlockSpec((tm, d), lambda i, s: (i, 0)),
            scratch_shapes=[pltpu.VMEM((2, TOP_K, tm, d), jnp.float32),
                            pltpu.SemaphoreType.DMA((2,))]),
        out_shape=jax.ShapeDtypeStruct((n, d), jnp.float32),
        compiler_params=_cparams(("arbitrary",)),
        name="moe_combine",
    )(slots_flat, ys, hx, wgt, gt2)


def _rope_tables(n_t):
    rows = n_t // GRID_W
    row_ids = jnp.repeat(jnp.arange(rows), GRID_W).astype(jnp.float32)
    col_ids = jnp.tile(jnp.arange(GRID_W), rows).astype(jnp.float32)
    freqs = jnp.exp(-math.log(ROPE_THETA) * jnp.arange(ROPE_PAIRS_AXIS, dtype=jnp.float32) / ROPE_PAIRS_AXIS)
    ang = jnp.concatenate([row_ids[:, None] * freqs, col_ids[:, None] * freqs], axis=-1)
    cos_e = jnp.repeat(jnp.cos(ang), 2, axis=-1)
    sin = jnp.sin(ang)
    sin_s = jnp.stack([-sin, sin], axis=-1).reshape(n_t, HEAD_DIM)
    return cos_e, sin_s


def _gate_layouts(og, n_b, n_t):
    g = og[:, :4 * M_HEADS].reshape(n_b, n_t, 4, M_HEADS)
    return g.transpose(0, 3, 2, 1), g.transpose(0, 3, 1, 2)


def _layer(x, c, ctx, c_ctx, w_mod, b_mod, g_norm1, g_norm2, w_in, b_in, g_q, g_k, g_mh,
           w_br_m, w_br_a, w_out, w_router, b_router, w_gu, b_gu, w_dn, b_dn):
    n_b, n_t, d = x.shape
    n_c = ctx.shape[1]
    bf = jnp.bfloat16

    b_in2 = b_in.reshape(1, _F_IN)
    wr = jnp.pad(w_router, ((0, 0), (0, LANES - N_EXPERTS)))
    wr_hi = wr.astype(bf)
    wr_lo = (wr - wr_hi.astype(jnp.float32)).astype(bf)
    br = jnp.pad(b_router, (0, LANES - N_EXPERTS)).reshape(1, LANES)

    c8 = jnp.zeros((8, d), jnp.float32).at[:n_b].set(c).at[n_b].set(c_ctx)
    mod = _modulation(c8, w_mod, b_mod)
    mod6 = mod.reshape(8, 6, d)
    sh1x, sc1x, gt1x, sh2x, sc2x, gt2x = [mod6[:n_b, i].reshape(n_b, 1, d) for i in range(6)]
    sh1c, sc1c = [mod6[n_b:n_b + 1, i].reshape(1, 1, d) for i in range(2)]

    g1 = g_norm1.reshape(1, d)
    x2 = x.reshape(n_b * n_t, d)
    w_in_t = w_in.T
    b_gate = b_in[_O_GATES:_O_AQ].reshape(1, GATE_COLS)
    ux, ogx = _adaln_norm(x2, g1, sc1x, sh1x, w_in_t, b_gate, rows_per_mod=n_t)
    uc, ogc = _adaln_norm(ctx.reshape(n_b * n_c, d), g1, sc1c, sh1c, w_in_t, b_gate, rows_per_mod=n_b * n_c)
    px2 = _in_projection(ux, w_in_t, b_in2, jnp.arange(_P_COLS // PROJ_TN, dtype=jnp.int32))
    ctx_tiles = jnp.array([_P_MK // PROJ_TN, _P_MV // PROJ_TN, _P_MV // PROJ_TN + 1, _P_AK // PROJ_TN], jnp.int32)
    pc2 = _in_projection(uc, w_in_t, b_in2, ctx_tiles)
    px = px2.reshape(n_b, n_t, _P_COLS)
    pc = pc2.reshape(n_b, n_c, _C_COLS)

    cos_e, sin_s = _rope_tables(n_t)
    a_out = _attention(px, pc, cos_e, sin_s, g_q.reshape(1, HEAD_DIM), g_k.reshape(1, HEAD_DIM))

    g_rows, g_cols = _gate_layouts(ogx, n_b, n_t)
    g_rows_c, g_cols_c = _gate_layouts(ogc, n_b, n_c)
    h_fwd = _mlstm(px, pc, g_rows, g_cols, g_rows_c, g_cols_c, backward=False)
    m_out = _mlstm(px, pc, g_rows, g_cols, g_rows_c, g_cols_c, backward=True,
                   h_fwd=h_fwd, g_mh=g_mh.reshape(1, M_HEADS * MV_DIM))

    n = n_b * n_t
    z = _merge(m_out.reshape(n, d), a_out.reshape(n, d), px2, w_br_m.astype(bf), w_br_a.astype(bf))
    hx, u2, top_idx, top_w, rank, cnt = _outproj_router(
        z, x2, w_out.astype(bf), gt1x, g_norm2.reshape(1, d), sc2x, sh2x, wr_hi, wr_lo, br, rows_per_mod=n_t)

    counts = cnt[0, :N_EXPERTS].astype(jnp.int32)
    meta_up, meta_dn, gap_rows, n_rows = _row_layout(counts, n * TOP_K)
    slots = (meta_up[0][top_idx[:, :TOP_K]] + rank[:, :TOP_K]).astype(jnp.int32).reshape(n * TOP_K)

    xs = _dispatch(slots, gap_rows, meta_up[2], u2, n_rows)
    hid = _expert_up(meta_up, xs, w_gu, b_gu)
    ys = _expert_down(meta_dn, hid, w_dn, b_dn)
    out = _combine(slots, ys, hx, top_w, gt2x, rows_per_mod=n_t)
    return out.reshape(n_b, n_t, d)


def kernel(x, c, ctx, c_ctx, w_mod, b_mod, g_norm1, g_norm2, w_in, b_in, g_q, g_k, g_mh, w_br_m, w_br_a, w_out,
           w_router, b_router, w_gu, b_gu, w_dn, b_dn):
    assert w_mod.shape[0] == 1, "single layer: the context stream has no consumer after it"
    return _layer(x, c, ctx, c_ctx, w_mod[0], b_mod[0], g_norm1[0], g_norm2[0], w_in[0], b_in[0], g_q[0], g_k[0],
                  g_mh[0], w_br_m[0], w_br_a[0], w_out[0], w_router[0], b_router[0], w_gu[0], b_gu[0],
                  w_dn[0], b_dn[0])
```

```python
import functools
import math

import jax
import jax.numpy as jnp
from jax import lax
from jax.experimental import pallas as pl
from jax.experimental.pallas import tpu as pltpu

D_MODEL = 2048
GRID_W = 64
HEAD_DIM = 128
A_HEADS = 16
KV_HEADS = 4
Q_GROUP = A_HEADS // KV_HEADS
ROPE_THETA = 10000.0
ROPE_PAIRS_AXIS = HEAD_DIM // 4
M_HEADS = 4
MV_DIM = D_MODEL // M_HEADS
MQK_DIM = MV_DIM // 2
N_EXPERTS = 32
TOP_K = 4
D_FF = D_MODEL
SWIGLU_LIMIT = 7.0
SWIGLU_ALPHA = 1.702
EPS = 1e-6

_O_MQ, _O_MK, _O_MV, _O_MO = 0, 1024, 2048, 4096
_O_GATES = 6144
GATE_COLS = 4 * M_HEADS
_O_AQ, _O_AK, _O_AV, _O_MG = 6160, 8208, 8720, 9232
_F_IN = 13328
_P_MQ, _P_MK, _P_MV, _P_MO, _P_AQ, _P_AK, _P_AV, _P_GM, _P_GA = (
    0, 1024, 2048, 4096, 6144, 8192, 8704, 9216, 11264)
_P_COLS = 13312
_C_MK, _C_MV, _C_AK, _C_AV = 0, 1024, 3072, 3584
_C_COLS = 4096

LANES = 128
M_CHUNK = 256
EXPERT_ROWS = 256
ROW_ALIGN = 16
BLOCK_DMA_PARTS = 4
WEIGHT_DMA_PARTS = 8
ROW_DMA_PRIORITY = 1
UP_COL_TILES = 2
DOWN_COL_TILES = 1
NEG_BIG = -1e30
VMEM_LIMIT = 56 * 1024 * 1024

_HI = lax.Precision.HIGHEST


def _cparams(sem, vmem=VMEM_LIMIT):
    return pltpu.CompilerParams(dimension_semantics=sem, vmem_limit_bytes=vmem)


def _mod_kernel(c_ref, w_ref, b_ref, o_ref):
    c = c_ref[...]
    a = c * jax.nn.sigmoid(c)
    o_ref[...] = lax.dot_general(a, w_ref[...], (((1,), (0,)), ((), ())), precision=_HI,
                                 preferred_element_type=jnp.float32) + b_ref[...]


def _modulation(c8, w_mod, b_mod):
    d, n = w_mod.shape
    tn = 1024
    return pl.pallas_call(
        _mod_kernel,
        grid=(n // tn,),
        in_specs=[pl.BlockSpec((8, d), lambda j: (0, 0)),
                  pl.BlockSpec((d, tn), lambda j: (0, j)),
                  pl.BlockSpec((1, tn), lambda j: (0, j))],
        out_specs=pl.BlockSpec((8, tn), lambda j: (0, j)),
        out_shape=jax.ShapeDtypeStruct((8, n), jnp.float32),
        compiler_params=_cparams(("arbitrary",)),
        name="modulation",
    )(c8, w_mod, b_mod.reshape(1, n))


def _adaln_kernel(x_ref, g_ref, sc_ref, sh_ref, wg_ref, bg_ref, u_ref, og_ref):
    x = x_ref[...]
    y = x * lax.rsqrt(jnp.mean(x * x, axis=-1, keepdims=True) + EPS) * g_ref[...]
    u = y * (1.0 + sc_ref[0]) + sh_ref[0]
    u_ref[...] = u.astype(jnp.bfloat16)
    og_ref[...] = lax.dot_general(u, wg_ref[...], (((1,), (1,)), ((), ())), precision=_HI,
                                  preferred_element_type=jnp.float32) + bg_ref[...]


def _adaln_norm(x2, g, sc, sh, w_in_t, b_gate, rows_per_mod, tm=512):
    n, d = x2.shape
    tiles_per_mod = rows_per_mod // tm
    gate_blk = _O_GATES // GATE_COLS
    return pl.pallas_call(
        _adaln_kernel,
        grid=(n // tm,),
        in_specs=[pl.BlockSpec((tm, d), lambda i: (i, 0)),
                  pl.BlockSpec((1, d), lambda i: (0, 0)),
                  pl.BlockSpec((1, 1, d), lambda i: (i // tiles_per_mod, 0, 0)),
                  pl.BlockSpec((1, 1, d), lambda i: (i // tiles_per_mod, 0, 0)),
                  pl.BlockSpec((GATE_COLS, d), lambda i: (gate_blk, 0)),
                  pl.BlockSpec((1, GATE_COLS), lambda i: (0, 0))],
        out_specs=[pl.BlockSpec((tm, d), lambda i: (i, 0)),
                   pl.BlockSpec((tm, GATE_COLS), lambda i: (i, 0))],
        out_shape=[jax.ShapeDtypeStruct((n, d), jnp.bfloat16),
                   jax.ShapeDtypeStruct((n, GATE_COLS), jnp.float32)],
        compiler_params=_cparams(("arbitrary",)),
        name="adaln_norm",
    )(x2, g, sc, sh, w_in_t, b_gate)


PROJ_TN = 1024


def _proj_kernel(tiles_ref, u_ref, w_ref, wx_ref, b_ref, bx_ref, o_ref, w_s, b_s):
    t = tiles_ref[pl.program_id(0)]
    shifted = t * PROJ_TN >= _O_GATES
    keep = PROJ_TN - GATE_COLS

    @pl.when((pl.program_id(1) == 0) & jnp.logical_not(shifted))
    def _():
        w_s[...] = w_ref[...].astype(jnp.bfloat16)
        b_s[...] = b_ref[...]

    @pl.when((pl.program_id(1) == 0) & shifted)
    def _():
        w_s[0:keep, :] = w_ref[GATE_COLS:, :].astype(jnp.bfloat16)
        w_s[keep:, :] = wx_ref[...].astype(jnp.bfloat16)
        b_s[...] = jnp.concatenate([b_ref[:, GATE_COLS:], bx_ref[:, :GATE_COLS]], axis=1)

    acc = lax.dot_general(u_ref[...], w_s[...], (((1,), (1,)), ((), ())), preferred_element_type=jnp.float32)
    o_ref[...] = (acc + b_s[...]).astype(o_ref.dtype)


def _in_projection(u, w_in_t, b_in2, tiles, tm=2048):
    n, d = u.shape
    tm = min(tm, n)
    n_tiles = tiles.shape[0]
    tn = PROJ_TN
    return pl.pallas_call(
        _proj_kernel,
        grid_spec=pltpu.PrefetchScalarGridSpec(
            num_scalar_prefetch=1,
            grid=(n_tiles, n // tm),
            in_specs=[pl.BlockSpec((tm, d), lambda j, i, t: (i, 0)),
                      pl.BlockSpec((tn, d), lambda j, i, t: (t[j], 0)),
                      pl.BlockSpec((GATE_COLS, d), lambda j, i, t: ((tn // GATE_COLS) * (t[j] + 1), 0)),
                      pl.BlockSpec((1, tn), lambda j, i, t: (0, t[j])),
                      pl.BlockSpec((1, LANES), lambda j, i, t: (0, (tn // LANES) * (t[j] + 1)))],
            out_specs=pl.BlockSpec((tm, tn), lambda j, i, t: (i, j)),
            scratch_shapes=[pltpu.VMEM((tn, d), jnp.bfloat16), pltpu.VMEM((1, tn), jnp.float32)]),
        out_shape=jax.ShapeDtypeStruct((n, n_tiles * tn), jnp.bfloat16),
        compiler_params=_cparams(("arbitrary", "arbitrary")),
        name="in_projection",
    )(tiles, u, w_in_t, w_in_t, b_in2, b_in2)


def _rms_head(x, g):
    return x * lax.rsqrt(jnp.mean(x * x, axis=-1, keepdims=True) + EPS) * g


def _rope(x, cos_e, sin_s):
    lane = lax.broadcasted_iota(jnp.int32, x.shape, 1)
    swapped = jnp.where(lane % 2 == 0, pltpu.roll(x, LANES - 1, 1), pltpu.roll(x, 1, 1))
    return x * cos_e + swapped * sin_s


def _attn_kernel(q_ref, kx_ref, vx_ref, kc_ref, vc_ref, cosq_ref, sinq_ref, cosk_ref, sink_ref,
                 gq_ref, gk_ref, o_ref, k_s, v_s):
    n_t = kx_ref.shape[1]

    @pl.when(pl.program_id(2) == 0)
    def _():
        kx = _rms_head(kx_ref[0].astype(jnp.float32), gk_ref[...])
        k_s[0:n_t, :] = _rope(kx, cosk_ref[...], sink_ref[...]).astype(jnp.bfloat16)
        k_s[n_t:, :] = _rms_head(kc_ref[0].astype(jnp.float32), gk_ref[...]).astype(jnp.bfloat16)
        v_s[0:n_t, :] = vx_ref[0]
        v_s[n_t:, :] = vc_ref[0]

    scale = HEAD_DIM ** -0.5
    kk = k_s[...]
    vv = v_s[...]
    for g in range(Q_GROUP):
        q = q_ref[0, :, g * HEAD_DIM:(g + 1) * HEAD_DIM].astype(jnp.float32)
        q = _rope(_rms_head(q, gq_ref[...]), cosq_ref[...], sinq_ref[...]) * scale
        s = lax.dot_general(q.astype(jnp.bfloat16), kk, (((1,), (1,)), ((), ())),
                            preferred_element_type=jnp.float32)
        p = jnp.exp(s - jnp.max(s, axis=-1, keepdims=True))
        l = jnp.sum(p, axis=-1, keepdims=True)
        o = jnp.dot(p.astype(jnp.bfloat16), vv, preferred_element_type=jnp.float32) / l
        o_ref[0, :, g * HEAD_DIM:(g + 1) * HEAD_DIM] = o.astype(o_ref.dtype)


def _attention(px, pc, cos_e, sin_s, g_q, g_k, tq=256):
    n_b, n_t, _ = px.shape
    n_c = pc.shape[1]
    gw = Q_GROUP * HEAD_DIM
    return pl.pallas_call(
        _attn_kernel,
        grid=(n_b, KV_HEADS, n_t // tq),
        in_specs=[pl.BlockSpec((1, tq, gw), lambda b, h, i: (b, i, _P_AQ // gw + h)),
                  pl.BlockSpec((1, n_t, HEAD_DIM), lambda b, h, i: (b, 0, _P_AK // HEAD_DIM + h)),
                  pl.BlockSpec((1, n_t, HEAD_DIM), lambda b, h, i: (b, 0, _P_AV // HEAD_DIM + h)),
                  pl.BlockSpec((1, n_c, HEAD_DIM), lambda b, h, i: (b, 0, _C_AK // HEAD_DIM + h)),
                  pl.BlockSpec((1, n_c, HEAD_DIM), lambda b, h, i: (b, 0, _C_AV // HEAD_DIM + h)),
                  pl.BlockSpec((tq, HEAD_DIM), lambda b, h, i: (i, 0)),
                  pl.BlockSpec((tq, HEAD_DIM), lambda b, h, i: (i, 0)),
                  pl.BlockSpec((n_t, HEAD_DIM), lambda b, h, i: (0, 0)),
                  pl.BlockSpec((n_t, HEAD_DIM), lambda b, h, i: (0, 0)),
                  pl.BlockSpec((1, HEAD_DIM), lambda b, h, i: (0, 0)),
                  pl.BlockSpec((1, HEAD_DIM), lambda b, h, i: (0, 0))],
        out_specs=pl.BlockSpec((1, tq, gw), lambda b, h, i: (b, i, h)),
        out_shape=jax.ShapeDtypeStruct((n_b, n_t, A_HEADS * HEAD_DIM), jnp.bfloat16),
        scratch_shapes=[pltpu.VMEM((n_t + n_c, HEAD_DIM), jnp.bfloat16),
                        pltpu.VMEM((n_t + n_c, HEAD_DIM), jnp.bfloat16)],
        compiler_params=_cparams(("arbitrary", "arbitrary", "arbitrary")),
        name="attention",
    )(px, px, px, pc, pc, cos_e, sin_s, cos_e, sin_s, g_q, g_k)


def _log_sigmoid(x):
    return jnp.minimum(x, 0.0) - jnp.log(1.0 + jnp.exp(-jnp.abs(x)))


def _mlstm_chunk(q, k, v, i_row, f_row, i_col, f_col, ct_ref, n_ref, m_ref, backward, want_h):
    n_l = k.shape[0]
    lf_row = _log_sigmoid(f_row)
    lf_col = _log_sigmoid(f_col)
    t_idx = lax.broadcasted_iota(jnp.int32, (n_l, n_l), 0)
    s_idx = lax.broadcasted_iota(jnp.int32, (n_l, n_l), 1)
    seen = (s_idx >= t_idx) if backward else (s_idx <= t_idx)
    b_col = jnp.sum(jnp.where(seen, lf_row, 0.0), axis=1, keepdims=True)
    seen_t = (t_idx >= s_idx) if backward else (t_idx <= s_idx)
    b_row = jnp.sum(jnp.where(seen_t, lf_col, 0.0), axis=0, keepdims=True)
    b_end = jnp.sum(lf_row, axis=1, keepdims=True)
    m_prev = m_ref[...]
    ct = ct_ref[...]
    n_vec = n_ref[...]
    kf = k.astype(jnp.float32)

    h = None
    if want_h:
        d_intra = jnp.where(seen, b_col - b_row + i_row, NEG_BIG)
        d_inter = b_col + m_prev
        m_t = jnp.maximum(d_inter, jnp.max(d_intra, axis=1, keepdims=True))
        qk = lax.dot_general(q, k, (((1,), (1,)), ((), ())), preferred_element_type=jnp.float32)
        s = qk * jnp.exp(d_intra - m_t)
        w_inter = jnp.exp(d_inter - m_t)
        num = (jnp.dot(s.astype(jnp.bfloat16), v, preferred_element_type=jnp.float32)
               + w_inter * jnp.dot(q, ct.astype(jnp.bfloat16), preferred_element_type=jnp.float32))
        den = (jnp.sum(s, axis=1, keepdims=True)
               + w_inter * jnp.sum(q.astype(jnp.float32) * n_vec, axis=1, keepdims=True))
        h = num / jnp.maximum(jnp.abs(den), jnp.exp(-m_t))

    g_col = b_end - b_col + i_col
    m_new = jnp.maximum(b_end + m_prev, jnp.max(g_col, axis=0, keepdims=True))
    w_s = jnp.exp(g_col - m_new)
    w_c = jnp.exp(b_end + m_prev - m_new)
    wv = (w_s * v.astype(jnp.float32)).astype(jnp.bfloat16)
    ct_ref[...] = w_c * ct + lax.dot_general(k, wv, (((0,), (0,)), ((), ())),
                                             preferred_element_type=jnp.float32)
    n_ref[...] = w_c * n_vec + jnp.sum(w_s * kf, axis=0, keepdims=True)
    m_ref[...] = m_new
    return h


def _mlstm_kernel(backward, *refs):
    if backward:
        (q_ref, k_ref, v_ref, kc_ref, vc_ref, gr_ref, gc_ref, grc_ref, gcc_ref,
         hf_ref, op_ref, gmh_ref, o_ref, ct_ref, n_ref, m_ref) = refs
    else:
        (q_ref, k_ref, v_ref, kc_ref, vc_ref, gr_ref, gc_ref, grc_ref, gcc_ref,
         o_ref, ct_ref, n_ref, m_ref) = refs
    gi, gf = (2, 3) if backward else (0, 1)
    step = pl.program_id(2)

    @pl.when(step == 0)
    def _():
        ct_ref[...] = jnp.zeros_like(ct_ref)
        n_ref[...] = jnp.zeros_like(n_ref)
        m_ref[...] = jnp.zeros_like(m_ref)
        _mlstm_chunk(None, kc_ref[0], vc_ref[0],
                     grc_ref[0, 0, gi:gi + 1, :], grc_ref[0, 0, gf:gf + 1, :],
                     gcc_ref[0, 0, :, gi:gi + 1], gcc_ref[0, 0, :, gf:gf + 1],
                     ct_ref, n_ref, m_ref, backward, False)

    @pl.when(step > 0)
    def _():
        q = (q_ref[0].astype(jnp.float32) * (MQK_DIM ** -0.5)).astype(jnp.bfloat16)
        h = _mlstm_chunk(q, k_ref[0], v_ref[0],
                         gr_ref[0, 0, gi:gi + 1, :], gr_ref[0, 0, gf:gf + 1, :],
                         gc_ref[0, 0, :, gi:gi + 1], gc_ref[0, 0, :, gf:gf + 1],
                         ct_ref, n_ref, m_ref, backward, True)
        if backward:
            ht = h + hf_ref[0]
            y = ht * lax.rsqrt(jnp.mean(ht * ht, axis=-1, keepdims=True) + EPS) * gmh_ref[...]
            o_ref[0] = (y * jax.nn.sigmoid(op_ref[0].astype(jnp.float32))).astype(o_ref.dtype)
        else:
            o_ref[0] = h


def _mlstm(px, pc, g_rows, g_cols, g_rows_c, g_cols_c, backward, h_fwd=None, g_mh=None):
    n_b, n_t, _ = px.shape
    n_c = pc.shape[1]
    n_l = M_CHUNK
    assert n_c == n_l and n_t % n_l == 0
    n_chunk = n_t // n_l

    if backward:
        def cidx(s):
            return jnp.minimum(n_chunk - s, n_chunk - 1)
    else:
        def cidx(s):
            return jnp.maximum(s - 1, 0)

    in_specs = [
        pl.BlockSpec((1, n_l, MQK_DIM), lambda b, h, s: (b, cidx(s), _P_MQ // MQK_DIM + h)),
        pl.BlockSpec((1, n_l, MQK_DIM), lambda b, h, s: (b, cidx(s), _P_MK // MQK_DIM + h)),
        pl.BlockSpec((1, n_l, MV_DIM), lambda b, h, s: (b, cidx(s), _P_MV // MV_DIM + h)),
        pl.BlockSpec((1, n_c, MQK_DIM), lambda b, h, s: (b, 0, _C_MK // MQK_DIM + h)),
        pl.BlockSpec((1, n_c, MV_DIM), lambda b, h, s: (b, 0, _C_MV // MV_DIM + h)),
        pl.BlockSpec((1, 1, 4, n_l), lambda b, h, s: (b, h, 0, cidx(s))),
        pl.BlockSpec((1, 1, n_l, 4), lambda b, h, s: (b, h, cidx(s), 0)),
        pl.BlockSpec((1, 1, 4, n_c), lambda b, h, s: (b, h, 0, 0)),
        pl.BlockSpec((1, 1, n_c, 4), lambda b, h, s: (b, h, 0, 0)),
    ]
    args = [px, px, px, pc, pc, g_rows, g_cols, g_rows_c, g_cols_c]
    if backward:
        in_specs += [
            pl.BlockSpec((1, n_l, MV_DIM), lambda b, h, s: (b, cidx(s), h)),
            pl.BlockSpec((1, n_l, MV_DIM), lambda b, h, s: (b, cidx(s), _P_MO // MV_DIM + h)),
            pl.BlockSpec((1, MV_DIM), lambda b, h, s: (0, h)),
        ]
        args += [h_fwd, px, g_mh]
        out_dtype = jnp.bfloat16
    else:
        out_dtype = jnp.float32
    return pl.pallas_call(
        functools.partial(_mlstm_kernel, backward),
        grid=(n_b, M_HEADS, n_chunk + 1),
        in_specs=in_specs,
        out_specs=pl.BlockSpec((1, n_l, MV_DIM), lambda b, h, s: (b, cidx(s), h)),
        out_shape=jax.ShapeDtypeStruct((n_b, n_t, M_HEADS * MV_DIM), out_dtype),
        scratch_shapes=[pltpu.VMEM((MQK_DIM, MV_DIM), jnp.float32),
                        pltpu.VMEM((1, MQK_DIM), jnp.float32),
                        pltpu.VMEM((1, 1), jnp.float32)],
        compiler_params=_cparams(("arbitrary", "arbitrary", "arbitrary")),
        name="mlstm_bwd" if backward else "mlstm_fwd",
    )(*args)


def _merge_kernel(m_ref, a_ref, gm_ref, ga_ref, wm_ref, wa_ref, o_ref):
    zm = jnp.dot(m_ref[...], wm_ref[...], preferred_element_type=jnp.float32)
    za = jnp.dot(a_ref[...], wa_ref[...], preferred_element_type=jnp.float32)
    z = (jax.nn.sigmoid(gm_ref[...].astype(jnp.float32)) * zm
         + jax.nn.sigmoid(ga_ref[...].astype(jnp.float32)) * za)
    o_ref[...] = z.astype(o_ref.dtype)


def _merge(m_out, a_out, px2, w_br_m, w_br_a, tm=1024, tn=512):
    n, d = m_out.shape
    return pl.pallas_call(
        _merge_kernel,
        grid=(n // tm, d // tn),
        in_specs=[pl.BlockSpec((tm, d), lambda i, j: (i, 0)),
                  pl.BlockSpec((tm, d), lambda i, j: (i, 0)),
                  pl.BlockSpec((tm, tn), lambda i, j: (i, _P_GM // tn + j)),
                  pl.BlockSpec((tm, tn), lambda i, j: (i, _P_GA // tn + j)),
                  pl.BlockSpec((d, tn), lambda i, j: (0, j)),
                  pl.BlockSpec((d, tn), lambda i, j: (0, j))],
        out_specs=pl.BlockSpec((tm, tn), lambda i, j: (i, j)),
        out_shape=jax.ShapeDtypeStruct((n, d), jnp.bfloat16),
        compiler_params=_cparams(("arbitrary", "arbitrary")),
        name="merge",
    )(m_out, a_out, px2, px2, w_br_m, w_br_a)


def _split_bf16(x):
    hi = x.astype(jnp.bfloat16)
    lo = (x - hi.astype(jnp.float32)).astype(jnp.bfloat16)
    return hi, lo


def _outproj_router_kernel(z_ref, x_ref, wo_ref, gt_ref, g2_ref, sc_ref, sh_ref, wrh_ref, wrl_ref, br_ref,
                           hx_ref, u2_ref, idx_ref, wgt_ref, rank_ref, cnt_ref, carry_ref):
    tm = z_ref.shape[0]

    @pl.when(pl.program_id(0) == 0)
    def _():
        carry_ref[...] = jnp.zeros_like(carry_ref)

    y = jnp.dot(z_ref[...], wo_ref[...], preferred_element_type=jnp.float32)
    hx = x_ref[...] + gt_ref[0] * y
    hx_ref[...] = hx
    u2 = (hx * lax.rsqrt(jnp.mean(hx * hx, axis=-1, keepdims=True) + EPS) * g2_ref[...]
          * (1.0 + sc_ref[0]) + sh_ref[0])
    u2_ref[...] = u2

    u_hi, u_lo = _split_bf16(u2)
    logits = (jnp.dot(u_hi, wrh_ref[...], preferred_element_type=jnp.float32)
              + jnp.dot(u_lo, wrh_ref[...], preferred_element_type=jnp.float32)
              + jnp.dot(u_hi, wrl_ref[...], preferred_element_type=jnp.float32)) + br_ref[...]
    lane = lax.broadcasted_iota(jnp.int32, (tm, LANES), 1)
    logits = jnp.where(lane < N_EXPERTS, logits, NEG_BIG)

    idx_out = jnp.zeros((tm, LANES), jnp.int32)
    val_out = jnp.zeros((tm, LANES), jnp.float32)
    chosen = jnp.zeros((tm, LANES), jnp.float32)
    sel = []
    top0 = None
    for k in range(TOP_K):
        mx = jnp.max(logits, axis=-1, keepdims=True)
        ix = jnp.min(jnp.where(logits == mx, lane, LANES), axis=-1, keepdims=True)
        hit = lane == ix
        if k == 0:
            top0 = mx
        idx_out = jnp.where(lane == k, ix, idx_out)
        val_out = jnp.where(lane == k, jnp.exp(mx - top0), val_out)
        chosen = jnp.where(hit, 1.0, chosen)
        sel.append(hit)
        logits = jnp.where(hit, NEG_BIG, logits)
    idx_ref[...] = idx_out
    wgt_ref[...] = val_out / jnp.sum(val_out, axis=-1, keepdims=True)

    r_idx = lax.broadcasted_iota(jnp.int32, (tm, tm), 0)
    c_idx = lax.broadcasted_iota(jnp.int32, (tm, tm), 1)
    before = jnp.where(c_idx < r_idx, 1.0, 0.0).astype(jnp.bfloat16)
    prior = jnp.dot(before, chosen.astype(jnp.bfloat16), preferred_element_type=jnp.float32) + carry_ref[...]
    rank_out = jnp.zeros((tm, LANES), jnp.int32)
    for k in range(TOP_K):
        rk = jnp.sum(jnp.where(sel[k], prior, 0.0), axis=-1, keepdims=True)
        rank_out = jnp.where(lane == k, rk.astype(jnp.int32), rank_out)
    rank_ref[...] = rank_out
    carry_ref[...] = carry_ref[...] + jnp.sum(chosen, axis=0, keepdims=True)
    cnt_ref[...] = jnp.broadcast_to(carry_ref[...], cnt_ref.shape)


def _outproj_router(z, x2, w_out, gt1, g2, sc2, sh2, wr_hi, wr_lo, br, rows_per_mod, tm=512):
    n, d = z.shape
    tiles_per_mod = rows_per_mod // tm
    row = lambda i: (i, 0)
    fixed = lambda i: (0, 0)
    modi = lambda i: (i // tiles_per_mod, 0, 0)
    return pl.pallas_call(
        _outproj_router_kernel,
        grid=(n // tm,),
        in_specs=[pl.BlockSpec((tm, d), row),
                  pl.BlockSpec((tm, d), row),
                  pl.BlockSpec((d, d), fixed),
                  pl.BlockSpec((1, 1, d), modi),
                  pl.BlockSpec((1, d), fixed),
                  pl.BlockSpec((1, 1, d), modi),
                  pl.BlockSpec((1, 1, d), modi),
                  pl.BlockSpec((d, LANES), fixed),
                  pl.BlockSpec((d, LANES), fixed),
                  pl.BlockSpec((1, LANES), fixed)],
        out_specs=[pl.BlockSpec((tm, d), row),
                   pl.BlockSpec((tm, d), row),
                   pl.BlockSpec((tm, LANES), row),
                   pl.BlockSpec((tm, LANES), row),
                   pl.BlockSpec((tm, LANES), row),
                   pl.BlockSpec((8, LANES), fixed)],
        out_shape=[jax.ShapeDtypeStruct((n, d), jnp.float32),
                   jax.ShapeDtypeStruct((n, d), jnp.float32),
                   jax.ShapeDtypeStruct((n, LANES), jnp.int32),
                   jax.ShapeDtypeStruct((n, LANES), jnp.float32),
                   jax.ShapeDtypeStruct((n, LANES), jnp.int32),
                   jax.ShapeDtypeStruct((8, LANES), jnp.float32)],
        scratch_shapes=[pltpu.VMEM((1, LANES), jnp.float32)],
        compiler_params=_cparams(("arbitrary",)),
        name="outproj_router",
    )(z, x2, w_out, gt1, g2, sc2, sh2, wr_hi, wr_lo, br)


def _start_zero_rows(zbuf, dst_rows, start, count, sem, wait):
    n_full = count // EXPERT_ROWS
    rem = count % EXPERT_ROWS
    pieces = [(i < n_full, start + i * EXPERT_ROWS, EXPERT_ROWS) for i in range(3)]
    off = start + n_full * EXPERT_ROWS
    p = EXPERT_ROWS // 2
    while p >= ROW_ALIGN:
        pieces.append(((rem & p) != 0, off + (rem // (2 * p)) * (2 * p), p))
        p //= 2
    for cond, row, size in pieces:
        @pl.when(cond)
        def _(row=row, size=size):
            cp = pltpu.make_async_copy(zbuf.at[pl.ds(0, size)], dst_rows(pl.multiple_of(row, ROW_ALIGN), size), sem)
            cp.wait() if wait else cp.start()


def _zero_rows(zbuf, dst_rows, start, count, sem):
    _start_zero_rows(zbuf, dst_rows, start, count, sem, wait=False)
    _start_zero_rows(zbuf, dst_rows, start, count, sem, wait=True)


def _dispatch_kernel(slot_ref, gap_ref, tail_ref, u_ref, xs_ref, zbuf, sem, zsem):
    tm = u_ref.shape[0]
    n_rows = xs_ref.shape[0]
    base = pl.program_id(0) * (tm * TOP_K)

    @pl.when(pl.program_id(0) == 0)
    def _():
        zbuf[...] = jnp.zeros_like(zbuf)
        for wait in (False, True):
            for e in range(N_EXPERTS):
                @pl.when(gap_ref[e] >= 0)
                def _(e=e, wait=wait):
                    cp = pltpu.make_async_copy(
                        zbuf.at[pl.ds(0, ROW_ALIGN)],
                        xs_ref.at[pl.ds(pl.multiple_of(gap_ref[e], ROW_ALIGN), ROW_ALIGN)], zsem)
                    cp.wait() if wait else cp.start()
        _zero_rows(zbuf, lambda r, s: xs_ref.at[pl.ds(r, s)], tail_ref[0], n_rows - tail_ref[0], zsem)

    def issue(r, carry):
        for k in range(TOP_K):
            s = slot_ref[base + r * TOP_K + k]
            pltpu.make_async_copy(u_ref.at[pl.ds(r, 1)], xs_ref.at[pl.ds(s, 1)], sem).start(priority=k % 2)
        return carry

    lax.fori_loop(0, tm, issue, 0)

    def drain(r, carry):
        for k in range(TOP_K):
            pltpu.make_async_copy(u_ref.at[pl.ds(0, 1)], xs_ref.at[pl.ds(0, 1)], sem).wait()
        return carry

    lax.fori_loop(0, tm, drain, 0)


def _dispatch(slots_flat, gap_rows, tail, u2, n_rows, tm=256):
    n, d = u2.shape
    return pl.pallas_call(
        _dispatch_kernel,
        grid_spec=pltpu.PrefetchScalarGridSpec(
            num_scalar_prefetch=3,
            grid=(n // tm,),
            in_specs=[pl.BlockSpec((tm, d), lambda i, *_: (i, 0))],
            out_specs=pl.BlockSpec(memory_space=pl.ANY),
            scratch_shapes=[pltpu.VMEM((EXPERT_ROWS, d), jnp.float32),
                            pltpu.SemaphoreType.DMA(()), pltpu.SemaphoreType.DMA(())]),
        out_shape=jax.ShapeDtypeStruct((n_rows, d), jnp.float32),
        compiler_params=_cparams(("arbitrary",)),
        name="moe_dispatch",
    )(slots_flat, gap_rows, tail, u2)


class _CopyGroup:
    def __init__(self, copies):
        self.copies = copies

    def start(self, priority=0):
        for cp in self.copies:
            cp.start(priority=priority)

    def wait(self):
        for cp in self.copies:
            cp.wait()


class _WeightStream:
    def __init__(self, w_hbm, wbuf, sem, col_starts):
        self.w_hbm, self.wbuf, self.sem, self.col_starts = w_hbm, wbuf, sem, col_starts
        self.tile = wbuf.shape[3]
        self.slab = wbuf.shape[2] // WEIGHT_DMA_PARTS
        n_f = pl.num_programs(1)
        step = pl.program_id(0) * n_f + pl.program_id(1)
        self.first = step == 0
        self.has_next = step + 1 < pl.num_programs(0) * n_f
        self.cur = lax.rem(step, 2)
        self.e, self.f = pl.program_id(0), pl.program_id(1)
        self.e_next, self.f_next = (step + 1) // n_f, lax.rem(step + 1, n_f)

    def _slab(self, j, e, f, buf):
        def aligned(x, m):
            return x if isinstance(x, int) else pl.multiple_of(x, m)

        rows = pl.ds(aligned(j * self.slab, self.slab), self.slab)
        return _CopyGroup([
            pltpu.make_async_copy(
                self.w_hbm.at[e, rows, pl.ds(aligned(c0 + f * self.tile, LANES), self.tile)],
                self.wbuf.at[buf, m, rows], self.sem.at[buf])
            for m, c0 in enumerate(self.col_starts)])

    def wait_current(self):
        @pl.when(self.first)
        def _():
            for j in range(WEIGHT_DMA_PARTS):
                self._slab(j, 0, 0, 0).start()

        for j in range(WEIGHT_DMA_PARTS):
            self._slab(j, self.e, self.f, self.cur).wait()

    def start_next_slab(self, j):
        @pl.when(self.has_next)
        def _():
            self._slab(j, self.e_next, self.f_next, 1 - self.cur).start()

    def current(self, m):
        return self.wbuf[self.cur, m]


def _expert_rows_loop(start_ref, nblk_ref, tail_ref, order_ref, src_ref, dst_ref, ibuf, obuf, zbuf,
                      sem_in, sem_out, sem_z, weights, prepare, compute):
    e = pl.program_id(0)
    f = pl.program_id(1)
    n_f = pl.num_programs(1)
    tile = obuf.shape[2]
    col = pl.multiple_of(f * tile, tile)
    nb = nblk_ref[e]
    base = start_ref[e]
    g0 = order_ref[0, e] + f * nb
    nxt = order_ref[1, e]

    part = EXPERT_ROWS // BLOCK_DMA_PARTS

    def rows(first_row, j):
        return pl.ds(pl.multiple_of(first_row + j * part, ROW_ALIGN), part)

    def in_copy(first_row, slot):
        return _CopyGroup([pltpu.make_async_copy(src_ref.at[rows(first_row, j)],
                                                 ibuf.at[slot, pl.ds(j * part, part)], sem_in.at[slot])
                           for j in range(BLOCK_DMA_PARTS)])

    def out_copy(k, slot):
        return _CopyGroup([pltpu.make_async_copy(obuf.at[slot, pl.ds(j * part, part)],
                                                 dst_ref.at[rows(base + k * EXPERT_ROWS, j), pl.ds(col, tile)],
                                                 sem_out.at[slot])
                           for j in range(BLOCK_DMA_PARTS)])

    @pl.when((nb > 0) & (g0 == 0))
    def _():
        in_copy(base, 0).start(priority=ROW_DMA_PRIORITY)

    weights.wait_current()
    prepare()

    def body(k, carry):
        slot = lax.rem(g0 + k, 2)
        oslot = lax.rem(k, 2)

        @pl.when(k + 1 < nb)
        def _():
            in_copy(base + (k + 1) * EXPERT_ROWS, 1 - slot).start(priority=ROW_DMA_PRIORITY)

        @pl.when((k + 1 == nb) & (f + 1 < n_f))
        def _():
            in_copy(base, 1 - slot).start(priority=ROW_DMA_PRIORITY)

        @pl.when((k + 1 == nb) & (f + 1 == n_f) & (nxt >= 0))
        def _():
            in_copy(start_ref[jnp.maximum(nxt, 0)], 1 - slot).start(priority=ROW_DMA_PRIORITY)

        @pl.when(k < WEIGHT_DMA_PARTS)
        def _():
            weights.start_next_slab(k)

        in_copy(base, slot).wait()

        @pl.when(k >= 2)
        def _():
            out_copy(k - 2, oslot).wait()

        obuf[oslot] = compute(ibuf[slot]).astype(obuf.dtype)
        out_copy(k, oslot).start()
        return carry

    lax.fori_loop(0, nb, body, 0)

    def rest(j, carry):
        weights.start_next_slab(j)
        return carry

    lax.fori_loop(jnp.minimum(nb, WEIGHT_DMA_PARTS), WEIGHT_DMA_PARTS, rest, 0)

    @pl.when(nb >= 2)
    def _():
        out_copy(nb - 2, lax.rem(nb, 2)).wait()

    @pl.when(nb >= 1)
    def _():
        out_copy(nb - 1, lax.rem(nb + 1, 2)).wait()

    @pl.when(e == pl.num_programs(0) - 1)
    def _():
        zbuf[...] = jnp.zeros_like(zbuf)
        _zero_rows(zbuf, lambda r, s: dst_ref.at[pl.ds(r, s), pl.ds(col, tile)],
                   tail_ref[1], dst_ref.shape[0] - tail_ref[1], sem_z)


def _expert_up_kernel(start_ref, nblk_ref, tail_ref, order_ref, xs_ref, w_ref, bg_ref, bu_ref, hid_ref,
                      wbuf, wg_s, wu_s, ibuf, obuf, zbuf, sem_w, sem_in, sem_out, sem_z):
    weights = _WeightStream(w_ref, wbuf, sem_w, (0, D_FF))

    def prepare():
        wg_s[...] = weights.current(0).astype(jnp.bfloat16)
        wu_s[...] = weights.current(1).astype(jnp.bfloat16)

    def compute(xb):
        x = xb.astype(jnp.bfloat16)
        gate = jnp.dot(x, wg_s[...], preferred_element_type=jnp.float32) + bg_ref[...]
        up = jnp.dot(x, wu_s[...], preferred_element_type=jnp.float32) + bu_ref[...]
        gate = jnp.minimum(gate, SWIGLU_LIMIT)
        up = jnp.clip(up, -SWIGLU_LIMIT, SWIGLU_LIMIT)
        return (up + 1.0) * gate * jax.nn.sigmoid(SWIGLU_ALPHA * gate)

    _expert_rows_loop(start_ref, nblk_ref, tail_ref, order_ref, xs_ref, hid_ref, ibuf, obuf, zbuf,
                      sem_in, sem_out, sem_z, weights, prepare, compute)


def _expert_up(meta, xs, w_gu, b_gu):
    n_rows, d = xs.shape
    tf = D_FF // UP_COL_TILES
    nfc = UP_COL_TILES
    b3 = b_gu.reshape(N_EXPERTS, 1, 2 * D_FF)
    return pl.pallas_call(
        _expert_up_kernel,
        grid_spec=pltpu.PrefetchScalarGridSpec(
            num_scalar_prefetch=4,
            grid=(N_EXPERTS, nfc),
            in_specs=[pl.BlockSpec(memory_space=pl.ANY),
                      pl.BlockSpec(memory_space=pl.ANY),
                      pl.BlockSpec((None, 1, tf), lambda e, f, *_: (e, 0, f)),
                      pl.BlockSpec((None, 1, tf), lambda e, f, *_: (e, 0, nfc + f))],
            out_specs=pl.BlockSpec(memory_space=pl.ANY),
            scratch_shapes=[pltpu.VMEM((2, 2, d, tf), jnp.float32),
                            pltpu.VMEM((d, tf), jnp.bfloat16), pltpu.VMEM((d, tf), jnp.bfloat16),
                            pltpu.VMEM((2, EXPERT_ROWS, d), jnp.float32),
                            pltpu.VMEM((2, EXPERT_ROWS, tf), jnp.bfloat16),
                            pltpu.VMEM((EXPERT_ROWS, tf), jnp.bfloat16),
                            pltpu.SemaphoreType.DMA((2,)),
                            pltpu.SemaphoreType.DMA((2,)), pltpu.SemaphoreType.DMA((2,)),
                            pltpu.SemaphoreType.DMA(())]),
        out_shape=jax.ShapeDtypeStruct((n_rows, D_FF), jnp.bfloat16),
        compiler_params=_cparams(("arbitrary", "arbitrary")),
        name="expert_up",
    )(*meta, xs, w_gu, b3, b3)


def _expert_down_kernel(start_ref, nblk_ref, tail_ref, order_ref, hid_ref, w_ref, bd_ref, ys_ref,
                        wbuf, wd_s, ibuf, obuf, zbuf, sem_w, sem_in, sem_out, sem_z):
    weights = _WeightStream(w_ref, wbuf, sem_w, (0,))

    def prepare():
        wd_s[...] = weights.current(0).astype(jnp.bfloat16)

    def compute(hb):
        return jnp.dot(hb, wd_s[...], preferred_element_type=jnp.float32) + bd_ref[...]

    _expert_rows_loop(start_ref, nblk_ref, tail_ref, order_ref, hid_ref, ys_ref, ibuf, obuf, zbuf,
                      sem_in, sem_out, sem_z, weights, prepare, compute)


def _expert_down(meta, hid, w_dn, b_dn):
    n_rows, dff = hid.shape
    d = w_dn.shape[2]
    tn = d // DOWN_COL_TILES
    b3 = b_dn.reshape(N_EXPERTS, 1, d)
    return pl.pallas_call(
        _expert_down_kernel,
        grid_spec=pltpu.PrefetchScalarGridSpec(
            num_scalar_prefetch=4,
            grid=(N_EXPERTS, d // tn),
            in_specs=[pl.BlockSpec(memory_space=pl.ANY),
                      pl.BlockSpec(memory_space=pl.ANY),
                      pl.BlockSpec((None, 1, tn), lambda e, f, *_: (e, 0, f))],
            out_specs=pl.BlockSpec(memory_space=pl.ANY),
            scratch_shapes=[pltpu.VMEM((2, 1, dff, tn), jnp.float32),
                            pltpu.VMEM((dff, tn), jnp.bfloat16),
                            pltpu.VMEM((2, EXPERT_ROWS, dff), jnp.bfloat16),
                            pltpu.VMEM((2, EXPERT_ROWS, tn), jnp.float32),
                            pltpu.VMEM((EXPERT_ROWS, tn), jnp.float32),
                            pltpu.SemaphoreType.DMA((2,)),
                            pltpu.SemaphoreType.DMA((2,)), pltpu.SemaphoreType.DMA((2,)),
                            pltpu.SemaphoreType.DMA(())]),
        out_shape=jax.ShapeDtypeStruct((n_rows, d), jnp.float32),
        compiler_params=_cparams(("arbitrary", "arbitrary")),
        name="expert_down",
    )(*meta, hid, w_dn, b3)


def _row_layout(counts, n_assign):
    c_al = (counts + ROW_ALIGN - 1) // ROW_ALIGN * ROW_ALIGN
    start = jnp.cumsum(c_al) - c_al
    total = jnp.sum(c_al)
    nblk = (counts + EXPERT_ROWS - 1) // EXPERT_ROWS
    covered = jnp.max(start + nblk * EXPERT_ROWS)
    gap_rows = jnp.where(counts > 0, start + c_al - ROW_ALIGN, -1)
    n_rows = n_assign + N_EXPERTS * ROW_ALIGN + EXPERT_ROWS
    i32 = lambda a: a.astype(jnp.int32)
    tail = i32(jnp.stack([total, covered]))
    blocks_before = jnp.cumsum(nblk) - nblk
    ids = jnp.arange(N_EXPERTS)
    later = lax.cummin(jnp.where(nblk > 0, ids, N_EXPERTS), reverse=True)
    nxt = jnp.concatenate([later[1:], jnp.full((1,), N_EXPERTS)])
    nxt = jnp.where(nxt < N_EXPERTS, nxt, -1)

    def meta(col_tiles):
        return i32(start), i32(nblk), tail, i32(jnp.stack([col_tiles * blocks_before, nxt]))

    return meta(UP_COL_TILES), meta(DOWN_COL_TILES), i32(gap_rows), n_rows


def _combine_kernel(slot_ref, ys_ref, hx_ref, wgt_ref, gt_ref, o_ref, buf, sem):
    tm = hx_ref.shape[0]
    i = pl.program_id(0)
    n_i = pl.num_programs(0)

    def issue(tile, b):
        base = tile * (tm * TOP_K)

        def body(r, carry):
            for k in range(TOP_K):
                s = slot_ref[base + r * TOP_K + k]
                pltpu.make_async_copy(ys_ref.at[pl.ds(s, 1)], buf.at[b, k, pl.ds(r, 1)],
                                      sem.at[b]).start(priority=k % 2)
            return carry

        lax.fori_loop(0, tm, body, 0)

    @pl.when(i == 0)
    def _():
        issue(0, 0)

    @pl.when(i + 1 < n_i)
    def _():
        issue(i + 1, (i + 1) % 2)

    cur = i % 2

    def drain(r, carry):
        for k in range(TOP_K):
            pltpu.make_async_copy(ys_ref.at[pl.ds(0, 1)], buf.at[cur, k, pl.ds(0, 1)], sem.at[cur]).wait()
        return carry

    lax.fori_loop(0, tm, drain, 0)

    wgt = wgt_ref[...]
    acc = wgt[:, 0:1] * buf[cur, 0]
    for k in range(1, TOP_K):
        acc = acc + wgt[:, k:k + 1] * buf[cur, k]
    o_ref[...] = hx_ref[...] + gt_ref[0] * acc


def _combine(slots_flat, ys, hx, wgt, gt2, rows_per_mod, tm=128):
    n, d = hx.shape
    tiles_per_mod = rows_per_mod // tm
    return pl.pallas_call(
        _combine_kernel,
        grid_spec=pltpu.PrefetchScalarGridSpec(
            num_scalar_prefetch=1,
            grid=(n // tm,),
            in_specs=[pl.BlockSpec(memory_space=pl.ANY),
                      pl.BlockSpec((tm, d), lambda i, s: (i, 0)),
                      pl.BlockSpec((tm, LANES), lambda i, s: (i, 0)),
                      pl.BlockSpec((1, 1, d), lambda i, s: (i // tiles_per_mod, 0, 0))],
            out_specs=pl.BlockSpec((tm, d), lambda i, s: (i, 0)),
            scratch_shapes=[pltpu.VMEM((2, TOP_K, tm, d), jnp.float32),
                            pltpu.SemaphoreType.DMA((2,))]),
        out_shape=jax.ShapeDtypeStruct((n, d), jnp.float32),
        compiler_params=_cparams(("arbitrary",)),
        name="moe_combine",
    )(slots_flat, ys, hx, wgt, gt2)


def _rope_tables(n_t):
    rows = n_t // GRID_W
    row_ids = jnp.repeat(jnp.arange(rows), GRID_W).astype(jnp.float32)
    col_ids = jnp.tile(jnp.arange(GRID_W), rows).astype(jnp.float32)
    freqs = jnp.exp(-math.log(ROPE_THETA) * jnp.arange(ROPE_PAIRS_AXIS, dtype=jnp.float32) / ROPE_PAIRS_AXIS)
    ang = jnp.concatenate([row_ids[:, None] * freqs, col_ids[:, None] * freqs], axis=-1)
    cos_e = jnp.repeat(jnp.cos(ang), 2, axis=-1)
    sin = jnp.sin(ang)
    sin_s = jnp.stack([-sin, sin], axis=-1).reshape(n_t, HEAD_DIM)
    return cos_e, sin_s


def _gate_layouts(og, n_b, n_t):
    g = og[:, :4 * M_HEADS].reshape(n_b, n_t, 4, M_HEADS)
    return g.transpose(0, 3, 2, 1), g.transpose(0, 3, 1, 2)


def _layer(x, c, ctx, c_ctx, w_mod, b_mod, g_norm1, g_norm2, w_in, b_in, g_q, g_k, g_mh,
           w_br_m, w_br_a, w_out, w_router, b_router, w_gu, b_gu, w_dn, b_dn):
    n_b, n_t, d = x.shape
    n_c = ctx.shape[1]
    bf = jnp.bfloat16

    b_in2 = b_in.reshape(1, _F_IN)
    wr = jnp.pad(w_router, ((0, 0), (0, LANES - N_EXPERTS)))
    wr_hi = wr.astype(bf)
    wr_lo = (wr - wr_hi.astype(jnp.float32)).astype(bf)
    br = jnp.pad(b_router, (0, LANES - N_EXPERTS)).reshape(1, LANES)

    c8 = jnp.zeros((8, d), jnp.float32).at[:n_b].set(c).at[n_b].set(c_ctx)
    mod = _modulation(c8, w_mod, b_mod)
    mod6 = mod.reshape(8, 6, d)
    sh1x, sc1x, gt1x, sh2x, sc2x, gt2x = [mod6[:n_b, i].reshape(n_b, 1, d) for i in range(6)]
    sh1c, sc1c = [mod6[n_b:n_b + 1, i].reshape(1, 1, d) for i in range(2)]

    g1 = g_norm1.reshape(1, d)
    x2 = x.reshape(n_b * n_t, d)
    w_in_t = w_in.T
    b_gate = b_in[_O_GATES:_O_AQ].reshape(1, GATE_COLS)
    ux, ogx = _adaln_norm(x2, g1, sc1x, sh1x, w_in_t, b_gate, rows_per_mod=n_t)
    uc, ogc = _adaln_norm(ctx.reshape(n_b * n_c, d), g1, sc1c, sh1c, w_in_t, b_gate, rows_per_mod=n_b * n_c)
    px2 = _in_projection(ux, w_in_t, b_in2, jnp.arange(_P_COLS // PROJ_TN, dtype=jnp.int32))
    ctx_tiles = jnp.array([_P_MK // PROJ_TN, _P_MV // PROJ_TN, _P_MV // PROJ_TN + 1, _P_AK // PROJ_TN], jnp.int32)
    pc2 = _in_projection(uc, w_in_t, b_in2, ctx_tiles)
    px = px2.reshape(n_b, n_t, _P_COLS)
    pc = pc2.reshape(n_b, n_c, _C_COLS)

    cos_e, sin_s = _rope_tables(n_t)
    a_out = _attention(px, pc, cos_e, sin_s, g_q.reshape(1, HEAD_DIM), g_k.reshape(1, HEAD_DIM))

    g_rows, g_cols = _gate_layouts(ogx, n_b, n_t)
    g_rows_c, g_cols_c = _gate_layouts(ogc, n_b, n_c)
    h_fwd = _mlstm(px, pc, g_rows, g_cols, g_rows_c, g_cols_c, backward=False)
    m_out = _mlstm(px, pc, g_rows, g_cols, g_rows_c, g_cols_c, backward=True,
                   h_fwd=h_fwd, g_mh=g_mh.reshape(1, M_HEADS * MV_DIM))

    n = n_b * n_t
    z = _merge(m_out.reshape(n, d), a_out.reshape(n, d), px2, w_br_m.astype(bf), w_br_a.astype(bf))
    hx, u2, top_idx, top_w, rank, cnt = _outproj_router(
        z, x2, w_out.astype(bf), gt1x, g_norm2.reshape(1, d), sc2x, sh2x, wr_hi, wr_lo, br, rows_per_mod=n_t)

    counts = cnt[0, :N_EXPERTS].astype(jnp.int32)
    meta_up, meta_dn, gap_rows, n_rows = _row_layout(counts, n * TOP_K)
    slots = (meta_up[0][top_idx[:, :TOP_K]] + rank[:, :TOP_K]).astype(jnp.int32).reshape(n * TOP_K)

    xs = _dispatch(slots, gap_rows, meta_up[2], u2, n_rows)
    hid = _expert_up(meta_up, xs, w_gu, b_gu)
    ys = _expert_down(meta_dn, hid, w_dn, b_dn)
    out = _combine(slots, ys, hx, top_w, gt2x, rows_per_mod=n_t)
    return out.reshape(n_b, n_t, d)


def kernel(x, c, ctx, c_ctx, w_mod, b_mod, g_norm1, g_norm2, w_in, b_in, g_q, g_k, g_mh, w_br_m, w_br_a, w_out,
           w_router, b_router, w_gu, b_gu, w_dn, b_dn):
    assert w_mod.shape[0] == 1, "single layer: the context stream has no consumer after it"
    return _layer(x, c, ctx, c_ctx, w_mod[0], b_mod[0], g_norm1[0], g_norm2[0], w_in[0], b_in[0], g_q[0], g_k[0],
                  g_mh[0], w_br_m[0], w_br_a[0], w_out[0], w_router[0], b_router[0], w_gu[0], b_gu[0],
                  w_dn[0], b_dn[0])
```
